```python
import math
import jax, jax.numpy as jnp
from jax import lax
import numpy as np

D_MODEL = 4096
BATCH = 2
SEQ = 4096
DEPTH = 2
DEC_BATCH = 16
DEC_SEQ = 32
PAST_LEN = 1024

CHUNK = 64
Q_BLOCK = 128
N_BRANCH = 4
BRANCH_WIDTH = D_MODEL // 4
N_HEADS_A = 8
HEAD_DIM_A = BRANCH_WIDTH // (2 * N_HEADS_A)
DV_A = 2 * HEAD_DIM_A
WIDTH_A = N_HEADS_A * DV_A
N_HEADS_D = 8
HEAD_DIM_D = BRANCH_WIDTH // N_HEADS_D
WIDTH_D = N_HEADS_D * HEAD_DIM_D
LRU_WIDTH = BRANCH_WIDTH
LRU_BLOCKS = 8
LRU_BLOCK = LRU_WIDTH // LRU_BLOCKS
CONV_WIDTH = 4
LRU_C = 8.0
GMLP_WIDTH = BRANCH_WIDTH
GMLP_GROUPS = 8
GMLP_GROUP = GMLP_WIDTH // GMLP_GROUPS
GMLP_CHUNK = 128
D_FF = ((8 * D_MODEL // 3 + 255) // 256) * 256
HALF_STEP = 0.5
NUM_BUCKETS = 32
MAX_DISTANCE = 128
N_SUB = 3
EPS = 1e-6
NEG_INF = -1e30

OFF_AQ = 0
OFF_AK = OFF_AQ + WIDTH_A
OFF_AV = OFF_AK + WIDTH_A
OFF_DQ = OFF_AV + WIDTH_A
OFF_DK = OFF_DQ + WIDTH_D
OFF_DV = OFF_DK + WIDTH_D
OFF_DF = OFF_DV + WIDTH_D
OFF_CX = OFF_DF + N_HEADS_D
OFF_CG = OFF_CX + LRU_WIDTH
OFF_BU = OFF_CG + LRU_WIDTH
OFF_BV = OFF_BU + GMLP_WIDTH
OFF_GATE = OFF_BV + GMLP_WIDTH
N_IN = OFF_GATE + N_BRANCH * D_MODEL

kernel_name = 'streaming_hybrid_encoder_step'

F32 = jnp.float32


def rms_norm(x, g):
    xf = x.astype(F32)
    y = xf * lax.rsqrt(jnp.mean(xf * xf, axis=-1, keepdims=True) + EPS)
    return (y * g.astype(F32)).astype(x.dtype)


def t5_bucket(rel):
    nb = NUM_BUCKETS // 2
    max_exact = nb // 2
    n = jnp.abs(rel)
    large = max_exact + (jnp.log(jnp.maximum(n, 1).astype(F32) / max_exact)
                         / math.log(MAX_DISTANCE / max_exact) * (nb - max_exact)).astype(jnp.int32)
    large = jnp.minimum(large, nb - 1)
    return jnp.where(rel > 0, nb, 0) + jnp.where(n < max_exact, n, large)


def over_query_blocks(fn, q_arrays, q_pos):
    tq = q_pos.shape[0]
    if tq <= Q_BLOCK:
        return fn(q_arrays, q_pos)
    nblk = tq // Q_BLOCK
    qb = tuple(a.reshape(a.shape[0], nblk, Q_BLOCK, *a.shape[2:]).swapaxes(0, 1) for a in q_arrays)
    out = lax.map(lambda args: fn(args[0], args[1]), (qb, q_pos.reshape(nblk, Q_BLOCK)))
    out = out.swapaxes(0, 1)
    return out.reshape(out.shape[0], tq, *out.shape[3:])


def diff_attention(q, k, v, q_pos, k_pos, rel_bias, lam, lam_init, g_norm):
    scale = HEAD_DIM_A ** -0.5
    kf = k.astype(F32)
    vf = v.astype(F32)
    table = rel_bias.astype(F32)
    k_chunk = k_pos // CHUNK

    def block(qa, qp):
        (qb,) = qa
        s = jnp.einsum('bqhmd,bkhmd->bhmqk', qb.astype(F32), kf) * scale
        bias = table[t5_bucket(k_pos[None, :] - qp[:, None])]
        s = s + bias.transpose(2, 3, 0, 1)
        mask = k_chunk[None, :] <= (qp // CHUNK)[:, None]
        p = jax.nn.softmax(jnp.where(mask, s, NEG_INF), axis=-1)
        a = p[:, :, 0] - lam * p[:, :, 1]
        return jnp.einsum('bhqk,bkhd->bqhd', a, vf)

    o = over_query_blocks(block, (q,), q_pos)
    o = rms_norm(o, g_norm) * (1.0 - lam_init)
    return o.reshape(o.shape[0], o.shape[1], WIDTH_A)


def forgetting_attention(q, k, v, cq, ck, q_pos, k_pos):
    scale = HEAD_DIM_D ** -0.5
    kf = k.astype(F32)
    vf = v.astype(F32)
    ck_t = ck.transpose(0, 2, 1)[:, :, None, :]

    def block(qa, qp):
        qb, cqb = qa
        s = jnp.einsum('bqhd,bkhd->bhqk', qb.astype(F32), kf) * scale
        s = s + cqb.transpose(0, 2, 1)[..., None] - ck_t
        mask = k_pos[None, :] <= qp[:, None]
        p = jax.nn.softmax(jnp.where(mask, s, NEG_INF), axis=-1)
        return jnp.einsum('bhqk,bkhd->bqhd', p, vf)

    o = over_query_blocks(block, (q, cq), q_pos)
    return o.reshape(o.shape[0], o.shape[1], WIDTH_D)


def causal_conv(xc, buf, w, b):
    t = xc.shape[1]
    full = jnp.concatenate([buf.astype(xc.dtype), xc], axis=1)
    ff = full.astype(F32)
    out = b.astype(F32)
    for j in range(CONV_WIDTH):
        out = out + ff[:, j:j + t] * w[j].astype(F32)
    return out, full[:, t:]


def rg_lru(x, h0, w_r, b_r, w_i, b_i, lam):
    bsz, t, _ = x.shape
    xb = x.reshape(bsz, t, LRU_BLOCKS, LRU_BLOCK)
    r = jax.nn.sigmoid(jnp.einsum('btgc,gcd->btgd', xb, w_r.astype(F32)).reshape(bsz, t, LRU_WIDTH) + b_r.astype(F32))
    i = jax.nn.sigmoid(jnp.einsum('btgc,gcd->btgd', xb, w_i.astype(F32)).reshape(bsz, t, LRU_WIDTH) + b_i.astype(F32))
    log_a = -LRU_C * r * jax.nn.softplus(-lam.astype(F32))
    a = jnp.exp(log_a)
    u = jnp.sqrt(-jnp.expm1(2.0 * log_a)) * (i * x)

    def combine(left, right):
        return (left[0] * right[0], right[0] * left[1] + right[1])

    a_cum, h_zero = lax.associative_scan(combine, (a, u), axis=1)
    h = h_zero + a_cum * h0.astype(F32)[:, None]
    return h, h[:, -1]


def gmlp_sgu(u, v, ln_g, ln_b, ws, bs):
    bsz, t, _ = u.shape
    vf = v.astype(F32)
    mu = jnp.mean(vf, axis=-1, keepdims=True)
    var = jnp.mean(jnp.square(vf - mu), axis=-1, keepdims=True)
    vn = (vf - mu) * lax.rsqrt(var + EPS) * ln_g.astype(F32) + ln_b.astype(F32)
    t_pad = -(-t // GMLP_CHUNK) * GMLP_CHUNK
    vp = jnp.pad(vn, ((0, 0), (0, t_pad - t), (0, 0)))
    vp = vp.reshape(bsz, t_pad // GMLP_CHUNK, GMLP_CHUNK, GMLP_GROUPS, GMLP_GROUP)
    tril = jnp.tril(jnp.ones((GMLP_CHUNK, GMLP_CHUNK), F32))
    w = ws.astype(F32) * tril
    s = jnp.einsum('gts,bnsgc->bntgc', w, vp) + bs.astype(F32).T[:, :, None]
    s = s.reshape(bsz, t_pad, GMLP_WIDTH)[:, :t]
    return (u.astype(F32) * s).astype(u.dtype), vn.astype(v.dtype)


def token_mixers(h, l, p, a_k_past, a_v_past, d_k_past, d_v_past, d_logf_past, conv_buf, lru_h0):
    bsz, t, _ = h.shape
    past = a_k_past.shape[1]
    q_pos = past + jnp.arange(t, dtype=jnp.int32)
    k_pos = jnp.arange(past + t, dtype=jnp.int32)
    z = h @ p['w_in'][l]

    qa = z[..., OFF_AQ:OFF_AK].reshape(bsz, t, N_HEADS_A, 2, HEAD_DIM_A)
    ka = z[..., OFF_AK:OFF_AV].reshape(bsz, t, N_HEADS_A, 2, HEAD_DIM_A)
    va = z[..., OFF_AV:OFF_DQ].reshape(bsz, t, N_HEADS_A, DV_A)
    ka_all = jnp.concatenate([a_k_past.astype(ka.dtype), ka], axis=1)
    va_all = jnp.concatenate([a_v_past.astype(va.dtype), va], axis=1)
    lq = p['lambda_qk'][l].astype(F32)
    lam_init = 0.8 - 0.6 * math.exp(-0.3 * l)
    lam = jnp.exp(jnp.sum(lq[0] * lq[1])) - jnp.exp(jnp.sum(lq[2] * lq[3])) + lam_init
    o_a = diff_attention(qa, ka_all, va_all, q_pos, k_pos, p['rel_bias'], lam, lam_init, p['diff_norm_g'][l])

    qd = z[..., OFF_DQ:OFF_DK].reshape(bsz, t, N_HEADS_D, HEAD_DIM_D)
    kd = z[..., OFF_DK:OFF_DV].reshape(bsz, t, N_HEADS_D, HEAD_DIM_D)
    vd = z[..., OFF_DV:OFF_DF].reshape(bsz, t, N_HEADS_D, HEAD_DIM_D)
    logf = jax.nn.log_sigmoid(z[..., OFF_DF:OFF_CX].astype(F32) + p['b_forget'][l].astype(F32))
    kd_all = jnp.concatenate([d_k_past.astype(kd.dtype), kd], axis=1)
    vd_all = jnp.concatenate([d_v_past.astype(vd.dtype), vd], axis=1)
    cum = jnp.cumsum(jnp.concatenate([d_logf_past.astype(F32), logf], axis=1), axis=1)
    o_d = forgetting_attention(qd, kd_all, vd_all, cum[:, past:], cum, q_pos, k_pos)

    xc, new_buf = causal_conv(z[..., OFF_CX:OFF_CG], conv_buf, p['conv_w'][l], p['conv_b'][l])
    h_seq, h_last = rg_lru(xc, lru_h0, p['w_rgate'][l], p['b_rgate'][l], p['w_igate'][l], p['b_igate'][l], p['lru_lambda'][l])
    o_c = h_seq * jax.nn.gelu(z[..., OFF_CG:OFF_BU].astype(F32))

    o_b, v_rows = gmlp_sgu(jax.nn.gelu(z[..., OFF_BU:OFF_BV]), jax.nn.gelu(z[..., OFF_BV:OFF_GATE]),
                           p['gmlp_ln_g'][l], p['gmlp_ln_b'][l], p['gmlp_ws'][l], p['gmlp_bs'][l])

    gates = jax.nn.sigmoid(z[..., OFF_GATE:].reshape(bsz, t, N_BRANCH, D_MODEL))
    w_out = p['w_out'][l]
    y = jnp.zeros((bsz, t, D_MODEL), h.dtype)
    for i, o in enumerate((o_a, o_b, o_c, o_d)):
        y = y + gates[:, :, i] * (o.astype(h.dtype) @ w_out[i * BRANCH_WIDTH:(i + 1) * BRANCH_WIDTH])
    return y, (ka, va, kd, vd, logf, new_buf, h_last, v_rows)


def layer(x, c, l, p, caches):
    bsz = x.shape[0]
    mod = (jax.nn.silu(c) @ p['w_ada'][l] + p['b_ada'][l]).reshape(bsz, N_SUB, 3, 1, D_MODEL)

    def pre(xx, k):
        return rms_norm(xx, p['g_pre'][l, k]) * (1.0 + mod[:, k, 1]) + mod[:, k, 0]

    def post(xx, yy, k, coef):
        return xx + coef * mod[:, k, 2] * rms_norm(yy, p['g_post'][l, k])

    def ffn(hh, j):
        return (jax.nn.silu(hh @ p['w_ffn_gate'][l, j]) * (hh @ p['w_ffn_up'][l, j])) @ p['w_ffn_down'][l, j]

    x = post(x, ffn(pre(x, 0), 0), 0, HALF_STEP)
    y, states = token_mixers(pre(x, 1), l, p, *caches)
    x = post(x, y, 1, 1.0)
    x = post(x, ffn(pre(x, 2), 1), 2, HALF_STEP)
    return x, states


def setup_inputs(seed: int = 0) -> dict:
    key = jax.random.key(seed)
    ks = iter(jax.random.split(key, 48))

    def nrm(shape, s):
        return jax.random.normal(next(ks), shape, F32) * s

    u = jax.random.uniform(next(ks), (DEPTH, LRU_WIDTH), F32, minval=0.9, maxval=0.999)
    base = u ** (1.0 / LRU_C)
    return {
        'x_prompt': nrm((BATCH, SEQ, D_MODEL), 1.0),
        'x_sample': nrm((DEC_BATCH, DEC_SEQ, D_MODEL), 1.0),
        'c_prompt': nrm((BATCH, D_MODEL), 1.0),
        'c_sample': nrm((DEC_BATCH, D_MODEL), 1.0),
        'cache_a_k': nrm((DEPTH, DEC_BATCH, PAST_LEN, N_HEADS_A, 2, HEAD_DIM_A), 1.0),
        'cache_a_v': nrm((DEPTH, DEC_BATCH, PAST_LEN, N_HEADS_A, DV_A), 1.0),
        'cache_d_k': nrm((DEPTH, DEC_BATCH, PAST_LEN, N_HEADS_D, HEAD_DIM_D), 1.0),
        'cache_d_v': nrm((DEPTH, DEC_BATCH, PAST_LEN, N_HEADS_D, HEAD_DIM_D), 1.0),
        'cache_d_logf': jax.nn.log_sigmoid(3.0 + nrm((DEPTH, DEC_BATCH, PAST_LEN, N_HEADS_D), 1.0)),
        'state_conv': nrm((DEPTH, DEC_BATCH, CONV_WIDTH - 1, LRU_WIDTH), 1.0),
        'state_lru': nrm((DEPTH, DEC_BATCH, LRU_WIDTH), 0.5),
        'w_ada': nrm((DEPTH, D_MODEL, N_SUB * 3 * D_MODEL), D_MODEL ** -0.5),
        'b_ada': nrm((DEPTH, N_SUB * 3 * D_MODEL), 0.02),
        'g_pre': 1.0 + nrm((DEPTH, N_SUB, D_MODEL), 0.05),
        'g_post': 1.0 + nrm((DEPTH, N_SUB, D_MODEL), 0.05),
        'w_ffn_gate': nrm((DEPTH, 2, D_MODEL, D_FF), D_MODEL ** -0.5),
        'w_ffn_up': nrm((DEPTH, 2, D_MODEL, D_FF), D_MODEL ** -0.5),
        'w_ffn_down': nrm((DEPTH, 2, D_FF, D_MODEL), D_FF ** -0.5),
        'w_in': nrm((DEPTH, D_MODEL, N_IN), D_MODEL ** -0.5),
        'w_out': nrm((DEPTH, N_BRANCH * BRANCH_WIDTH, D_MODEL), (N_BRANCH * BRANCH_WIDTH) ** -0.5),
        'rel_bias': nrm((NUM_BUCKETS, N_HEADS_A, 2), 0.5),
        'lambda_qk': nrm((DEPTH, 4, HEAD_DIM_A), 0.1),
        'diff_norm_g': 1.0 + nrm((DEPTH, DV_A), 0.05),
        'gmlp_ln_g': 1.0 + nrm((DEPTH, GMLP_WIDTH), 0.05),
        'gmlp_ln_b': nrm((DEPTH, GMLP_WIDTH), 0.02),
        'gmlp_ws': nrm((DEPTH, GMLP_GROUPS, GMLP_CHUNK, GMLP_CHUNK), GMLP_CHUNK ** -0.5),
        'gmlp_bs': 1.0 + nrm((DEPTH, GMLP_GROUPS, GMLP_CHUNK), 0.05),
        'conv_w': nrm((DEPTH, CONV_WIDTH, LRU_WIDTH), CONV_WIDTH ** -0.5),
        'conv_b': nrm((DEPTH, LRU_WIDTH), 0.02),
        'w_rgate': nrm((DEPTH, LRU_BLOCKS, LRU_BLOCK, LRU_BLOCK), LRU_BLOCK ** -0.5),
        'b_rgate': nrm((DEPTH, LRU_WIDTH), 0.02),
        'w_igate': nrm((DEPTH, LRU_BLOCKS, LRU_BLOCK, LRU_BLOCK), LRU_BLOCK ** -0.5),
        'b_igate': nrm((DEPTH, LRU_WIDTH), 0.02),
        'lru_lambda': jnp.log(base) - jnp.log1p(-base),
        'b_forget': 3.0 + nrm((DEPTH, N_HEADS_D), 0.1),
    }


def reference(x_prompt, x_sample, c_prompt, c_sample, cache_a_k, cache_a_v, cache_d_k, cache_d_v,
              cache_d_logf, state_conv, state_lru, w_ada, b_ada, g_pre, g_post, w_ffn_gate, w_ffn_up,
              w_ffn_down, w_in, w_out, rel_bias, lambda_qk, diff_norm_g, gmlp_ln_g, gmlp_ln_b, gmlp_ws,
              gmlp_bs, conv_w, conv_b, w_rgate, b_rgate, w_igate, b_igate, lru_lambda, b_forget):
    p = {'w_ada': w_ada, 'b_ada': b_ada, 'g_pre': g_pre, 'g_post': g_post, 'w_ffn_gate': w_ffn_gate,
         'w_ffn_up': w_ffn_up, 'w_ffn_down': w_ffn_down, 'w_in': w_in, 'w_out': w_out,
         'rel_bias': rel_bias, 'lambda_qk': lambda_qk, 'diff_norm_g': diff_norm_g,
         'gmlp_ln_g': gmlp_ln_g, 'gmlp_ln_b': gmlp_ln_b, 'gmlp_ws': gmlp_ws, 'gmlp_bs': gmlp_bs,
         'conv_w': conv_w, 'conv_b': conv_b, 'w_rgate': w_rgate, 'b_rgate': b_rgate,
         'w_igate': w_igate, 'b_igate': b_igate, 'lru_lambda': lru_lambda, 'b_forget': b_forget}
    bp = x_prompt.shape[0]
    dt = x_prompt.dtype
    empty = (jnp.zeros((bp, 0, N_HEADS_A, 2, HEAD_DIM_A), dt), jnp.zeros((bp, 0, N_HEADS_A, DV_A), dt),
             jnp.zeros((bp, 0, N_HEADS_D, HEAD_DIM_D), dt), jnp.zeros((bp, 0, N_HEADS_D, HEAD_DIM_D), dt),
             jnp.zeros((bp, 0, N_HEADS_D), F32), jnp.zeros((bp, CONV_WIDTH - 1, LRU_WIDTH), dt),
             jnp.zeros((bp, LRU_WIDTH), F32))
    st_p = [[] for _ in range(7)]
    st_s = [[] for _ in range(8)]
    xp, xs = x_prompt, x_sample
    for l in range(DEPTH):
        xp, sp = layer(xp, c_prompt, l, p, empty)
        xs, ss = layer(xs, c_sample, l, p, (cache_a_k[l], cache_a_v[l], cache_d_k[l], cache_d_v[l],
                                            cache_d_logf[l], state_conv[l], state_lru[l]))
        for j in range(7):
            st_p[j].append(sp[j])
        for j in range(8):
            st_s[j].append(ss[j])
    np_ = [jnp.stack(s) for s in st_p]
    ns_ = [jnp.stack(s) for s in st_s]
    return (xp, xs, np_[0], np_[1], np_[2], np_[3], np_[4], np_[5], np_[6],
            ns_[0], ns_[1], ns_[2], ns_[3], ns_[4], ns_[5], ns_[6], ns_[7])
```

```python
import functools
import math

import jax
import jax.numpy as jnp
from jax import lax
from jax.experimental import pallas as pl
from jax.experimental.pallas import tpu as pltpu

F32 = jnp.float32
BF16 = jnp.bfloat16

LANE = 128
SUBLANE = 8
VMEM_LIMIT_BYTES = 56 * 1024 * 1024

CHUNK = 64
CHUNK_SHIFT = 6
N_HEADS = 8
HEAD_W = 128
HEAD_DIM_A = 64
N_BRANCH = 4
BRANCH_WIDTH = 1024
LRU_BLOCKS = 8
LRU_C = 8.0
CONV_WIDTH = 4
GMLP_GROUPS = 8
GMLP_CHUNK = 128
NUM_BUCKETS = 32
N_SUB = 3
EPS = 1e-6
NEG_INF = -1e30
T5_LARGE_STARTS = (12, 16, 23, 32, 46, 64, 91)
T5_SATURATION = 91

OFF_DF = 6 * BRANCH_WIDTH
OFF_CX = OFF_DF + N_HEADS
REST_CX, REST_CG, REST_BU, REST_BV, REST_GATE = 0, 1, 2, 3, 4


def _params(*sem):
    return pltpu.CompilerParams(dimension_semantics=sem, vmem_limit_bytes=VMEM_LIMIT_BYTES)


def _tile(n, target, mult):
    best = None
    for t in range(mult, min(n, target) + 1, mult):
        if n % t == 0:
            best = t
    return n if best is None else best


def _gelu(x):
    return x * (0.5 * (1.0 + jnp.tanh(math.sqrt(2.0 / math.pi) * (x + 0.044715 * (x * x * x)))))


def _rms(x, g):
    return x * lax.rsqrt(jnp.mean(x * x, axis=-1, keepdims=True) + EPS) * g


def _ada_kernel(c_ref, w_ref, b_ref, o_ref):
    c = c_ref[...]
    a = (c * jax.nn.sigmoid(c)).astype(BF16)
    o_ref[0] = jnp.dot(a, w_ref[0].astype(BF16), preferred_element_type=F32) + b_ref[0]


def _ada(c_all, w_ada, b_ada):
    depth, d, n = w_ada.shape
    bt = c_all.shape[0]
    tn = _tile(n, 512, LANE)
    return pl.pallas_call(
        _ada_kernel,
        grid=(depth, n // tn),
        in_specs=[pl.BlockSpec((bt, d), lambda l, j: (0, 0)),
                  pl.BlockSpec((1, d, tn), lambda l, j: (l, 0, j)),
                  pl.BlockSpec((1, 1, tn), lambda l, j: (l, 0, j))],
        out_specs=pl.BlockSpec((1, bt, tn), lambda l, j: (l, 0, j)),
        out_shape=jax.ShapeDtypeStruct((depth, bt, n), F32),
        compiler_params=_params("parallel", "parallel"),
        name="ada",
    )(c_all, w_ada, b_ada.reshape(depth, 1, n))


def _norm_kernel(*refs, k_post, coef, k_pre):
    refs = list(refs)
    x_ref = refs.pop(0)
    x = x_ref[0]
    if k_post is not None:
        y_ref, mpost_ref, gpost_ref = refs.pop(0), refs.pop(0), refs.pop(0)
    if k_pre is not None:
        mpre_ref, gpre_ref = refs.pop(0), refs.pop(0)
    if k_post is not None:
        xo_ref = refs.pop(0)
        m = mpost_ref[0]
        gate = m[3 * k_post + 2:3 * k_post + 3]
        x = x + (coef * gate) * _rms(y_ref[0], gpost_ref[k_post:k_post + 1, :])
        xo_ref[0] = x
    if k_pre is not None:
        ho_ref = refs.pop(0)
        m = mpre_ref[0]
        shift = m[3 * k_pre:3 * k_pre + 1]
        scale = m[3 * k_pre + 1:3 * k_pre + 2]
        ho_ref[0] = (_rms(x, gpre_ref[k_pre:k_pre + 1, :]) * (1.0 + scale) + shift).astype(BF16)


def _norm(x, y=None, post=None, pre=None):
    b, t, d = x.shape
    tt = _tile(t, 256, SUBLANE)
    row = pl.BlockSpec((1, tt, d), lambda i, j: (i, j, 0))
    mod_spec = pl.BlockSpec((1, 3 * N_SUB, d), lambda i, j: (i, 0, 0))
    g_spec = pl.BlockSpec((N_SUB, d), lambda i, j: (0, 0))
    args, in_specs, out_shape, out_specs = [x], [row], [], []
    if post is not None:
        args += [y, post[0], post[1]]
        in_specs += [row, mod_spec, g_spec]
        out_shape.append(jax.ShapeDtypeStruct((b, t, d), F32))
        out_specs.append(row)
    if pre is not None:
        args += [pre[0], pre[1]]
        in_specs += [mod_spec, g_spec]
        out_shape.append(jax.ShapeDtypeStruct((b, t, d), BF16))
        out_specs.append(row)
    outs = pl.pallas_call(
        functools.partial(_norm_kernel, k_post=None if post is None else post[2],
                          coef=None if post is None else post[3],
                          k_pre=None if pre is None else pre[2]),
        grid=(b, t // tt), in_specs=in_specs, out_specs=out_specs, out_shape=out_shape,
        compiler_params=_params("parallel", "parallel"),
        name="norm",
    )(*args)
    outs = list(outs)
    x_new = outs.pop(0) if post is not None else None
    h = outs.pop(0) if pre is not None else None
    return x_new, h


def _mm_kernel(x_ref, w_ref, o_ref):
    o_ref[...] = jnp.dot(x_ref[...], w_ref[0].astype(BF16),
                         preferred_element_type=F32).astype(o_ref.dtype)


def _matmul(x, w, l, col0, ncols, out_dtype, tm_target=1024, tn_target=512):
    m, k = x.shape
    tm = _tile(m, tm_target, 16)
    tn = _tile(math.gcd(ncols, col0) if col0 else ncols, tn_target, LANE)
    c0 = col0 // tn
    return pl.pallas_call(
        _mm_kernel,
        grid=(m // tm, ncols // tn),
        in_specs=[pl.BlockSpec((tm, k), lambda i, j: (i, 0)),
                  pl.BlockSpec((1, k, tn), lambda i, j: (l, 0, c0 + j))],
        out_specs=pl.BlockSpec((tm, tn), lambda i, j: (i, j)),
        out_shape=jax.ShapeDtypeStruct((m, ncols), out_dtype),
        compiler_params=_params("parallel", "arbitrary"),
        name="matmul",
    )(x, w)


def _ffn_up_kernel(x_ref, wg_ref, wu_ref, o_ref):
    x = x_ref[...]
    g = jnp.dot(x, wg_ref[0].astype(BF16), preferred_element_type=F32)
    u = jnp.dot(x, wu_ref[0].astype(BF16), preferred_element_type=F32)
    o_ref[...] = ((g * jax.nn.sigmoid(g)) * u).astype(o_ref.dtype)


def _ffn_up(x, wg, wu, l):
    m, k = x.shape
    n = wg.shape[-1]
    tm = _tile(m, 1024, 16)
    tn = _tile(n, 256, LANE)
    w_spec = pl.BlockSpec((1, k, tn), lambda i, j: (l, 0, j))
    return pl.pallas_call(
        _ffn_up_kernel,
        grid=(m // tm, n // tn),
        in_specs=[pl.BlockSpec((tm, k), lambda i, j: (i, 0)), w_spec, w_spec],
        out_specs=pl.BlockSpec((tm, tn), lambda i, j: (i, j)),
        out_shape=jax.ShapeDtypeStruct((m, n), BF16),
        compiler_params=_params("parallel", "arbitrary"),
        name="ffn_up",
    )(x, wg, wu)


def _ffn_down_kernel(x_ref, w_ref, o_ref):
    part = jnp.dot(x_ref[...], w_ref[0].astype(BF16), preferred_element_type=F32)

    @pl.when(pl.program_id(1) == 0)
    def _():
        o_ref[...] = part

    @pl.when(pl.program_id(1) != 0)
    def _():
        o_ref[...] += part


def _ffn_down(x, w, l):
    m, k = x.shape
    n = w.shape[-1]
    tm = _tile(m, 1024, 16)
    tk = _tile(k, 256, LANE)
    return pl.pallas_call(
        _ffn_down_kernel,
        grid=(m // tm, k // tk),
        in_specs=[pl.BlockSpec((tm, tk), lambda i, j: (i, j)),
                  pl.BlockSpec((1, tk, n), lambda i, j: (l, j, 0))],
        out_specs=pl.BlockSpec((tm, n), lambda i, j: (i, 0), pipeline_mode=pl.Buffered(1)),
        out_shape=jax.ShapeDtypeStruct((m, n), F32),
        compiler_params=_params("parallel", "arbitrary"),
        name="ffn_down",
    )(x, w)


def _merge_kernel(h_ref, oa_ref, ob_ref, oc_ref, od_ref, wg0, wg1, wg2, wg3,
                  wo0, wo1, wo2, wo3, y_ref):
    h = h_ref[...]
    y = None
    for o_ref, wg, wo in ((oa_ref, wg0, wo0), (ob_ref, wg1, wo1), (oc_ref, wg2, wo2), (od_ref, wg3, wo3)):
        gate = jax.nn.sigmoid(jnp.dot(h, wg[0].astype(BF16), preferred_element_type=F32))
        proj = jnp.dot(o_ref[...], wo[0].astype(BF16), preferred_element_type=F32)
        y = gate * proj if y is None else y + gate * proj
    y_ref[...] = y


def _merge(h, branches, w_rest, w_out, l):
    m, d = h.shape
    n = w_out.shape[-1]
    bw = branches[0].shape[-1]
    tm = _tile(m, 512, 16)
    tn = _tile(n, 256, LANE)
    nj = n // tn
    g0 = REST_GATE * BRANCH_WIDTH // tn

    def gate_spec(i):
        return pl.BlockSpec((1, d, tn), lambda a, j: (0, 0, g0 + i * nj + j))

    def out_spec(i):
        return pl.BlockSpec((1, bw, tn), lambda a, j: (l * N_BRANCH + i, 0, j))

    o_spec = pl.BlockSpec((tm, bw), lambda a, j: (a, 0))
    return pl.pallas_call(
        _merge_kernel,
        grid=(m // tm, nj),
        in_specs=[pl.BlockSpec((tm, d), lambda a, j: (a, 0))] + [o_spec] * N_BRANCH
        + [gate_spec(i) for i in range(N_BRANCH)] + [out_spec(i) for i in range(N_BRANCH)],
        out_specs=pl.BlockSpec((tm, tn), lambda a, j: (a, j)),
        out_shape=jax.ShapeDtypeStruct((m, n), F32),
        compiler_params=_params("parallel", "arbitrary"),
        name="merge",
    )(h, *branches, w_rest, w_rest, w_rest, w_rest, w_out, w_out, w_out, w_out)


def _bias_kernel(tab_ref, o_ref, *, tq, tk, deltas):
    h = pl.program_id(0)
    row = lax.broadcasted_iota(jnp.int32, (tq, tk), 0)
    col = lax.broadcasted_iota(jnp.int32, (tq, tk), 1)
    for c, d0 in enumerate(deltas):
        rel = d0 + col - row
        n = jnp.abs(rel)
        large = NUM_BUCKETS // 4
        for start in T5_LARGE_STARTS:
            large = large + jnp.where(n >= start, 1, 0)
        bucket = jnp.where(rel > 0, NUM_BUCKETS // 2, 0) + jnp.where(n < NUM_BUCKETS // 4, n, large)
        for mp in range(2):
            val = jnp.zeros((tq, tk), F32)
            for b in range(NUM_BUCKETS):
                val = jnp.where(bucket == b, tab_ref[b * (2 * N_HEADS) + h * 2 + mp], val)
            o_ref[0, c, mp * tq:(mp + 1) * tq, :] = val


def _bias_tiles(rel_bias, tq, tk, deltas):
    return pl.pallas_call(
        functools.partial(_bias_kernel, tq=tq, tk=tk, deltas=tuple(deltas)),
        grid=(N_HEADS,),
        in_specs=[pl.BlockSpec(memory_space=pltpu.SMEM)],
        out_specs=pl.BlockSpec((1, len(deltas), 2 * tq, tk), lambda h: (h, 0, 0, 0)),
        out_shape=jax.ShapeDtypeStruct((N_HEADS, len(deltas), 2 * tq, tk), F32),
        compiler_params=_params("parallel"),
        name="bias_tiles",
    )(rel_bias.reshape(-1))


def _attn_a_kernel(lq_ref, q_ref, k_ref, v_ref, bias_ref, g_ref, o_ref, *,
                   tq, tk, nk, ncls, past, kv_len, lam_init):
    qi = pl.program_id(2)
    lq = lq_ref[...]
    lam = (jnp.exp(jnp.sum(lq[0:1] * lq[1:2], axis=-1, keepdims=True))
           - jnp.exp(jnp.sum(lq[2:3] * lq[3:4], axis=-1, keepdims=True)) + lam_init)
    q = q_ref[0] * (HEAD_DIM_A ** -0.5)
    lane = lax.broadcasted_iota(jnp.int32, (tq, HEAD_W), 1)
    q2 = jnp.concatenate([jnp.where(lane < HEAD_DIM_A, q, 0.0),
                          jnp.where(lane >= HEAD_DIM_A, q, 0.0)], axis=0).astype(BF16)
    q0 = past + qi * tq
    row = lax.broadcasted_iota(jnp.int32, (2 * tq, tk), 0)
    q_chunk = (q0 + jnp.where(row >= tq, row - tq, row)) >> CHUNK_SHIFT
    col = lax.broadcasted_iota(jnp.int32, (2 * tq, tk), 1)
    last_key = jnp.minimum((((q0 + tq - 1) >> CHUNK_SHIFT) + 1) * CHUNK, kv_len) - 1
    n_vis = jnp.minimum(last_key // tk + 1, nk)

    def body(kt, carry):
        m, l, acc = carry
        k0 = pl.multiple_of(kt * tk, tk)
        k = k_ref[0, pl.ds(k0, tk), :].astype(BF16)
        v = v_ref[0, pl.ds(k0, tk), :].astype(BF16)
        s = lax.dot_general(q2, k, (((1,), (1,)), ((), ())), preferred_element_type=F32)
        s = s + bias_ref[0, jnp.minimum(qi - kt, ncls - 1)]
        k_pos = k0 + col
        mask = (k_pos >> CHUNK_SHIFT) <= q_chunk
        if kv_len < nk * tk:
            mask = jnp.logical_and(mask, k_pos < kv_len)
        s = jnp.where(mask, s, NEG_INF)
        m_new = jnp.maximum(m, jnp.max(s, axis=-1, keepdims=True))
        p = jnp.exp(s - m_new)
        alpha = jnp.exp(m - m_new)
        l = alpha * l + jnp.sum(p, axis=-1, keepdims=True)
        acc = alpha * acc + jnp.dot(p.astype(BF16), v, preferred_element_type=F32)
        return m_new, l, acc

    init = (jnp.full((2 * tq, 1), NEG_INF, F32), jnp.zeros((2 * tq, 1), F32),
            jnp.zeros((2 * tq, HEAD_W), F32))
    _, l, acc = lax.fori_loop(0, n_vis, body, init)
    o = acc / l
    o = o[:tq] - lam * o[tq:]
    o_ref[0] = (_rms(o, g_ref[...]) * (1.0 - lam_init)).astype(o_ref.dtype)


def _attn_a(q_arr, q_blk, k_arr, k_blk, v_arr, v_blk, bias, lq, g_norm, *, t, past, tq, tk, kv_len, lam_init):
    b = q_arr.shape[0]
    tkv = k_arr.shape[1]
    nk = tkv // tk
    ncls = bias.shape[1]
    return pl.pallas_call(
        functools.partial(_attn_a_kernel, tq=tq, tk=tk, nk=nk, ncls=ncls, past=past, kv_len=kv_len,
                          lam_init=lam_init),
        grid=(b, N_HEADS, t // tq),
        in_specs=[pl.BlockSpec((4, HEAD_DIM_A), lambda i, h, j: (0, 0)),
                  pl.BlockSpec((1, tq, HEAD_W), lambda i, h, j: (i, j, q_blk + h)),
                  pl.BlockSpec((1, tkv, HEAD_W), lambda i, h, j: (i, 0, k_blk + h)),
                  pl.BlockSpec((1, tkv, HEAD_W), lambda i, h, j: (i, 0, v_blk + h)),
                  pl.BlockSpec((1, ncls, 2 * tq, tk), lambda i, h, j: (h, 0, 0, 0)),
                  pl.BlockSpec((1, HEAD_W), lambda i, h, j: (0, 0))],
        out_specs=pl.BlockSpec((1, tq, HEAD_W), lambda i, h, j: (i, j, h)),
        out_shape=jax.ShapeDtypeStruct((b, t, N_HEADS * HEAD_W), BF16),
        compiler_params=_params("parallel", "parallel", "arbitrary"),
        name="attn_a",
    )(lq, q_arr, k_arr, v_arr, bias, g_norm.reshape(1, HEAD_W))


def _attn_d_kernel(q_ref, k_ref, v_ref, cq_ref, ck_ref, o_ref, *, tq, tk, nk, past):
    qi = pl.program_id(2)
    q = q_ref[0].astype(BF16)
    cq = cq_ref[0, 0]
    q0 = past + qi * tq
    q_pos = q0 + lax.broadcasted_iota(jnp.int32, (tq, tk), 0)
    col = lax.broadcasted_iota(jnp.int32, (tq, tk), 1)
    n_vis = jnp.minimum((q0 + tq - 1) // tk + 1, nk)
    scale = HEAD_W ** -0.5

    def body(kt, carry):
        m, l, acc = carry
        k0 = pl.multiple_of(kt * tk, tk)
        k = k_ref[0, pl.ds(k0, tk), :].astype(BF16)
        v = v_ref[0, pl.ds(k0, tk), :].astype(BF16)
        s = lax.dot_general(q, k, (((1,), (1,)), ((), ())), preferred_element_type=F32) * scale
        s = s + cq - ck_ref[0, 0, :, pl.ds(k0, tk)]
        s = jnp.where(k0 + col <= q_pos, s, NEG_INF)
        m_new = jnp.maximum(m, jnp.max(s, axis=-1, keepdims=True))
        p = jnp.exp(s - m_new)
        alpha = jnp.exp(m - m_new)
        l = alpha * l + jnp.sum(p, axis=-1, keepdims=True)
        acc = alpha * acc + jnp.dot(p.astype(BF16), v, preferred_element_type=F32)
        return m_new, l, acc

    init = (jnp.full((tq, 1), NEG_INF, F32), jnp.zeros((tq, 1), F32), jnp.zeros((tq, HEAD_W), F32))
    _, l, acc = lax.fori_loop(0, n_vis, body, init)
    o_ref[0] = (acc / l).astype(o_ref.dtype)


def _attn_d(q_arr, q_blk, k_arr, k_blk, v_arr, v_blk, cq, ck, *, t, past, tq, tk):
    b = q_arr.shape[0]
    tkv = k_arr.shape[1]
    return pl.pallas_call(
        functools.partial(_attn_d_kernel, tq=tq, tk=tk, nk=tkv // tk, past=past),
        grid=(b, N_HEADS, t // tq),
        in_specs=[pl.BlockSpec((1, tq, HEAD_W), lambda i, h, j: (i, j, q_blk + h)),
                  pl.BlockSpec((1, tkv, HEAD_W), lambda i, h, j: (i, 0, k_blk + h)),
                  pl.BlockSpec((1, tkv, HEAD_W), lambda i, h, j: (i, 0, v_blk + h)),
                  pl.BlockSpec((1, 1, tq, 1), lambda i, h, j: (i, h, j, 0)),
                  pl.BlockSpec((1, 1, 1, tkv), lambda i, h, j: (i, h, 0, 0))],
        out_specs=pl.BlockSpec((1, tq, HEAD_W), lambda i, h, j: (i, j, h)),
        out_shape=jax.ShapeDtypeStruct((b, t, N_HEADS * HEAD_W), BF16),
        compiler_params=_params("parallel", "parallel", "arbitrary"),
        name="attn_d",
    )(q_arr, k_arr, v_arr, cq, ck)


def _logf_kernel(*refs, t, p, tk_pad):
    refs = list(refs)
    zf_ref, bf_ref = refs.pop(0), refs.pop(0)
    past_ref = refs.pop(0) if p else None
    logf_ref, cum_ref = refs
    x = zf_ref[0] + bf_ref[...]
    logf = jnp.minimum(x, 0.0) - jnp.log1p(jnp.exp(-jnp.abs(x)))
    logf_ref[0] = logf
    parts = ([past_ref[0]] if p else []) + [logf]
    if tk_pad > p + t:
        parts.append(jnp.zeros((tk_pad - p - t, LANE), F32))
    cum = jnp.concatenate(parts, axis=0) if len(parts) > 1 else logf
    row = lax.broadcasted_iota(jnp.int32, (tk_pad, LANE), 0)
    s = 1
    while s < tk_pad:
        cum = cum + jnp.where(row >= s, pltpu.roll(cum, s, 0), 0.0)
        s *= 2
    cum_ref[0] = cum


def _logf(zf, b_forget, past_logf, tk_pad):
    b, t, _ = zf.shape
    p = 0 if past_logf is None else past_logf.shape[1]
    bf = jnp.pad(b_forget, (0, LANE - N_HEADS)).reshape(1, LANE)
    args = [zf, bf]
    in_specs = [pl.BlockSpec((1, t, LANE), lambda i: (i, 0, 0)), pl.BlockSpec((1, LANE), lambda i: (0, 0))]
    if p:
        args.append(jnp.pad(past_logf, ((0, 0), (0, 0), (0, LANE - N_HEADS))))
        in_specs.append(pl.BlockSpec((1, p, LANE), lambda i: (i, 0, 0)))
    return pl.pallas_call(
        functools.partial(_logf_kernel, t=t, p=p, tk_pad=tk_pad),
        grid=(b,),
        in_specs=in_specs,
        out_specs=[pl.BlockSpec((1, t, LANE), lambda i: (i, 0, 0)),
                   pl.BlockSpec((1, tk_pad, LANE), lambda i: (i, 0, 0))],
        out_shape=[jax.ShapeDtypeStruct((b, t, LANE), F32), jax.ShapeDtypeStruct((b, tk_pad, LANE), F32)],
        compiler_params=_params("parallel"),
        name="logf_cumsum",
    )(*args)


def _lru_kernel(cx_ref, cg_ref, buf0_ref, h0_ref, cw_ref, cb_ref, wr_ref, br_ref, wi_ref, bi_ref,
                lam_ref, oc_ref, nbuf_ref, hl_ref, tail_sc, h_sc, *, tt):
    w = cx_ref.shape[-1]
    blk = w // LRU_BLOCKS
    pad = SUBLANE - (CONV_WIDTH - 1)

    @pl.when(pl.program_id(1) == 0)
    def _():
        tail_sc[0:pad, :] = jnp.zeros((pad, w), F32)
        tail_sc[pad:SUBLANE, :] = buf0_ref[0]
        h_sc[...] = h0_ref[0]

    full = jnp.concatenate([tail_sc[...], cx_ref[0]], axis=0)
    cw = cw_ref[...]
    xc = cb_ref[...]
    for j in range(CONV_WIDTH):
        xc = xc + full[pad + j:pad + j + tt] * cw[j:j + 1]
    tail_sc[...] = full[tt:tt + SUBLANE]
    nbuf_ref[0] = full[tt + pad:tt + SUBLANE]

    xb = xc.astype(BF16)

    def gate(w_ref, b_ref):
        parts = [jnp.dot(xb[:, g * blk:(g + 1) * blk], w_ref[g].astype(BF16), preferred_element_type=F32)
                 for g in range(LRU_BLOCKS)]
        return jax.nn.sigmoid(jnp.concatenate(parts, axis=1) + b_ref[...])

    r = gate(wr_ref, br_ref)
    i = gate(wi_ref, bi_ref)
    nl = -lam_ref[...]
    softplus = jnp.maximum(nl, 0.0) + jnp.log1p(jnp.exp(-jnp.abs(nl)))
    log_a = (-LRU_C * r) * softplus
    a = jnp.exp(log_a)
    u = jnp.sqrt(-jnp.tanh(log_a) * (a * a + 1.0)) * (i * xc)

    row = lax.broadcasted_iota(jnp.int32, (tt, w), 0)
    s = 1
    while s < tt:
        valid = row >= s
        u = jnp.where(valid, a * pltpu.roll(u, s, 0) + u, u)
        a = jnp.where(valid, a * pltpu.roll(a, s, 0), a)
        s *= 2
    h = u + a * h_sc[...]
    h_sc[...] = h[tt - 1:tt]
    hl_ref[0] = h[tt - 1:tt]
    oc_ref[0] = (h * _gelu(cg_ref[0])).astype(oc_ref.dtype)


def _lru(zcb, conv_buf, h0, conv_w, conv_b, w_r, b_r, w_i, b_i, lam):
    b, t, _ = zcb.shape
    w = BRANCH_WIDTH
    tt = _tile(t, 256, SUBLANE)
    vec = pl.BlockSpec((1, w), lambda i, j: (0, 0))
    mat = pl.BlockSpec((LRU_BLOCKS, w // LRU_BLOCKS, w // LRU_BLOCKS), lambda i, j: (0, 0, 0))
    return pl.pallas_call(
        functools.partial(_lru_kernel, tt=tt),
        grid=(b, t // tt),
        in_specs=[pl.BlockSpec((1, tt, w), lambda i, j: (i, j, REST_CX)),
                  pl.BlockSpec((1, tt, w), lambda i, j: (i, j, REST_CG)),
                  pl.BlockSpec((1, CONV_WIDTH - 1, w), lambda i, j: (i, 0, 0)),
                  pl.BlockSpec((1, 1, w), lambda i, j: (i, 0, 0)),
                  pl.BlockSpec((CONV_WIDTH, w), lambda i, j: (0, 0)),
                  vec, mat, vec, mat, vec, vec],
        out_specs=[pl.BlockSpec((1, tt, w), lambda i, j: (i, j, 0)),
                   pl.BlockSpec((1, CONV_WIDTH - 1, w), lambda i, j: (i, 0, 0)),
                   pl.BlockSpec((1, 1, w), lambda i, j: (i, 0, 0))],
        out_shape=[jax.ShapeDtypeStruct((b, t, w), BF16),
                   jax.ShapeDtypeStruct((b, CONV_WIDTH - 1, w), F32),
                   jax.ShapeDtypeStruct((b, 1, w), F32)],
        scratch_shapes=[pltpu.VMEM((SUBLANE, w), F32), pltpu.VMEM((1, w), F32)],
        compiler_params=_params("parallel", "arbitrary"),
        name="conv_lru",
    )(zcb, zcb, conv_buf, h0.reshape(b, 1, w), conv_w, conv_b.reshape(1, w), w_r, b_r.reshape(1, w),
      w_i, b_i.reshape(1, w), lam.reshape(1, w))


def _gmlp_kernel(zu_ref, zv_ref, lng_ref, lnb_ref, ws_ref, bst_ref, ob_ref, *vn_refs, ct):
    u = _gelu(zu_ref[0])
    v = _gelu(zv_ref[0])
    w = v.shape[-1]
    gw = w // GMLP_GROUPS
    d = v - jnp.mean(v, axis=-1, keepdims=True)
    vn = d * lax.rsqrt(jnp.mean(d * d, axis=-1, keepdims=True) + EPS) * lng_ref[...] + lnb_ref[...]
    if vn_refs:
        vn_refs[0][0] = vn
    if ct < GMLP_CHUNK:
        vn = jnp.concatenate([vn, jnp.zeros((GMLP_CHUNK - ct, w), F32)], axis=0)
    vb = vn.astype(BF16)
    tril = (lax.broadcasted_iota(jnp.int32, (GMLP_CHUNK, GMLP_CHUNK), 0)
            >= lax.broadcasted_iota(jnp.int32, (GMLP_CHUNK, GMLP_CHUNK), 1))
    parts = []
    for g in range(GMLP_GROUPS):
        wg = jnp.where(tril, ws_ref[g], 0.0).astype(BF16)
        s = jnp.dot(wg, vb[:, g * gw:(g + 1) * gw], preferred_element_type=F32) + bst_ref[:, g:g + 1]
        parts.append(s[:ct])
    ob_ref[0] = (u * jnp.concatenate(parts, axis=1)).astype(ob_ref.dtype)


def _gmlp(zcb, ln_g, ln_b, ws, bs, want_rows):
    b, t, _ = zcb.shape
    w = BRANCH_WIDTH
    ct = min(t, GMLP_CHUNK)
    vec = pl.BlockSpec((1, w), lambda i, j: (0, 0))
    row_out = pl.BlockSpec((1, ct, w), lambda i, j: (i, j, 0))
    out_shape = [jax.ShapeDtypeStruct((b, t, w), BF16)]
    out_specs = [row_out]
    if want_rows:
        out_shape.append(jax.ShapeDtypeStruct((b, t, w), F32))
        out_specs.append(row_out)
    outs = pl.pallas_call(
        functools.partial(_gmlp_kernel, ct=ct),
        grid=(b, t // ct),
        in_specs=[pl.BlockSpec((1, ct, w), lambda i, j: (i, j, REST_BU)),
                  pl.BlockSpec((1, ct, w), lambda i, j: (i, j, REST_BV)),
                  vec, vec,
                  pl.BlockSpec((GMLP_GROUPS, GMLP_CHUNK, GMLP_CHUNK), lambda i, j: (0, 0, 0)),
                  pl.BlockSpec((GMLP_CHUNK, GMLP_GROUPS), lambda i, j: (0, 0))],
        out_specs=out_specs, out_shape=out_shape,
        compiler_params=_params("parallel", "parallel"),
        name="gmlp",
    )(zcb, zcb, ln_g.reshape(1, w), ln_b.reshape(1, w), ws, bs.T)
    return (outs[0], outs[1]) if want_rows else (outs[0], None)


def _mixers(h, l, p, w_rest, caches, bias, attn_cfg):
    b, t, d = h.shape
    m = b * t
    hf = h.reshape(m, d)
    a_k_past, a_v_past, d_k_past, d_v_past, d_logf_past, conv_buf, lru_h0 = caches
    past = 0 if a_k_past is None else a_k_past.shape[1]
    tq, tk, tk_pad = attn_cfg

    z_qkv = _matmul(hf, p['w_in'], l, 0, OFF_DF, F32).reshape(b, t, OFF_DF)
    zf = _matmul(hf, p['w_in'], l, OFF_DF, LANE, F32).reshape(b, t, LANE)
    zcb = _matmul(hf, w_rest, 0, 0, 4 * BRANCH_WIDTH, F32).reshape(b, t, 4 * BRANCH_WIDTH)

    bw = BRANCH_WIDTH
    ka = z_qkv[:, :, 1 * bw:2 * bw]
    va = z_qkv[:, :, 2 * bw:3 * bw]
    kd = z_qkv[:, :, 4 * bw:5 * bw]
    vd = z_qkv[:, :, 5 * bw:6 * bw]
    nblk = bw // HEAD_W

    logf, cum = _logf(zf, p['b_forget'][l], d_logf_past, tk_pad)
    logf = logf[:, :, :N_HEADS]
    cum = cum[:, :, :N_HEADS].transpose(0, 2, 1)
    cq = cum[:, :, past:past + t].reshape(b, N_HEADS, t, 1)
    ck = cum.reshape(b, N_HEADS, 1, tk_pad)

    lam_init = 0.8 - 0.6 * math.exp(-0.3 * l)
    if past:
        def with_past(cache, new):
            zeros = jnp.zeros((b, tk_pad - past - t, bw), F32)
            return jnp.concatenate([cache.reshape(b, past, bw), new, zeros], axis=1)
        ka_all, va_all = with_past(a_k_past, ka), with_past(a_v_past, va)
        kd_all, vd_all = with_past(d_k_past, kd), with_past(d_v_past, vd)
        o_a = _attn_a(z_qkv, 0, ka_all, 0, va_all, 0, bias, p['lambda_qk'][l], p['diff_norm_g'][l],
                      t=t, past=past, tq=tq, tk=tk, kv_len=past + t, lam_init=lam_init)
        o_d = _attn_d(z_qkv, 3 * nblk, kd_all, 0, vd_all, 0, cq, ck, t=t, past=past, tq=tq, tk=tk)
    else:
        o_a = _attn_a(z_qkv, 0, z_qkv, nblk, z_qkv, 2 * nblk, bias, p['lambda_qk'][l], p['diff_norm_g'][l],
                      t=t, past=0, tq=tq, tk=tk, kv_len=t, lam_init=lam_init)
        o_d = _attn_d(z_qkv, 3 * nblk, z_qkv, 4 * nblk, z_qkv, 5 * nblk, cq, ck, t=t, past=0, tq=tq, tk=tk)

    o_c, new_buf, h_last = _lru(zcb, conv_buf, lru_h0, p['conv_w'][l], p['conv_b'][l], p['w_rgate'][l],
                                p['b_rgate'][l], p['w_igate'][l], p['b_igate'][l], p['lru_lambda'][l])
    o_b, v_rows = _gmlp(zcb, p['gmlp_ln_g'][l], p['gmlp_ln_b'][l], p['gmlp_ws'][l], p['gmlp_bs'][l],
                        want_rows=past > 0)

    y = _merge(hf, [o.reshape(m, bw) for o in (o_a, o_b, o_c, o_d)], w_rest, p['w_out'], l)
    states = (ka.reshape(b, t, N_HEADS, 2, HEAD_DIM_A), va.reshape(b, t, N_HEADS, HEAD_W),
              kd.reshape(b, t, N_HEADS, HEAD_W), vd.reshape(b, t, N_HEADS, HEAD_W),
              logf, new_buf, h_last.reshape(b, bw), v_rows)
    return y, states


def _ffn(h, l, j, p):
    b, t, d = h.shape
    idx = l * 2 + j
    mid = _ffn_up(h.reshape(b * t, d), p['w_ffn_gate'], p['w_ffn_up'], idx)
    return _ffn_down(mid, p['w_ffn_down'], idx).reshape(b, t, d)


def _run_group(x, mods, p, w_rests, caches_of, bias, attn_cfg):
    depth = len(mods)
    states = []
    _, h = _norm(x, pre=(mods[0], p['g_pre'][0], 0))
    for l in range(depth):
        mod, g_pre, g_post = mods[l], p['g_pre'][l], p['g_post'][l]
        x, h = _norm(x, _ffn(h, l, 0, p), post=(mod, g_post, 0, 0.5), pre=(mod, g_pre, 1))
        y, st = _mixers(h, l, p, w_rests[l], caches_of(l), bias, attn_cfg)
        states.append(st)
        x, h = _norm(x, y.reshape(x.shape), post=(mod, g_post, 1, 1.0), pre=(mod, g_pre, 2))
        nxt = (mods[l + 1], p['g_pre'][l + 1], 0) if l + 1 < depth else None
        x, h = _norm(x, _ffn(h, l, 1, p), post=(mod, g_post, 2, 0.5), pre=nxt)
    return x, states


def kernel(x_prompt, x_sample, c_prompt, c_sample, cache_a_k, cache_a_v, cache_d_k, cache_d_v, cache_d_logf, state_conv, state_lru, w_ada, b_ada, g_pre, g_post, w_ffn_gate, w_ffn_up, w_ffn_down, w_in, w_out, rel_bias, lambda_qk, diff_norm_g, gmlp_ln_g, gmlp_ln_b, gmlp_ws, gmlp_bs, conv_w, conv_b, w_rgate, b_rgate, w_igate, b_igate, lru_lambda, b_forget):
    depth, d_model, _ = w_in.shape
    bp, seq, _ = x_prompt.shape
    bs, dec_seq, _ = x_sample.shape
    past = cache_a_k.shape[2]
    d_ff = w_ffn_gate.shape[-1]
    p = {'g_pre': g_pre, 'g_post': g_post,
         'w_ffn_gate': w_ffn_gate.reshape(depth * 2, d_model, d_ff),
         'w_ffn_up': w_ffn_up.reshape(depth * 2, d_model, d_ff),
         'w_ffn_down': w_ffn_down.reshape(depth * 2, d_ff, d_model),
         'w_in': w_in, 'w_out': w_out.reshape(depth * N_BRANCH, BRANCH_WIDTH, d_model),
         'lambda_qk': lambda_qk, 'diff_norm_g': diff_norm_g, 'gmlp_ln_g': gmlp_ln_g,
         'gmlp_ln_b': gmlp_ln_b, 'gmlp_ws': gmlp_ws, 'gmlp_bs': gmlp_bs, 'conv_w': conv_w,
         'conv_b': conv_b, 'w_rgate': w_rgate, 'b_rgate': b_rgate, 'w_igate': w_igate,
         'b_igate': b_igate, 'lru_lambda': lru_lambda, 'b_forget': b_forget}

    mod = _ada(jnp.concatenate([c_prompt, c_sample], axis=0), w_ada, b_ada)
    mod = mod.reshape(depth, bp + bs, 3 * N_SUB, d_model)
    mods_p = [mod[l, :bp] for l in range(depth)]
    mods_s = [mod[l, bp:] for l in range(depth)]
    w_rests = [w_in[l, :, OFF_CX:].astype(BF16)[None] for l in range(depth)]

    tq_p = _tile(seq, 256, CHUNK)
    bias_p = _bias_tiles(rel_bias, tq_p, tq_p, [0, -tq_p, -2 * tq_p])
    assert 2 * tq_p - (tq_p - 1) >= T5_SATURATION
    empty = (None, None, None, None, None, jnp.zeros((bp, CONV_WIDTH - 1, BRANCH_WIDTH), F32),
             jnp.zeros((bp, BRANCH_WIDTH), F32))
    xp, st_p = _run_group(x_prompt, mods_p, p, w_rests, lambda l: empty, bias_p, (tq_p, tq_p, seq))

    tk_pad = -(-(past + dec_seq) // LANE) * LANE
    bias_s = _bias_tiles(rel_bias, dec_seq, tk_pad, [-past])
    xs, st_s = _run_group(
        x_sample, mods_s, p, w_rests,
        lambda l: (cache_a_k[l], cache_a_v[l], cache_d_k[l], cache_d_v[l], cache_d_logf[l],
                   state_conv[l], state_lru[l]),
        bias_s, (dec_seq, tk_pad, tk_pad))

    np_ = [jnp.stack([st_p[l][j] for l in range(depth)]) for j in range(7)]
    ns_ = [jnp.stack([st_s[l][j] for l in range(depth)]) for j in range(8)]
    return (xp, xs, *np_, *ns_)
```

```python
import functools
import math

import jax
import jax.numpy as jnp
from jax import lax
from jax.experimental import pallas as pl
from jax.experimental.pallas import tpu as pltpu

F32 = jnp.float32
BF16 = jnp.bfloat16

LANE = 128
SUBLANE = 8
VMEM_LIMIT_BYTES = 56 * 1024 * 1024

CHUNK = 64
CHUNK_SHIFT = 6
N_HEADS = 8
HEAD_W = 128
HEAD_DIM_A = 64
N_BRANCH = 4
BRANCH_WIDTH = 1024
LRU_BLOCKS = 8
LRU_C = 8.0
CONV_WIDTH = 4
GMLP_GROUPS = 8
GMLP_CHUNK = 128
NUM_BUCKETS = 32
N_SUB = 3
EPS = 1e-6
NEG_INF = -1e30
T5_LARGE_STARTS = (12, 16, 23, 32, 46, 64, 91)
T5_SATURATION = 91
T5_FAR_BUCKET = NUM_BUCKETS // 2 - 1

OFF_DF = 6 * BRANCH_WIDTH
OFF_CX = OFF_DF + N_HEADS
OFF_GATE = OFF_CX + 4 * BRANCH_WIDTH
REST_CX, REST_CG, REST_BU, REST_BV = 0, 1, 2, 3

NT = (((1,), (1,)), ((), ()))
TN = (((0,), (0,)), ((), ()))


def _params(*sem):
    return pltpu.CompilerParams(dimension_semantics=sem, vmem_limit_bytes=VMEM_LIMIT_BYTES)


def _tile(n, target, mult):
    best = None
    for t in range(mult, min(n, target) + 1, mult):
        if n % t == 0:
            best = t
    return n if best is None else best


def _resident(shape, index_map):
    return pl.BlockSpec(shape, index_map, pipeline_mode=pl.Buffered(1))


def _gelu(x):
    return x * (0.5 * (1.0 + jnp.tanh(math.sqrt(2.0 / math.pi) * (x + 0.044715 * (x * x * x)))))


def _rms(x, g):
    return x * lax.rsqrt(jnp.mean(x * x, axis=-1, keepdims=True) + EPS) * g


def _pad_rows(x, rows):
    if x.shape[0] == rows:
        return x
    return jnp.concatenate([x, jnp.zeros((rows - x.shape[0],) + x.shape[1:], x.dtype)], axis=0)


def _lane_cumsum(x):
    n = x.shape[-1]
    lane = lax.broadcasted_iota(jnp.int32, x.shape, x.ndim - 1)
    s = 1
    while s < n:
        x = x + jnp.where(lane >= s, pltpu.roll(x, s, x.ndim - 1), 0.0)
        s *= 2
    return x


def _ada_kernel(c_ref, w_ref, b_ref, o_ref):
    c = c_ref[...]
    a = (c * jax.nn.sigmoid(c)).astype(BF16)
    o_ref[0] = jnp.dot(a, w_ref[0].astype(BF16), preferred_element_type=F32) + b_ref[0]


def _ada(c_all, w_ada, b_ada):
    depth, d, n = w_ada.shape
    bt = c_all.shape[0]
    tn = _tile(n, 512, LANE)
    return pl.pallas_call(
        _ada_kernel,
        grid=(depth, n // tn),
        in_specs=[pl.BlockSpec((bt, d), lambda l, j: (0, 0)),
                  pl.BlockSpec((1, d, tn), lambda l, j: (l, 0, j)),
                  pl.BlockSpec((1, 1, tn), lambda l, j: (l, 0, j))],
        out_specs=pl.BlockSpec((1, bt, tn), lambda l, j: (l, 0, j)),
        out_shape=jax.ShapeDtypeStruct((depth, bt, n), F32),
        compiler_params=_params("parallel", "parallel"),
        name="ada",
    )(c_all, w_ada, b_ada.reshape(depth, 1, n))


def _norm_kernel(*refs, k_post, coef, k_pre):
    refs = list(refs)
    x_ref = refs.pop(0)
    x = x_ref[0]
    if k_post is not None:
        y_ref, mpost_ref, gpost_ref = refs.pop(0), refs.pop(0), refs.pop(0)
    if k_pre is not None:
        mpre_ref, gpre_ref = refs.pop(0), refs.pop(0)
    if k_post is not None:
        xo_ref = refs.pop(0)
        m = mpost_ref[0]
        gate = m[3 * k_post + 2:3 * k_post + 3]
        x = x + (coef * gate) * _rms(y_ref[0], gpost_ref[k_post:k_post + 1, :])
        xo_ref[0] = x
    if k_pre is not None:
        ho_ref = refs.pop(0)
        m = mpre_ref[0]
        shift = m[3 * k_pre:3 * k_pre + 1]
        scale = m[3 * k_pre + 1:3 * k_pre + 2]
        ho_ref[0] = (_rms(x, gpre_ref[k_pre:k_pre + 1, :]) * (1.0 + scale) + shift).astype(BF16)


def _norm(x, y=None, post=None, pre=None):
    b, t, d = x.shape
    tt = _tile(t, 256, SUBLANE)
    row = pl.BlockSpec((1, tt, d), lambda i, j: (i, j, 0))
    mod_spec = pl.BlockSpec((1, 3 * N_SUB, d), lambda i, j: (i, 0, 0))
    g_spec = pl.BlockSpec((N_SUB, d), lambda i, j: (0, 0))
    args, in_specs, out_shape, out_specs = [x], [row], [], []
    if post is not None:
        args += [y, post[0], post[1]]
        in_specs += [row, mod_spec, g_spec]
        out_shape.append(jax.ShapeDtypeStruct((b, t, d), F32))
        out_specs.append(row)
    if pre is not None:
        args += [pre[0], pre[1]]
        in_specs += [mod_spec, g_spec]
        out_shape.append(jax.ShapeDtypeStruct((b, t, d), BF16))
        out_specs.append(row)
    outs = pl.pallas_call(
        functools.partial(_norm_kernel, k_post=None if post is None else post[2],
                          coef=None if post is None else post[3],
                          k_pre=None if pre is None else pre[2]),
        grid=(b, t // tt), in_specs=in_specs, out_specs=out_specs, out_shape=out_shape,
        compiler_params=_params("parallel", "parallel"),
        name="norm",
    )(*args)
    outs = list(outs)
    x_new = outs.pop(0) if post is not None else None
    h = outs.pop(0) if pre is not None else None
    return x_new, h


def _mm_kernel(x_ref, w_ref, o_ref):
    o_ref[...] = jnp.dot(x_ref[...], w_ref[0].astype(BF16),
                         preferred_element_type=F32).astype(o_ref.dtype)


def _matmul(x, w, l, out_dtype, tm_target, tn_target):
    m, k = x.shape
    n = w.shape[-1]
    tm = _tile(m, tm_target, 16)
    tn = _tile(n, tn_target, LANE)
    return pl.pallas_call(
        _mm_kernel,
        grid=(m // tm, n // tn),
        in_specs=[_resident((tm, k), lambda i, j: (i, 0)),
                  pl.BlockSpec((1, k, tn), lambda i, j: (l, 0, j))],
        out_specs=pl.BlockSpec((tm, tn), lambda i, j: (i, j)),
        out_shape=jax.ShapeDtypeStruct((m, n), out_dtype),
        compiler_params=_params("parallel", "arbitrary"),
        name="matmul",
    )(x, w)


def _mm_nt_kernel(x_ref, w_ref, o_ref):
    o_ref[...] = lax.dot_general(x_ref[...], w_ref[...].astype(BF16), NT,
                                 preferred_element_type=F32).astype(o_ref.dtype)


def _matmul_nt(x, wt, l, row0, nrows, out_dtype, tm_target=2048, tn_target=256):
    m, k = x.shape
    tm = _tile(m, tm_target, 16)
    tn = _tile(nrows, tn_target, LANE)
    return pl.pallas_call(
        _mm_nt_kernel,
        grid=(m // tm, nrows // tn),
        in_specs=[_resident((tm, k), lambda i, j: (i, 0)),
                  pl.BlockSpec((None, pl.Element(tn), pl.Element(k)),
                               lambda i, j: (l, pl.multiple_of(row0 + j * tn, SUBLANE), 0))],
        out_specs=pl.BlockSpec((tm, tn), lambda i, j: (i, j)),
        out_shape=jax.ShapeDtypeStruct((m, nrows), out_dtype),
        compiler_params=_params("parallel", "arbitrary"),
        name="matmul_nt",
    )(x, wt)


def _ffn_up_kernel(x_ref, wg_ref, wu_ref, o_ref):
    x = x_ref[...]
    g = jnp.dot(x, wg_ref[0].astype(BF16), preferred_element_type=F32)
    u = jnp.dot(x, wu_ref[0].astype(BF16), preferred_element_type=F32)
    o_ref[...] = ((g * jax.nn.sigmoid(g)) * u).astype(o_ref.dtype)


def _ffn_up(x, wg, wu, l):
    m, k = x.shape
    n = wg.shape[-1]
    tm = _tile(m, 2048, 16)
    tn = _tile(n, 256, LANE)
    w_spec = pl.BlockSpec((1, k, tn), lambda i, j: (l, 0, j))
    return pl.pallas_call(
        _ffn_up_kernel,
        grid=(m // tm, n // tn),
        in_specs=[_resident((tm, k), lambda i, j: (i, 0)), w_spec, w_spec],
        out_specs=pl.BlockSpec((tm, tn), lambda i, j: (i, j)),
        out_shape=jax.ShapeDtypeStruct((m, n), BF16),
        compiler_params=_params("parallel", "arbitrary"),
        name="ffn_up",
    )(x, wg, wu)


def _merge_kernel(h_ref, oa_ref, ob_ref, oc_ref, od_ref, wg_ref, wo_ref, y_ref):
    i = pl.program_id(2)
    gate = jax.nn.sigmoid(lax.dot_general(h_ref[...], wg_ref[...].astype(BF16), NT,
                                          preferred_element_type=F32))
    wo = wo_ref[0].astype(BF16)
    for c, o_ref in enumerate((oa_ref, ob_ref, oc_ref, od_ref)):
        @pl.when(i == c)
        def _(c=c, o_ref=o_ref):
            part = gate * jnp.dot(o_ref[...], wo, preferred_element_type=F32)
            if c == 0:
                y_ref[...] = part
            else:
                y_ref[...] += part


def _merge(h, branches, w_in_t, w_out, l):
    m, d = h.shape
    n = w_out.shape[-1]
    bw = branches[0].shape[-1]
    tm = _tile(m, 1024, 16)
    tn = _tile(n, 256, LANE)
    o_spec = _resident((tm, bw), lambda a, j, i: (a, 0))
    return pl.pallas_call(
        _merge_kernel,
        grid=(m // tm, n // tn, N_BRANCH),
        in_specs=[_resident((tm, d), lambda a, j, i: (a, 0))] + [o_spec] * N_BRANCH
        + [pl.BlockSpec((None, pl.Element(tn), pl.Element(d)),
                        lambda a, j, i: (l, pl.multiple_of(OFF_GATE + i * n + j * tn, SUBLANE), 0)),
           pl.BlockSpec((1, bw, tn), lambda a, j, i: (l * N_BRANCH + i, 0, j))],
        out_specs=pl.BlockSpec((tm, tn), lambda a, j, i: (a, j)),
        out_shape=jax.ShapeDtypeStruct((m, n), F32),
        compiler_params=_params("parallel", "arbitrary", "arbitrary"),
        name="merge",
    )(h, *branches, w_in_t, w_out)


def _bias_kernel(tab_ref, o_ref, *, nq, nk, tiles, transposed, kv_end):
    h = pl.program_id(0)
    shape = (nk, nq) if transposed else (nq, nk)
    q_i = lax.broadcasted_iota(jnp.int32, shape, 1 if transposed else 0)
    k_i = lax.broadcasted_iota(jnp.int32, shape, 0 if transposed else 1)
    for c, tile in enumerate(tiles):
        for mp in range(2):
            if tile is None:
                val = jnp.full(shape, NEG_INF, F32)
            else:
                q0, k0 = tile
                q_pos, k_pos = q0 + q_i, k0 + k_i
                rel = k_pos - q_pos
                n = jnp.abs(rel)
                large = NUM_BUCKETS // 4
                for start in T5_LARGE_STARTS:
                    large = large + jnp.where(n >= start, 1, 0)
                bucket = jnp.where(rel > 0, NUM_BUCKETS // 2, 0) + jnp.where(n < NUM_BUCKETS // 4, n, large)
                val = jnp.zeros(shape, F32)
                for b in range(NUM_BUCKETS):
                    val = jnp.where(bucket == b, tab_ref[b * (2 * N_HEADS) + h * 2 + mp], val)
                val = val - tab_ref[T5_FAR_BUCKET * (2 * N_HEADS) + h * 2 + mp]
                visible = jnp.logical_and((k_pos >> CHUNK_SHIFT) <= (q_pos >> CHUNK_SHIFT), k_pos < kv_end)
                val = jnp.where(visible, val, NEG_INF)
            if transposed:
                o_ref[0, c, :, mp * nq:(mp + 1) * nq] = val
            else:
                o_ref[0, c, mp * nq:(mp + 1) * nq, :] = val


def _bias_tiles(rel_bias, nq, nk, tiles, transposed, kv_end):
    shape = (nk, 2 * nq) if transposed else (2 * nq, nk)
    return pl.pallas_call(
        functools.partial(_bias_kernel, nq=nq, nk=nk, tiles=tuple(tiles), transposed=transposed, kv_end=kv_end),
        grid=(N_HEADS,),
        in_specs=[pl.BlockSpec(memory_space=pltpu.SMEM)],
        out_specs=pl.BlockSpec((1, len(tiles)) + shape, lambda h: (h, 0, 0, 0)),
        out_shape=jax.ShapeDtypeStruct((N_HEADS, len(tiles)) + shape, F32),
        compiler_params=_params("parallel"),
        name="bias_tiles",
    )(rel_bias.reshape(-1))


def _lambda(lq_ref, lam_init):
    lq = lq_ref[...]
    return (jnp.exp(jnp.sum(lq[0:1] * lq[1:2], axis=-1, keepdims=True))
            - jnp.exp(jnp.sum(lq[2:3] * lq[3:4], axis=-1, keepdims=True)) + lam_init)


def _split_maps(q):
    lane = lax.broadcasted_iota(jnp.int32, q.shape, 1)
    q = q * (HEAD_DIM_A ** -0.5)
    return jnp.concatenate([jnp.where(lane < HEAD_DIM_A, q, 0.0),
                            jnp.where(lane >= HEAD_DIM_A, q, 0.0)], axis=0).astype(BF16)


def _flash_step_t(carry, acc_ref, s, v):
    m, l = carry
    m_new = jnp.maximum(m, jnp.max(s, axis=0, keepdims=True))
    p = jnp.exp(s - m_new)
    alpha = jnp.exp(m - m_new)
    l = alpha * l + jnp.sum(p, axis=0, keepdims=True)
    acc_ref[...] = alpha * acc_ref[...] + lax.dot_general(v, p.astype(BF16), TN, preferred_element_type=F32)
    return m_new, l


def _flash_init(acc_ref):
    acc_ref[...] = jnp.zeros(acc_ref.shape, F32)
    nq = acc_ref.shape[-1]
    return jnp.full((1, nq), NEG_INF, F32), jnp.zeros((1, nq), F32)


HEADS_PER_STEP = 4


def _head(x, g):
    return x[:, g * HEAD_W:(g + 1) * HEAD_W]


def _attn_a_prompt_kernel(lq_ref, q_ref, k_ref, v_ref, bias_ref, g_ref, o_ref, acc_sc, *, tq, lam_init):
    qi = pl.program_id(2)
    heads = range(HEADS_PER_STEP)
    q_all = q_ref[0]
    q2 = [_split_maps(_head(q_all, g)) for g in heads]

    def tile(kt, carries, bias_tile):
        k0 = pl.multiple_of(kt * tq, tq)
        k_all = k_ref[0, pl.ds(k0, tq), :]
        v_all = v_ref[0, pl.ds(k0, tq), :]
        out = []
        for g in heads:
            s = lax.dot_general(_head(k_all, g).astype(BF16), q2[g], NT,
                                preferred_element_type=F32)
            if bias_tile is not None:
                s = s + bias_ref[g, bias_tile]
            out.append(_flash_step_t(carries[g], acc_sc.at[g], s, _head(v_all, g).astype(BF16)))
        return tuple(out)

    carries = tile(qi, tuple(_flash_init(acc_sc.at[g]) for g in heads), 0)
    carries = tile(jnp.maximum(qi - 1, 0), carries, jnp.where(qi == 0, 2, 1))
    carries = lax.fori_loop(0, qi - 1, lambda kt, c: tile(kt, c, None), carries)
    lam = _lambda(lq_ref, lam_init)
    for g in heads:
        o = acc_sc[g] * (1.0 / carries[g][1])
        o = (o[:, :tq] - lam * o[:, tq:]).T
        o_ref[0, :, g * HEAD_W:(g + 1) * HEAD_W] = (_rms(o, g_ref[...]) * (1.0 - lam_init)).astype(o_ref.dtype)


def _attn_a_prompt(z_qkv, bias, lq, g_norm, *, tq, lam_init):
    b, t, _ = z_qkv.shape
    gw = HEADS_PER_STEP * HEAD_W
    nblk = BRANCH_WIDTH // gw
    return pl.pallas_call(
        functools.partial(_attn_a_prompt_kernel, tq=tq, lam_init=lam_init),
        grid=(b, nblk, t // tq),
        in_specs=[pl.BlockSpec((4, HEAD_DIM_A), lambda i, h, j: (0, 0)),
                  pl.BlockSpec((1, tq, gw), lambda i, h, j: (i, j, h)),
                  _resident((1, t, gw), lambda i, h, j: (i, 0, nblk + h)),
                  _resident((1, t, gw), lambda i, h, j: (i, 0, 2 * nblk + h)),
                  _resident((HEADS_PER_STEP, 3, tq, 2 * tq), lambda i, h, j: (h, 0, 0, 0)),
                  pl.BlockSpec((1, HEAD_W), lambda i, h, j: (0, 0))],
        out_specs=pl.BlockSpec((1, tq, gw), lambda i, h, j: (i, j, h)),
        out_shape=jax.ShapeDtypeStruct((b, t, N_HEADS * HEAD_W), BF16),
        scratch_shapes=[pltpu.VMEM((HEADS_PER_STEP, HEAD_W, 2 * tq), F32)],
        compiler_params=_params("parallel", "parallel", "arbitrary"),
        name="attn_a_prompt",
    )(lq, z_qkv, z_qkv, z_qkv, bias, g_norm.reshape(1, HEAD_W))


def _attn_d_prompt_kernel(q_ref, k_ref, v_ref, cq_ref, ck_ref, o_ref, ckb_sc, acc_sc, *, tq, t):
    h0 = pl.program_id(1) * HEADS_PER_STEP
    qi = pl.program_id(2)
    heads = range(HEADS_PER_STEP)

    @pl.when(qi == 0)
    def _():
        for g in heads:
            ck = ck_ref[0, pl.ds(h0 + g, 1), :]
            for c in range(t // LANE):
                ckb_sc[g, c * LANE:(c + 1) * LANE, :] = jnp.broadcast_to(
                    ck[:, c * LANE:(c + 1) * LANE], (LANE, LANE)).T

    q_all = q_ref[0]
    q = [_head(q_all, g).astype(BF16) for g in heads]
    cq = [cq_ref[0, pl.ds(h0 + g, 1), :] for g in heads]
    scale = HEAD_W ** -0.5

    def tile(kt, carries, diagonal):
        k0 = pl.multiple_of(kt * tq, tq)
        k_all = k_ref[0, pl.ds(k0, tq), :]
        v_all = v_ref[0, pl.ds(k0, tq), :]
        out = []
        for g in heads:
            s = lax.dot_general(_head(k_all, g).astype(BF16), q[g], NT,
                                preferred_element_type=F32) * scale
            ckb = ckb_sc[g, pl.ds(k0, tq), :]
            s = s + cq[g] - jnp.concatenate([ckb] * (tq // LANE), axis=1)
            if diagonal:
                key = lax.broadcasted_iota(jnp.int32, (tq, tq), 0)
                qry = lax.broadcasted_iota(jnp.int32, (tq, tq), 1)
                s = jnp.where(key <= qry, s, NEG_INF)
            out.append(_flash_step_t(carries[g], acc_sc.at[g], s, _head(v_all, g).astype(BF16)))
        return tuple(out)

    carries = tile(qi, tuple(_flash_init(acc_sc.at[g]) for g in heads), True)
    carries = lax.fori_loop(0, qi, lambda kt, c: tile(kt, c, False), carries)
    for g in heads:
        o = acc_sc[g] * (1.0 / carries[g][1])
        o_ref[0, :, g * HEAD_W:(g + 1) * HEAD_W] = o.T.astype(o_ref.dtype)


def _attn_d_prompt(z_qkv, cum, *, tq):
    b, t, _ = z_qkv.shape
    gw = HEADS_PER_STEP * HEAD_W
    nblk = BRANCH_WIDTH // gw
    return pl.pallas_call(
        functools.partial(_attn_d_prompt_kernel, tq=tq, t=t),
        grid=(b, nblk, t // tq),
        in_specs=[pl.BlockSpec((1, tq, gw), lambda i, h, j: (i, j, 3 * nblk + h)),
                  _resident((1, t, gw), lambda i, h, j: (i, 0, 4 * nblk + h)),
                  _resident((1, t, gw), lambda i, h, j: (i, 0, 5 * nblk + h)),
                  pl.BlockSpec((1, N_HEADS, tq), lambda i, h, j: (i, 0, j)),
                  pl.BlockSpec((1, N_HEADS, t), lambda i, h, j: (i, 0, 0))],
        out_specs=pl.BlockSpec((1, tq, gw), lambda i, h, j: (i, j, h)),
        out_shape=jax.ShapeDtypeStruct((b, t, N_HEADS * HEAD_W), BF16),
        scratch_shapes=[pltpu.VMEM((HEADS_PER_STEP, t, LANE), F32),
                        pltpu.VMEM((HEADS_PER_STEP, HEAD_W, tq), F32)],
        compiler_params=_params("parallel", "arbitrary", "arbitrary"),
        name="attn_d_prompt",
    )(z_qkv, z_qkv, z_qkv, cum, cum)


def _softmax_two(s_p, s_n, v_p, v_n):
    m = jnp.maximum(jnp.max(s_p, axis=-1, keepdims=True), jnp.max(s_n, axis=-1, keepdims=True))
    p_p = jnp.exp(s_p - m)
    p_n = jnp.exp(s_n - m)
    l = jnp.sum(p_p, axis=-1, keepdims=True) + jnp.sum(p_n, axis=-1, keepdims=True)
    acc = (jnp.dot(p_p.astype(BF16), v_p, preferred_element_type=F32)
           + jnp.dot(p_n.astype(BF16), v_n, preferred_element_type=F32))
    return acc / l


def _attn_a_sample_kernel(lq_ref, q_ref, kt_ref, vp_ref, kn_ref, vn_ref, bp_ref, bn_ref, g_ref, o_ref, *,
                          t, lam_init):
    q2 = _split_maps(q_ref[0])
    kn = _pad_rows(kn_ref[0], LANE).astype(BF16)
    vn = _pad_rows(vn_ref[0], LANE).astype(BF16)
    s_p = jnp.dot(q2, kt_ref[0, 0, 0].astype(BF16), preferred_element_type=F32) + bp_ref[0, 0]
    s_n = lax.dot_general(q2, kn, NT, preferred_element_type=F32) + bn_ref[0, 0]
    o = _softmax_two(s_p, s_n, vp_ref[0, 0].astype(BF16), vn)
    o = o[:t] - _lambda(lq_ref, lam_init) * o[t:]
    o_ref[0] = (_rms(o, g_ref[...]) * (1.0 - lam_init)).astype(o_ref.dtype)


def _attn_a_sample(z_qkv, k_cache_t, v_cache, l, bias_past, bias_new, lq, g_norm, *, lam_init):
    b, t, _ = z_qkv.shape
    past = v_cache.shape[2]
    nblk = BRANCH_WIDTH // HEAD_W
    return pl.pallas_call(
        functools.partial(_attn_a_sample_kernel, t=t, lam_init=lam_init),
        grid=(b, N_HEADS),
        in_specs=[pl.BlockSpec((4, HEAD_DIM_A), lambda i, h: (0, 0)),
                  pl.BlockSpec((1, t, HEAD_W), lambda i, h: (i, 0, h)),
                  pl.BlockSpec((1, 1, 1, HEAD_W, past), lambda i, h: (l, i, h, 0, 0)),
                  pl.BlockSpec((1, 1, past, HEAD_W), lambda i, h: (l, i, 0, h)),
                  pl.BlockSpec((1, t, HEAD_W), lambda i, h: (i, 0, nblk + h)),
                  pl.BlockSpec((1, t, HEAD_W), lambda i, h: (i, 0, 2 * nblk + h)),
                  pl.BlockSpec((1, 1, 2 * t, past), lambda i, h: (h, 0, 0, 0)),
                  pl.BlockSpec((1, 1, 2 * t, LANE), lambda i, h: (h, 0, 0, 0)),
                  pl.BlockSpec((1, HEAD_W), lambda i, h: (0, 0))],
        out_specs=pl.BlockSpec((1, t, HEAD_W), lambda i, h: (i, 0, h)),
        out_shape=jax.ShapeDtypeStruct((b, t, N_HEADS * HEAD_W), BF16),
        compiler_params=_params("parallel", "parallel"),
        name="attn_a_sample",
    )(lq, z_qkv, k_cache_t, v_cache, z_qkv, z_qkv, bias_past, bias_new, g_norm.reshape(1, HEAD_W))


def _attn_d_sample_kernel(q_ref, kp_ref, vp_ref, kn_ref, vn_ref, cq_ref, ckp_ref, ckn_ref, o_ref, *, t):
    h = pl.program_id(1)
    q = q_ref[0].astype(BF16)
    kn = _pad_rows(kn_ref[0], LANE).astype(BF16)
    vn = _pad_rows(vn_ref[0], LANE).astype(BF16)
    cq = cq_ref[0, 0]
    scale = HEAD_W ** -0.5
    s_p = lax.dot_general(q, kp_ref[0, 0].astype(BF16), NT, preferred_element_type=F32) * scale
    s_p = s_p + cq - ckp_ref[0, pl.ds(h, 1), :]
    s_n = lax.dot_general(q, kn, NT, preferred_element_type=F32) * scale
    s_n = s_n + cq - ckn_ref[0, pl.ds(h, 1), :]
    qry = lax.broadcasted_iota(jnp.int32, (t, LANE), 0)
    key = lax.broadcasted_iota(jnp.int32, (t, LANE), 1)
    s_n = jnp.where(key <= qry, s_n, NEG_INF)
    o_ref[0] = _softmax_two(s_p, s_n, vp_ref[0, 0].astype(BF16), vn).astype(o_ref.dtype)


def _attn_d_sample(z_qkv, k_cache, v_cache, l, cq, cum_past, cum_new):
    b, t, _ = z_qkv.shape
    past = k_cache.shape[2]
    nblk = BRANCH_WIDTH // HEAD_W
    cache_spec = pl.BlockSpec((1, 1, past, HEAD_W), lambda i, h: (l, i, 0, h))
    return pl.pallas_call(
        functools.partial(_attn_d_sample_kernel, t=t),
        grid=(b, N_HEADS),
        in_specs=[pl.BlockSpec((1, t, HEAD_W), lambda i, h: (i, 0, 3 * nblk + h)),
                  cache_spec, cache_spec,
                  pl.BlockSpec((1, t, HEAD_W), lambda i, h: (i, 0, 4 * nblk + h)),
                  pl.BlockSpec((1, t, HEAD_W), lambda i, h: (i, 0, 5 * nblk + h)),
                  pl.BlockSpec((1, 1, t, 1), lambda i, h: (i, h, 0, 0)),
                  pl.BlockSpec((1, N_HEADS, past), lambda i, h: (i, 0, 0)),
                  pl.BlockSpec((1, N_HEADS, LANE), lambda i, h: (i, 0, 0))],
        out_specs=pl.BlockSpec((1, t, HEAD_W), lambda i, h: (i, 0, h)),
        out_shape=jax.ShapeDtypeStruct((b, t, N_HEADS * HEAD_W), BF16),
        compiler_params=_params("parallel", "parallel"),
        name="attn_d_sample",
    )(z_qkv, k_cache, v_cache, z_qkv, z_qkv, cq, cum_past, cum_new)


def _logf_kernel(*refs, tt, has_past):
    refs = list(refs)
    h_ref, wf_ref, bf_ref = refs.pop(0), refs.pop(0), refs.pop(0)
    past_ref = refs.pop(0) if has_past else None
    logf_ref, cum_ref = refs.pop(0), refs.pop(0)
    cump_ref = refs.pop(0) if has_past else None
    carry_sc = refs.pop(0)

    @pl.when(pl.program_id(1) == 0)
    def _():
        if has_past:
            cump = _lane_cumsum(past_ref[0, 0])
            cump_ref[0] = cump
            carry_sc[...] = jnp.broadcast_to(cump[:, cump.shape[1] - 1:], carry_sc.shape)
        else:
            carry_sc[...] = jnp.zeros(carry_sc.shape, F32)

    ttp = logf_ref.shape[-1]
    hb = _pad_rows(h_ref[...], ttp)
    z = lax.dot_general(wf_ref[...].astype(BF16), hb, NT, preferred_element_type=F32) + bf_ref[...]
    logf = jnp.minimum(z, 0.0) - jnp.log1p(jnp.exp(-jnp.abs(z)))
    if tt < ttp:
        logf = jnp.where(lax.broadcasted_iota(jnp.int32, logf.shape, 1) < tt, logf, 0.0)
    logf_ref[0] = logf
    cum = _lane_cumsum(logf) + carry_sc[:, 0:1]
    cum_ref[0] = cum
    carry_sc[...] = jnp.broadcast_to(cum[:, ttp - 1:], carry_sc.shape)


def _logf(hf, b, w_in_t, l, b_forget, past_logf_t):
    m, d = hf.shape
    t = m // b
    tt = _tile(t, 1024, LANE) if t >= LANE else t
    ttp = max(tt, LANE)
    nt = t // tt
    has_past = past_logf_t is not None
    args = [hf, w_in_t, b_forget.reshape(N_HEADS, 1)]
    in_specs = [pl.BlockSpec((tt, d), lambda i, j: (i * nt + j, 0)),
                pl.BlockSpec((None, pl.Element(N_HEADS), pl.Element(d)), lambda i, j: (l, OFF_DF, 0)),
                pl.BlockSpec((N_HEADS, 1), lambda i, j: (0, 0))]
    out_shape = [jax.ShapeDtypeStruct((b, N_HEADS, nt * ttp), F32)] * 2
    out_specs = [pl.BlockSpec((1, N_HEADS, ttp), lambda i, j: (i, 0, j))] * 2
    if has_past:
        past = past_logf_t.shape[-1]
        args.append(past_logf_t)
        in_specs.append(pl.BlockSpec((1, 1, N_HEADS, past), lambda i, j: (l, i, 0, 0)))
        out_shape.append(jax.ShapeDtypeStruct((b, N_HEADS, past), F32))
        out_specs.append(pl.BlockSpec((1, N_HEADS, past), lambda i, j: (i, 0, 0)))
    outs = pl.pallas_call(
        functools.partial(_logf_kernel, tt=tt, has_past=has_past),
        grid=(b, nt), in_specs=in_specs, out_specs=out_specs, out_shape=out_shape,
        scratch_shapes=[pltpu.VMEM((N_HEADS, LANE), F32)],
        compiler_params=_params("parallel", "arbitrary"),
        name="logf_cumsum",
    )(*args)
    return outs[0], outs[1], (outs[2] if has_past else None)


def _lru_kernel(cx_ref, cg_ref, buf0_ref, h0_ref, cw_ref, cb_ref, wr_ref, br_ref, wi_ref, bi_ref,
                lam_ref, oc_ref, nbuf_ref, hl_ref, tail_sc, h_sc, *, tt):
    w = cx_ref.shape[-1]
    blk = w // LRU_BLOCKS
    pad = SUBLANE - (CONV_WIDTH - 1)

    @pl.when(pl.program_id(1) == 0)
    def _():
        tail_sc[0:pad, :] = jnp.zeros((pad, w), F32)
        tail_sc[pad:SUBLANE, :] = buf0_ref[0]
        h_sc[...] = h0_ref[0]

    full = jnp.concatenate([tail_sc[...], cx_ref[0]], axis=0)
    cw = cw_ref[...]
    xc = cb_ref[...]
    for j in range(CONV_WIDTH):
        xc = xc + full[pad + j:pad + j + tt] * cw[j:j + 1]
    tail_sc[...] = full[tt:tt + SUBLANE]
    nbuf_ref[0] = full[tt + pad:tt + SUBLANE]

    xb = xc.astype(BF16)

    def gate(w_ref, b_ref):
        parts = [jnp.dot(xb[:, g * blk:(g + 1) * blk], w_ref[g].astype(BF16), preferred_element_type=F32)
                 for g in range(LRU_BLOCKS)]
        return jax.nn.sigmoid(jnp.concatenate(parts, axis=1) + b_ref[...])

    r = gate(wr_ref, br_ref)
    i = gate(wi_ref, bi_ref)
    nl = -lam_ref[...]
    softplus = jnp.maximum(nl, 0.0) + jnp.log1p(jnp.exp(-jnp.abs(nl)))
    log_a = (-LRU_C * r) * softplus
    a = jnp.exp(log_a)
    u = jnp.sqrt(-jnp.tanh(log_a) * (a * a + 1.0)) * (i * xc)

    row = lax.broadcasted_iota(jnp.int32, (tt, w), 0)
    s = 1
    while s < tt:
        valid = row >= s
        u = jnp.where(valid, a * pltpu.roll(u, s, 0) + u, u)
        a = jnp.where(valid, a * pltpu.roll(a, s, 0), a)
        s *= 2
    h = u + a * h_sc[...]
    h_sc[...] = h[tt - 1:tt]
    hl_ref[0] = h[tt - 1:tt]
    oc_ref[0] = (h * _gelu(cg_ref[0])).astype(oc_ref.dtype)


def _lru(zcb, conv_buf, h0, conv_w, conv_b, w_r, b_r, w_i, b_i, lam):
    b, t, _ = zcb.shape
    w = BRANCH_WIDTH
    tt = _tile(t, 256, SUBLANE)
    vec = pl.BlockSpec((1, w), lambda i, j: (0, 0))
    mat = pl.BlockSpec((LRU_BLOCKS, w // LRU_BLOCKS, w // LRU_BLOCKS), lambda i, j: (0, 0, 0))
    return pl.pallas_call(
        functools.partial(_lru_kernel, tt=tt),
        grid=(b, t // tt),
        in_specs=[pl.BlockSpec((1, tt, w), lambda i, j: (i, j, REST_CX)),
                  pl.BlockSpec((1, tt, w), lambda i, j: (i, j, REST_CG)),
                  pl.BlockSpec((1, CONV_WIDTH - 1, w), lambda i, j: (i, 0, 0)),
                  pl.BlockSpec((1, 1, w), lambda i, j: (i, 0, 0)),
                  pl.BlockSpec((CONV_WIDTH, w), lambda i, j: (0, 0)),
                  vec, mat, vec, mat, vec, vec],
        out_specs=[pl.BlockSpec((1, tt, w), lambda i, j: (i, j, 0)),
                   pl.BlockSpec((1, CONV_WIDTH - 1, w), lambda i, j: (i, 0, 0)),
                   pl.BlockSpec((1, 1, w), lambda i, j: (i, 0, 0))],
        out_shape=[jax.ShapeDtypeStruct((b, t, w), BF16),
                   jax.ShapeDtypeStruct((b, CONV_WIDTH - 1, w), F32),
                   jax.ShapeDtypeStruct((b, 1, w), F32)],
        scratch_shapes=[pltpu.VMEM((SUBLANE, w), F32), pltpu.VMEM((1, w), F32)],
        compiler_params=_params("parallel", "arbitrary"),
        name="conv_lru",
    )(zcb, zcb, conv_buf, h0.reshape(b, 1, w), conv_w, conv_b.reshape(1, w), w_r, b_r.reshape(1, w),
      w_i, b_i.reshape(1, w), lam.reshape(1, w))


def _gmlp_kernel(zu_ref, zv_ref, lng_ref, lnb_ref, ws_ref, bst_ref, ob_ref, *vn_refs, ct):
    u = _gelu(zu_ref[0])
    v = _gelu(zv_ref[0])
    w = v.shape[-1]
    gw = w // GMLP_GROUPS
    d = v - jnp.mean(v, axis=-1, keepdims=True)
    vn = d * lax.rsqrt(jnp.mean(d * d, axis=-1, keepdims=True) + EPS) * lng_ref[...] + lnb_ref[...]
    if vn_refs:
        vn_refs[0][0] = vn
    vb = _pad_rows(vn, GMLP_CHUNK).astype(BF16)
    tril = (lax.broadcasted_iota(jnp.int32, (GMLP_CHUNK, GMLP_CHUNK), 0)
            >= lax.broadcasted_iota(jnp.int32, (GMLP_CHUNK, GMLP_CHUNK), 1))
    parts = []
    for g in range(GMLP_GROUPS):
        wg = jnp.where(tril, ws_ref[g], 0.0).astype(BF16)
        s = jnp.dot(wg, vb[:, g * gw:(g + 1) * gw], preferred_element_type=F32) + bst_ref[:, g:g + 1]
        parts.append(s[:ct])
    ob_ref[0] = (u * jnp.concatenate(parts, axis=1)).astype(ob_ref.dtype)


def _gmlp(zcb, ln_g, ln_b, ws, bs, want_rows):
    b, t, _ = zcb.shape
    w = BRANCH_WIDTH
    ct = min(t, GMLP_CHUNK)
    vec = pl.BlockSpec((1, w), lambda i, j: (0, 0))
    row_out = pl.BlockSpec((1, ct, w), lambda i, j: (i, j, 0))
    out_shape = [jax.ShapeDtypeStruct((b, t, w), BF16)]
    out_specs = [row_out]
    if want_rows:
        out_shape.append(jax.ShapeDtypeStruct((b, t, w), F32))
        out_specs.append(row_out)
    outs = pl.pallas_call(
        functools.partial(_gmlp_kernel, ct=ct),
        grid=(b, t // ct),
        in_specs=[pl.BlockSpec((1, ct, w), lambda i, j: (i, j, REST_BU)),
                  pl.BlockSpec((1, ct, w), lambda i, j: (i, j, REST_BV)),
                  vec, vec,
                  pl.BlockSpec((GMLP_GROUPS, GMLP_CHUNK, GMLP_CHUNK), lambda i, j: (0, 0, 0)),
                  pl.BlockSpec((GMLP_CHUNK, GMLP_GROUPS), lambda i, j: (0, 0))],
        out_specs=out_specs, out_shape=out_shape,
        compiler_params=_params("parallel", "parallel"),
        name="gmlp",
    )(zcb, zcb, ln_g.reshape(1, w), ln_b.reshape(1, w), ws, bs.T)
    return (outs[0], outs[1]) if want_rows else (outs[0], None)


def _mixers(h, l, p, caches, bias):
    b, t, d = h.shape
    m = b * t
    hf = h.reshape(m, d)
    bw = BRANCH_WIDTH
    w_in_t = p['w_in_t']
    lam_init = 0.8 - 0.6 * math.exp(-0.3 * l)

    z_qkv = _matmul_nt(hf, w_in_t, l, 0, OFF_DF, F32).reshape(b, t, OFF_DF)
    zcb = _matmul_nt(hf, w_in_t, l, OFF_CX, 4 * bw, F32).reshape(b, t, 4 * bw)
    ka = z_qkv[:, :, 1 * bw:2 * bw]
    va = z_qkv[:, :, 2 * bw:3 * bw]
    kd = z_qkv[:, :, 4 * bw:5 * bw]
    vd = z_qkv[:, :, 5 * bw:6 * bw]

    if caches is None:
        logf_t, cum, _ = _logf(hf, b, w_in_t, l, p['b_forget'][l], None)
        tq = bias.shape[2]
        o_a = _attn_a_prompt(z_qkv, bias, p['lambda_qk'][l], p['diff_norm_g'][l], tq=tq, lam_init=lam_init)
        o_d = _attn_d_prompt(z_qkv, cum, tq=tq)
        conv_buf = jnp.zeros((b, CONV_WIDTH - 1, bw), F32)
        lru_h0 = jnp.zeros((b, bw), F32)
    else:
        a_k_t, a_v, d_k, d_v, d_logf_t, state_conv, state_lru = caches
        logf_t, cum_new, cum_past = _logf(hf, b, w_in_t, l, p['b_forget'][l], d_logf_t)
        logf_t = logf_t[:, :, :t]
        cq = cum_new[:, :, :t].reshape(b, N_HEADS, t, 1)
        o_a = _attn_a_sample(z_qkv, a_k_t, a_v, l, bias[0], bias[1], p['lambda_qk'][l], p['diff_norm_g'][l],
                             lam_init=lam_init)
        o_d = _attn_d_sample(z_qkv, d_k, d_v, l, cq, cum_past, cum_new)
        conv_buf, lru_h0 = state_conv[l], state_lru[l]

    o_c, new_buf, h_last = _lru(zcb, conv_buf, lru_h0, p['conv_w'][l], p['conv_b'][l], p['w_rgate'][l],
                                p['b_rgate'][l], p['w_igate'][l], p['b_igate'][l], p['lru_lambda'][l])
    o_b, v_rows = _gmlp(zcb, p['gmlp_ln_g'][l], p['gmlp_ln_b'][l], p['gmlp_ws'][l], p['gmlp_bs'][l],
                        want_rows=caches is not None)

    y = _merge(hf, [o.reshape(m, bw) for o in (o_a, o_b, o_c, o_d)], w_in_t, p['w_out'], l)
    states = (ka.reshape(b, t, N_HEADS, 2, HEAD_DIM_A), va.reshape(b, t, N_HEADS, HEAD_W),
              kd.reshape(b, t, N_HEADS, HEAD_W), vd.reshape(b, t, N_HEADS, HEAD_W),
              logf_t.transpose(0, 2, 1), new_buf, h_last.reshape(b, bw), v_rows)
    return y, states


def _ffn(h, l, j, p):
    b, t, d = h.shape
    idx = l * 2 + j
    mid = _ffn_up(h.reshape(b * t, d), p['w_ffn_gate'], p['w_ffn_up'], idx)
    return _matmul(mid, p['w_ffn_down'], idx, F32, tm_target=1024, tn_target=256).reshape(b, t, d)


def _run_group(x, mods, p, caches, bias):
    depth = len(mods)
    states = []
    _, h = _norm(x, pre=(mods[0], p['g_pre'][0], 0))
    for l in range(depth):
        mod, g_pre, g_post = mods[l], p['g_pre'][l], p['g_post'][l]
        x, h = _norm(x, _ffn(h, l, 0, p), post=(mod, g_post, 0, 0.5), pre=(mod, g_pre, 1))
        y, st = _mixers(h, l, p, caches, bias)
        states.append(st)
        x, h = _norm(x, y.reshape(x.shape), post=(mod, g_post, 1, 1.0), pre=(mod, g_pre, 2))
        nxt = (mods[l + 1], p['g_pre'][l + 1], 0) if l + 1 < depth else None
        x, h = _norm(x, _ffn(h, l, 1, p), post=(mod, g_post, 2, 0.5), pre=nxt)
    return x, states


def kernel(x_prompt, x_sample, c_prompt, c_sample, cache_a_k, cache_a_v, cache_d_k, cache_d_v, cache_d_logf, state_conv, state_lru, w_ada, b_ada, g_pre, g_post, w_ffn_gate, w_ffn_up, w_ffn_down, w_in, w_out, rel_bias, lambda_qk, diff_norm_g, gmlp_ln_g, gmlp_ln_b, gmlp_ws, gmlp_bs, conv_w, conv_b, w_rgate, b_rgate, w_igate, b_igate, lru_lambda, b_forget):
    depth, d_model, _ = w_in.shape
    bp, seq, _ = x_prompt.shape
    bs, dec_seq, _ = x_sample.shape
    past = cache_a_k.shape[2]
    d_ff = w_ffn_gate.shape[-1]
    p = {'g_pre': g_pre, 'g_post': g_post,
         'w_ffn_gate': w_ffn_gate.reshape(depth * 2, d_model, d_ff),
         'w_ffn_up': w_ffn_up.reshape(depth * 2, d_model, d_ff),
         'w_ffn_down': w_ffn_down.reshape(depth * 2, d_ff, d_model).astype(BF16),
         'w_in_t': jnp.swapaxes(w_in, 1, 2),
         'w_out': w_out.reshape(depth * N_BRANCH, BRANCH_WIDTH, d_model),
         'lambda_qk': lambda_qk, 'diff_norm_g': diff_norm_g, 'gmlp_ln_g': gmlp_ln_g,
         'gmlp_ln_b': gmlp_ln_b, 'gmlp_ws': gmlp_ws, 'gmlp_bs': gmlp_bs, 'conv_w': conv_w,
         'conv_b': conv_b, 'w_rgate': w_rgate, 'b_rgate': b_rgate, 'w_igate': w_igate,
         'b_igate': b_igate, 'lru_lambda': lru_lambda, 'b_forget': b_forget}

    mod = _ada(jnp.concatenate([c_prompt, c_sample], axis=0), w_ada, b_ada)
    mod = mod.reshape(depth, bp + bs, 3 * N_SUB, d_model)
    mods_p = [mod[l, :bp] for l in range(depth)]
    mods_s = [mod[l, bp:] for l in range(depth)]

    tq = _tile(seq, 256, LANE)
    assert tq + 1 >= T5_SATURATION and tq % CHUNK == 0
    bias_p = _bias_tiles(rel_bias, tq, tq, [(0, 0), (tq, 0), None], True, seq)
    xp, st_p = _run_group(x_prompt, mods_p, p, None, bias_p)

    bias_s = (_bias_tiles(rel_bias, dec_seq, past, [(past, 0)], False, past + dec_seq),
              _bias_tiles(rel_bias, dec_seq, LANE, [(past, past)], False, past + dec_seq))
    caches = (cache_a_k.transpose(0, 1, 3, 4, 5, 2).reshape(depth, bs, N_HEADS, HEAD_W, past),
              cache_a_v.reshape(depth, bs, past, BRANCH_WIDTH),
              cache_d_k.reshape(depth, bs, past, BRANCH_WIDTH),
              cache_d_v.reshape(depth, bs, past, BRANCH_WIDTH),
              cache_d_logf.transpose(0, 1, 3, 2), state_conv, state_lru)
    xs, st_s = _run_group(x_sample, mods_s, p, caches, bias_s)

    np_ = [jnp.stack([st_p[l][j] for l in range(depth)]) for j in range(7)]
    ns_ = [jnp.stack([st_s[l][j] for l in range(depth)]) for j in range(8)]
    return (xp, xs, *np_, *ns_)
```

```python
import functools
import math

import jax
import jax.numpy as jnp
from jax import lax
from jax.experimental import pallas as pl
from jax.experimental.pallas import tpu as pltpu

F32 = jnp.float32
BF16 = jnp.bfloat16

LANE = 128
SUBLANE = 8
VMEM_LIMIT_BYTES = 56 * 1024 * 1024

CHUNK = 64
CHUNK_SHIFT = 6
N_HEADS = 8
HEAD_W = 128
HEAD_DIM_A = 64
N_BRANCH = 4
BRANCH_WIDTH = 1024
LRU_BLOCKS = 8
LRU_C = 8.0
CONV_WIDTH = 4
GMLP_GROUPS = 8
GMLP_CHUNK = 128
NUM_BUCKETS = 32
N_SUB = 3
EPS = 1e-6
NEG_INF = -1e30
T5_LARGE_STARTS = (12, 16, 23, 32, 46, 64, 91)
T5_SATURATION = 91
T5_FAR_BUCKET = NUM_BUCKETS // 2 - 1

OFF_DF = 6 * BRANCH_WIDTH
OFF_CX = OFF_DF + N_HEADS
OFF_GATE = OFF_CX + 4 * BRANCH_WIDTH
REST_CX, REST_CG, REST_BU, REST_BV = 0, 1, 2, 3

NT = (((1,), (1,)), ((), ()))
TN = (((0,), (0,)), ((), ()))


def _params(*sem):
    return pltpu.CompilerParams(dimension_semantics=sem, vmem_limit_bytes=VMEM_LIMIT_BYTES)


def _tile(n, target, mult):
    best = None
    for t in range(mult, min(n, target) + 1, mult):
        if n % t == 0:
            best = t
    return n if best is None else best


def _resident(shape, index_map):
    return pl.BlockSpec(shape, index_map, pipeline_mode=pl.Buffered(1))


def _gelu(x):
    return x * (0.5 * (1.0 + jnp.tanh(math.sqrt(2.0 / math.pi) * (x + 0.044715 * (x * x * x)))))


def _rms(x, g):
    return x * lax.rsqrt(jnp.mean(x * x, axis=-1, keepdims=True) + EPS) * g


def _pad_rows(x, rows):
    if x.shape[0] == rows:
        return x
    return jnp.concatenate([x, jnp.zeros((rows - x.shape[0],) + x.shape[1:], x.dtype)], axis=0)


def _lane_cumsum(x):
    n = x.shape[-1]
    lane = lax.broadcasted_iota(jnp.int32, x.shape, x.ndim - 1)
    s = 1
    while s < n:
        x = x + jnp.where(lane >= s, pltpu.roll(x, s, x.ndim - 1), 0.0)
        s *= 2
    return x


def _ada_kernel(c_ref, w_ref, b_ref, o_ref):
    c = c_ref[...]
    a = (c * jax.nn.sigmoid(c)).astype(BF16)
    o_ref[0] = jnp.dot(a, w_ref[0].astype(BF16), preferred_element_type=F32) + b_ref[0]


def _ada(c_all, w_ada, b_ada):
    depth, d, n = w_ada.shape
    bt = c_all.shape[0]
    tn = _tile(n, 512, LANE)
    return pl.pallas_call(
        _ada_kernel,
        grid=(depth, n // tn),
        in_specs=[pl.BlockSpec((bt, d), lambda l, j: (0, 0)),
                  pl.BlockSpec((1, d, tn), lambda l, j: (l, 0, j)),
                  pl.BlockSpec((1, 1, tn), lambda l, j: (l, 0, j))],
        out_specs=pl.BlockSpec((1, bt, tn), lambda l, j: (l, 0, j)),
        out_shape=jax.ShapeDtypeStruct((depth, bt, n), F32),
        compiler_params=_params("parallel", "parallel"),
        name="ada",
    )(c_all, w_ada, b_ada.reshape(depth, 1, n))


def _norm_kernel(*refs, k_post, coef, k_pre):
    refs = list(refs)
    x_ref = refs.pop(0)
    x = x_ref[0]
    if k_post is not None:
        y_ref, mpost_ref, gpost_ref = refs.pop(0), refs.pop(0), refs.pop(0)
    if k_pre is not None:
        mpre_ref, gpre_ref = refs.pop(0), refs.pop(0)
    if k_post is not None:
        xo_ref = refs.pop(0)
        m = mpost_ref[0]
        gate = m[3 * k_post + 2:3 * k_post + 3]
        x = x + (coef * gate) * _rms(y_ref[0], gpost_ref[k_post:k_post + 1, :])
        xo_ref[0] = x
    if k_pre is not None:
        ho_ref = refs.pop(0)
        m = mpre_ref[0]
        shift = m[3 * k_pre:3 * k_pre + 1]
        scale = m[3 * k_pre + 1:3 * k_pre + 2]
        ho_ref[0] = (_rms(x, gpre_ref[k_pre:k_pre + 1, :]) * (1.0 + scale) + shift).astype(BF16)


def _norm(x, y=None, post=None, pre=None):
    b, t, d = x.shape
    tt = _tile(t, 256, SUBLANE)
    row = pl.BlockSpec((1, tt, d), lambda i, j: (i, j, 0))
    mod_spec = pl.BlockSpec((1, 3 * N_SUB, d), lambda i, j: (i, 0, 0))
    g_spec = pl.BlockSpec((N_SUB, d), lambda i, j: (0, 0))
    args, in_specs, out_shape, out_specs = [x], [row], [], []
    if post is not None:
        args += [y, post[0], post[1]]
        in_specs += [row, mod_spec, g_spec]
        out_shape.append(jax.ShapeDtypeStruct((b, t, d), F32))
        out_specs.append(row)
    if pre is not None:
        args += [pre[0], pre[1]]
        in_specs += [mod_spec, g_spec]
        out_shape.append(jax.ShapeDtypeStruct((b, t, d), BF16))
        out_specs.append(row)
    outs = pl.pallas_call(
        functools.partial(_norm_kernel, k_post=None if post is None else post[2],
                          coef=None if post is None else post[3],
                          k_pre=None if pre is None else pre[2]),
        grid=(b, t // tt), in_specs=in_specs, out_specs=out_specs, out_shape=out_shape,
        compiler_params=_params("parallel", "parallel"),
        name="norm",
    )(*args)
    outs = list(outs)
    x_new = outs.pop(0) if post is not None else None
    h = outs.pop(0) if pre is not None else None
    return x_new, h


def _mm_kernel(x_ref, w_ref, o_ref):
    o_ref[...] = jnp.dot(x_ref[...], w_ref[0].astype(BF16),
                         preferred_element_type=F32).astype(o_ref.dtype)


def _matmul(x, w, l, out_dtype, tm_target, tn_target):
    m, k = x.shape
    n = w.shape[-1]
    tm = _tile(m, tm_target, 16)
    tn = _tile(n, tn_target, LANE)
    return pl.pallas_call(
        _mm_kernel,
        grid=(m // tm, n // tn),
        in_specs=[_resident((tm, k), lambda i, j: (i, 0)),
                  pl.BlockSpec((1, k, tn), lambda i, j: (l, 0, j))],
        out_specs=pl.BlockSpec((tm, tn), lambda i, j: (i, j)),
        out_shape=jax.ShapeDtypeStruct((m, n), out_dtype),
        compiler_params=_params("parallel", "arbitrary"),
        name="matmul",
    )(x, w)


def _mm_nt_kernel(x_ref, w_ref, o_ref):
    o_ref[...] = lax.dot_general(x_ref[...], w_ref[...].astype(BF16), NT,
                                 preferred_element_type=F32).astype(o_ref.dtype)


def _matmul_nt(x, wt, l, row0, nrows, out_dtype, tm_target=2048, tn_target=256):
    m, k = x.shape
    tm = _tile(m, tm_target, 16)
    tn = _tile(nrows, tn_target, LANE)
    return pl.pallas_call(
        _mm_nt_kernel,
        grid=(m // tm, nrows // tn),
        in_specs=[_resident((tm, k), lambda i, j: (i, 0)),
                  pl.BlockSpec((None, pl.Element(tn), pl.Element(k)),
                               lambda i, j: (l, pl.multiple_of(row0 + j * tn, SUBLANE), 0))],
        out_specs=pl.BlockSpec((tm, tn), lambda i, j: (i, j)),
        out_shape=jax.ShapeDtypeStruct((m, nrows), out_dtype),
        compiler_params=_params("parallel", "arbitrary"),
        name="matmul_nt",
    )(x, wt)


def _ffn_up_kernel(x_ref, wg_ref, wu_ref, o_ref):
    x = x_ref[...]
    g = jnp.dot(x, wg_ref[0].astype(BF16), preferred_element_type=F32)
    u = jnp.dot(x, wu_ref[0].astype(BF16), preferred_element_type=F32)
    o_ref[...] = ((g * jax.nn.sigmoid(g)) * u).astype(o_ref.dtype)


def _ffn_up(x, wg, wu, l):
    m, k = x.shape
    n = wg.shape[-1]
    tm = _tile(m, 2048, 16)
    tn = _tile(n, 256, LANE)
    w_spec = pl.BlockSpec((1, k, tn), lambda i, j: (l, 0, j))
    return pl.pallas_call(
        _ffn_up_kernel,
        grid=(m // tm, n // tn),
        in_specs=[_resident((tm, k), lambda i, j: (i, 0)), w_spec, w_spec],
        out_specs=pl.BlockSpec((tm, tn), lambda i, j: (i, j)),
        out_shape=jax.ShapeDtypeStruct((m, n), BF16),
        compiler_params=_params("parallel", "arbitrary"),
        name="ffn_up",
    )(x, wg, wu)


def _merge_kernel(h_ref, oa_ref, ob_ref, oc_ref, od_ref, wg0, wg1, wg2, wg3, wo0, wo1, wo2, wo3, y_ref):
    h = h_ref[...]
    y = None
    for o_ref, wg, wo in ((oa_ref, wg0, wo0), (ob_ref, wg1, wo1), (oc_ref, wg2, wo2), (od_ref, wg3, wo3)):
        gate = jax.nn.sigmoid(lax.dot_general(h, wg[0], NT, preferred_element_type=F32))
        proj = jnp.dot(o_ref[...], wo[0].astype(BF16), preferred_element_type=F32)
        y = gate * proj if y is None else y + gate * proj
    y_ref[...] = y


def _merge(h, branches, w_gate, w_out, l):
    m, d = h.shape
    n = w_out.shape[-1]
    bw = branches[0].shape[-1]
    tm = _tile(m, 1024, 16)
    tn = _tile(n, 256, LANE)
    nj = n // tn

    def gate_spec(i):
        return pl.BlockSpec((1, tn, d), lambda a, j: (l, i * nj + j, 0))

    def out_spec(i):
        return pl.BlockSpec((1, bw, tn), lambda a, j: (l * N_BRANCH + i, 0, j))

    return pl.pallas_call(
        _merge_kernel,
        grid=(m // tm, nj),
        in_specs=[_resident((tm, d), lambda a, j: (a, 0))] + [_resident((tm, bw), lambda a, j: (a, 0))] * N_BRANCH
        + [gate_spec(i) for i in range(N_BRANCH)] + [out_spec(i) for i in range(N_BRANCH)],
        out_specs=pl.BlockSpec((tm, tn), lambda a, j: (a, j)),
        out_shape=jax.ShapeDtypeStruct((m, n), F32),
        compiler_params=_params("parallel", "arbitrary"),
        name="merge",
    )(h, *branches, w_gate, w_gate, w_gate, w_gate, w_out, w_out, w_out, w_out)


def _bias_kernel(tab_ref, o_ref, *, nq, nk, tiles, transposed, kv_end):
    h = pl.program_id(0)
    shape = (nk, nq) if transposed else (nq, nk)
    q_i = lax.broadcasted_iota(jnp.int32, shape, 1 if transposed else 0)
    k_i = lax.broadcasted_iota(jnp.int32, shape, 0 if transposed else 1)
    for c, tile in enumerate(tiles):
        for mp in range(2):
            if tile is None:
                val = jnp.full(shape, NEG_INF, F32)
            else:
                q0, k0 = tile
                q_pos, k_pos = q0 + q_i, k0 + k_i
                rel = k_pos - q_pos
                n = jnp.abs(rel)
                large = NUM_BUCKETS // 4
                for start in T5_LARGE_STARTS:
                    large = large + jnp.where(n >= start, 1, 0)
                bucket = jnp.where(rel > 0, NUM_BUCKETS // 2, 0) + jnp.where(n < NUM_BUCKETS // 4, n, large)
                val = jnp.zeros(shape, F32)
                for b in range(NUM_BUCKETS):
                    val = jnp.where(bucket == b, tab_ref[b * (2 * N_HEADS) + h * 2 + mp], val)
                val = val - tab_ref[T5_FAR_BUCKET * (2 * N_HEADS) + h * 2 + mp]
                visible = jnp.logical_and((k_pos >> CHUNK_SHIFT) <= (q_pos >> CHUNK_SHIFT), k_pos < kv_end)
                val = jnp.where(visible, val, NEG_INF)
            if transposed:
                o_ref[0, c, :, mp * nq:(mp + 1) * nq] = val
            else:
                o_ref[0, c, mp * nq:(mp + 1) * nq, :] = val


def _bias_tiles(rel_bias, nq, nk, tiles, transposed, kv_end):
    shape = (nk, 2 * nq) if transposed else (2 * nq, nk)
    return pl.pallas_call(
        functools.partial(_bias_kernel, nq=nq, nk=nk, tiles=tuple(tiles), transposed=transposed, kv_end=kv_end),
        grid=(N_HEADS,),
        in_specs=[pl.BlockSpec(memory_space=pltpu.SMEM)],
        out_specs=pl.BlockSpec((1, len(tiles)) + shape, lambda h: (h, 0, 0, 0)),
        out_shape=jax.ShapeDtypeStruct((N_HEADS, len(tiles)) + shape, F32),
        compiler_params=_params("parallel"),
        name="bias_tiles",
    )(rel_bias.reshape(-1))


def _lambda(lq_ref, lam_init):
    lq = lq_ref[...]
    return (jnp.exp(jnp.sum(lq[0:1] * lq[1:2], axis=-1, keepdims=True))
            - jnp.exp(jnp.sum(lq[2:3] * lq[3:4], axis=-1, keepdims=True)) + lam_init)


def _split_maps(q):
    lane = lax.broadcasted_iota(jnp.int32, q.shape, 1)
    q = q * (HEAD_DIM_A ** -0.5)
    return jnp.concatenate([jnp.where(lane < HEAD_DIM_A, q, 0.0),
                            jnp.where(lane >= HEAD_DIM_A, q, 0.0)], axis=0).astype(BF16)


def _flash_tile_t(carries, acc_sc, scores, values):
    out, probs, alphas = [], [], []
    for (m, l), s in zip(carries, scores):
        m_new = jnp.maximum(m, jnp.max(s, axis=0, keepdims=True))
        p = jnp.exp(s - m_new)
        alpha = jnp.exp(m - m_new)
        out.append((m_new, alpha * l + jnp.sum(p, axis=0, keepdims=True)))
        probs.append(p.astype(BF16))
        alphas.append(alpha)
    for g, (v, p, alpha) in enumerate(zip(values, probs, alphas)):
        acc_sc[g] = alpha * acc_sc[g] + lax.dot_general(v, p, TN, preferred_element_type=F32)
    return tuple(out)


def _flash_init(acc_ref):
    acc_ref[...] = jnp.zeros(acc_ref.shape, F32)
    nq = acc_ref.shape[-1]
    return jnp.full((1, nq), NEG_INF, F32), jnp.zeros((1, nq), F32)


HEADS_PER_STEP = 4


def _head(x, g):
    return x[:, g * HEAD_W:(g + 1) * HEAD_W]


def _attn_a_prompt_kernel(lq_ref, q_ref, k_ref, v_ref, bias_ref, g_ref, o_ref, acc_sc, *, tq, lam_init):
    qi = pl.program_id(2)
    heads = range(HEADS_PER_STEP)
    q_all = q_ref[0]
    q2 = [_split_maps(_head(q_all, g)) for g in heads]

    def tile(kt, carries, bias_tile):
        k0 = pl.multiple_of(kt * tq, tq)
        k_all = k_ref[0, pl.ds(k0, tq), :]
        v_all = v_ref[0, pl.ds(k0, tq), :]
        scores = [lax.dot_general(_head(k_all, g).astype(BF16), q2[g], NT, preferred_element_type=F32)
                  for g in heads]
        if bias_tile is not None:
            scores = [s + bias_ref[g, bias_tile] for g, s in enumerate(scores)]
        return _flash_tile_t(carries, acc_sc, scores, [_head(v_all, g).astype(BF16) for g in heads])

    carries = tile(qi, tuple(_flash_init(acc_sc.at[g]) for g in heads), 0)
    carries = tile(jnp.maximum(qi - 1, 0), carries, jnp.where(qi == 0, 2, 1))
    carries = lax.fori_loop(0, qi - 1, lambda kt, c: tile(kt, c, None), carries)
    lam = _lambda(lq_ref, lam_init)
    for g in heads:
        o = acc_sc[g] * (1.0 / carries[g][1])
        o = (o[:, :tq] - lam * o[:, tq:]).T
        o_ref[0, :, g * HEAD_W:(g + 1) * HEAD_W] = (_rms(o, g_ref[...]) * (1.0 - lam_init)).astype(o_ref.dtype)


def _attn_a_prompt(z_qkv, bias, lq, g_norm, *, tq, lam_init):
    b, t, _ = z_qkv.shape
    gw = HEADS_PER_STEP * HEAD_W
    nblk = BRANCH_WIDTH // gw
    return pl.pallas_call(
        functools.partial(_attn_a_prompt_kernel, tq=tq, lam_init=lam_init),
        grid=(b, nblk, t // tq),
        in_specs=[pl.BlockSpec((4, HEAD_DIM_A), lambda i, h, j: (0, 0)),
                  pl.BlockSpec((1, tq, gw), lambda i, h, j: (i, j, h)),
                  _resident((1, t, gw), lambda i, h, j: (i, 0, nblk + h)),
                  _resident((1, t, gw), lambda i, h, j: (i, 0, 2 * nblk + h)),
                  _resident((HEADS_PER_STEP, 3, tq, 2 * tq), lambda i, h, j: (h, 0, 0, 0)),
                  pl.BlockSpec((1, HEAD_W), lambda i, h, j: (0, 0))],
        out_specs=pl.BlockSpec((1, tq, gw), lambda i, h, j: (i, j, h)),
        out_shape=jax.ShapeDtypeStruct((b, t, N_HEADS * HEAD_W), BF16),
        scratch_shapes=[pltpu.VMEM((HEADS_PER_STEP, HEAD_W, 2 * tq), F32)],
        compiler_params=_params("parallel", "parallel", "arbitrary"),
        name="attn_a_prompt",
    )(lq, z_qkv, z_qkv, z_qkv, bias, g_norm.reshape(1, HEAD_W))


def _attn_d_prompt_kernel(q_ref, k_ref, v_ref, cq_ref, ck_ref, o_ref, ckb_sc, acc_sc, *, tq, t):
    h0 = pl.program_id(1) * HEADS_PER_STEP
    qi = pl.program_id(2)
    heads = range(HEADS_PER_STEP)

    @pl.when(qi == 0)
    def _():
        for g in heads:
            ck = ck_ref[0, pl.ds(h0 + g, 1), :]
            for c in range(t // LANE):
                ckb_sc[g, c * LANE:(c + 1) * LANE, :] = jnp.broadcast_to(
                    ck[:, c * LANE:(c + 1) * LANE], (LANE, LANE)).T

    q_all = q_ref[0]
    q = [_head(q_all, g).astype(BF16) for g in heads]
    cq = [cq_ref[0, pl.ds(h0 + g, 1), :] for g in heads]
    scale = HEAD_W ** -0.5

    def tile(kt, carries, diagonal):
        k0 = pl.multiple_of(kt * tq, tq)
        k_all = k_ref[0, pl.ds(k0, tq), :]
        v_all = v_ref[0, pl.ds(k0, tq), :]
        scores = [lax.dot_general(_head(k_all, g).astype(BF16), q[g], NT, preferred_element_type=F32)
                  for g in heads]
        for g in heads:
            ckb = ckb_sc[g, pl.ds(k0, tq), :]
            s = scores[g] * scale + cq[g] - jnp.concatenate([ckb] * (tq // LANE), axis=1)
            if diagonal:
                key = lax.broadcasted_iota(jnp.int32, (tq, tq), 0)
                qry = lax.broadcasted_iota(jnp.int32, (tq, tq), 1)
                s = jnp.where(key <= qry, s, NEG_INF)
            scores[g] = s
        return _flash_tile_t(carries, acc_sc, scores, [_head(v_all, g).astype(BF16) for g in heads])

    carries = tile(qi, tuple(_flash_init(acc_sc.at[g]) for g in heads), True)
    carries = lax.fori_loop(0, qi, lambda kt, c: tile(kt, c, False), carries)
    for g in heads:
        o = acc_sc[g] * (1.0 / carries[g][1])
        o_ref[0, :, g * HEAD_W:(g + 1) * HEAD_W] = o.T.astype(o_ref.dtype)


def _attn_d_prompt(z_qkv, cum, *, tq):
    b, t, _ = z_qkv.shape
    gw = HEADS_PER_STEP * HEAD_W
    nblk = BRANCH_WIDTH // gw
    return pl.pallas_call(
        functools.partial(_attn_d_prompt_kernel, tq=tq, t=t),
        grid=(b, nblk, t // tq),
        in_specs=[pl.BlockSpec((1, tq, gw), lambda i, h, j: (i, j, 3 * nblk + h)),
                  _resident((1, t, gw), lambda i, h, j: (i, 0, 4 * nblk + h)),
                  _resident((1, t, gw), lambda i, h, j: (i, 0, 5 * nblk + h)),
                  pl.BlockSpec((1, N_HEADS, tq), lambda i, h, j: (i, 0, j)),
                  pl.BlockSpec((1, N_HEADS, t), lambda i, h, j: (i, 0, 0))],
        out_specs=pl.BlockSpec((1, tq, gw), lambda i, h, j: (i, j, h)),
        out_shape=jax.ShapeDtypeStruct((b, t, N_HEADS * HEAD_W), BF16),
        scratch_shapes=[pltpu.VMEM((HEADS_PER_STEP, t, LANE), F32),
                        pltpu.VMEM((HEADS_PER_STEP, HEAD_W, tq), F32)],
        compiler_params=_params("parallel", "arbitrary", "arbitrary"),
        name="attn_d_prompt",
    )(z_qkv, z_qkv, z_qkv, cum, cum)


def _softmax_two(s_p, s_n, v_p, v_n):
    m = jnp.maximum(jnp.max(s_p, axis=-1, keepdims=True), jnp.max(s_n, axis=-1, keepdims=True))
    p_p = jnp.exp(s_p - m)
    p_n = jnp.exp(s_n - m)
    l = jnp.sum(p_p, axis=-1, keepdims=True) + jnp.sum(p_n, axis=-1, keepdims=True)
    acc = (jnp.dot(p_p.astype(BF16), v_p, preferred_element_type=F32)
           + jnp.dot(p_n.astype(BF16), v_n, preferred_element_type=F32))
    return acc / l


def _attn_a_sample_kernel(lq_ref, q_ref, kt_ref, vp_ref, kn_ref, vn_ref, bp_ref, bn_ref, g_ref, o_ref, *,
                          t, lam_init):
    lam = _lambda(lq_ref, lam_init)
    q_all, kn_all, vn_all = q_ref[0], kn_ref[0], vn_ref[0]
    for h in range(N_HEADS):
        q2 = _split_maps(_head(q_all, h))
        kn = _pad_rows(_head(kn_all, h), LANE).astype(BF16)
        vn = _pad_rows(_head(vn_all, h), LANE).astype(BF16)
        s_p = jnp.dot(q2, kt_ref[0, 0, h].astype(BF16), preferred_element_type=F32) + bp_ref[h, 0]
        s_n = lax.dot_general(q2, kn, NT, preferred_element_type=F32) + bn_ref[h, 0]
        o = _softmax_two(s_p, s_n, vp_ref[0, 0, :, h, :].astype(BF16), vn)
        o = o[:t] - lam * o[t:]
        o_ref[0, :, h * HEAD_W:(h + 1) * HEAD_W] = (_rms(o, g_ref[...]) * (1.0 - lam_init)).astype(o_ref.dtype)


def _attn_a_sample(z_qkv, k_cache_t, v_cache, l, bias_past, bias_new, lq, g_norm, *, lam_init):
    b, t, _ = z_qkv.shape
    past = v_cache.shape[2]
    bw = BRANCH_WIDTH
    return pl.pallas_call(
        functools.partial(_attn_a_sample_kernel, t=t, lam_init=lam_init),
        grid=(b,),
        in_specs=[pl.BlockSpec((4, HEAD_DIM_A), lambda i: (0, 0)),
                  pl.BlockSpec((1, t, bw), lambda i: (i, 0, 0)),
                  pl.BlockSpec((1, 1, N_HEADS, HEAD_W, past), lambda i: (l, i, 0, 0, 0)),
                  pl.BlockSpec((1, 1, past, N_HEADS, HEAD_W), lambda i: (l, i, 0, 0, 0)),
                  pl.BlockSpec((1, t, bw), lambda i: (i, 0, 1)),
                  pl.BlockSpec((1, t, bw), lambda i: (i, 0, 2)),
                  pl.BlockSpec((N_HEADS, 1, 2 * t, past), lambda i: (0, 0, 0, 0)),
                  pl.BlockSpec((N_HEADS, 1, 2 * t, LANE), lambda i: (0, 0, 0, 0)),
                  pl.BlockSpec((1, HEAD_W), lambda i: (0, 0))],
        out_specs=pl.BlockSpec((1, t, bw), lambda i: (i, 0, 0)),
        out_shape=jax.ShapeDtypeStruct((b, t, bw), BF16),
        compiler_params=_params("parallel"),
        name="attn_a_sample",
    )(lq, z_qkv, k_cache_t, v_cache, z_qkv, z_qkv, bias_past, bias_new, g_norm.reshape(1, HEAD_W))


def _attn_d_sample_kernel(q_ref, kp_ref, vp_ref, kn_ref, vn_ref, cq_ref, ckp_ref, ckn_ref, o_ref, *, t):
    scale = HEAD_W ** -0.5
    qry = lax.broadcasted_iota(jnp.int32, (t, LANE), 0)
    key = lax.broadcasted_iota(jnp.int32, (t, LANE), 1)
    q_all, kn_all, vn_all = q_ref[0], kn_ref[0], vn_ref[0]
    for h in range(N_HEADS):
        q = _head(q_all, h).astype(BF16)
        kn = _pad_rows(_head(kn_all, h), LANE).astype(BF16)
        vn = _pad_rows(_head(vn_all, h), LANE).astype(BF16)
        cq = cq_ref[0, h]
        s_p = lax.dot_general(q, kp_ref[0, 0, :, h, :].astype(BF16), NT, preferred_element_type=F32) * scale
        s_p = s_p + cq - ckp_ref[0, h:h + 1, :]
        s_n = lax.dot_general(q, kn, NT, preferred_element_type=F32) * scale
        s_n = s_n + cq - ckn_ref[0, h:h + 1, :]
        s_n = jnp.where(key <= qry, s_n, NEG_INF)
        o = _softmax_two(s_p, s_n, vp_ref[0, 0, :, h, :].astype(BF16), vn)
        o_ref[0, :, h * HEAD_W:(h + 1) * HEAD_W] = o.astype(o_ref.dtype)


def _attn_d_sample(z_qkv, k_cache, v_cache, l, cq, cum_past, cum_new):
    b, t, _ = z_qkv.shape
    past = k_cache.shape[2]
    bw = BRANCH_WIDTH
    cache_spec = pl.BlockSpec((1, 1, past, N_HEADS, HEAD_W), lambda i: (l, i, 0, 0, 0))
    return pl.pallas_call(
        functools.partial(_attn_d_sample_kernel, t=t),
        grid=(b,),
        in_specs=[pl.BlockSpec((1, t, bw), lambda i: (i, 0, 3)),
                  cache_spec, cache_spec,
                  pl.BlockSpec((1, t, bw), lambda i: (i, 0, 4)),
                  pl.BlockSpec((1, t, bw), lambda i: (i, 0, 5)),
                  pl.BlockSpec((1, N_HEADS, t, 1), lambda i: (i, 0, 0, 0)),
                  pl.BlockSpec((1, N_HEADS, past), lambda i: (i, 0, 0)),
                  pl.BlockSpec((1, N_HEADS, LANE), lambda i: (i, 0, 0))],
        out_specs=pl.BlockSpec((1, t, bw), lambda i: (i, 0, 0)),
        out_shape=jax.ShapeDtypeStruct((b, t, bw), BF16),
        compiler_params=_params("parallel"),
        name="attn_d_sample",
    )(z_qkv, k_cache, v_cache, z_qkv, z_qkv, cq, cum_past, cum_new)


def _logf_kernel(*refs, tt, has_past):
    refs = list(refs)
    h_ref, wf_ref, bf_ref = refs.pop(0), refs.pop(0), refs.pop(0)
    past_ref = refs.pop(0) if has_past else None
    logf_ref, cum_ref = refs.pop(0), refs.pop(0)
    cump_ref = refs.pop(0) if has_past else None
    carry_sc = refs.pop(0)

    @pl.when(pl.program_id(1) == 0)
    def _():
        if has_past:
            cump = _lane_cumsum(past_ref[0, 0])
            cump_ref[0] = cump
            carry_sc[...] = jnp.broadcast_to(cump[:, cump.shape[1] - 1:], carry_sc.shape)
        else:
            carry_sc[...] = jnp.zeros(carry_sc.shape, F32)

    ttp = logf_ref.shape[-1]
    hb = _pad_rows(h_ref[...], ttp)
    z = lax.dot_general(wf_ref[...].astype(BF16), hb, NT, preferred_element_type=F32) + bf_ref[...]
    logf = jnp.minimum(z, 0.0) - jnp.log1p(jnp.exp(-jnp.abs(z)))
    if tt < ttp:
        logf = jnp.where(lax.broadcasted_iota(jnp.int32, logf.shape, 1) < tt, logf, 0.0)
    logf_ref[0] = logf
    cum = _lane_cumsum(logf) + carry_sc[:, 0:1]
    cum_ref[0] = cum
    carry_sc[...] = jnp.broadcast_to(cum[:, ttp - 1:], carry_sc.shape)


def _logf(hf, b, w_in_t, l, b_forget, past_logf_t):
    m, d = hf.shape
    t = m // b
    tt = _tile(t, 1024, LANE) if t >= LANE else t
    ttp = max(tt, LANE)
    nt = t // tt
    has_past = past_logf_t is not None
    args = [hf, w_in_t, b_forget.reshape(N_HEADS, 1)]
    in_specs = [pl.BlockSpec((tt, d), lambda i, j: (i * nt + j, 0)),
                pl.BlockSpec((None, pl.Element(N_HEADS), pl.Element(d)), lambda i, j: (l, OFF_DF, 0)),
                pl.BlockSpec((N_HEADS, 1), lambda i, j: (0, 0))]
    out_shape = [jax.ShapeDtypeStruct((b, N_HEADS, nt * ttp), F32)] * 2
    out_specs = [pl.BlockSpec((1, N_HEADS, ttp), lambda i, j: (i, 0, j))] * 2
    if has_past:
        past = past_logf_t.shape[-1]
        args.append(past_logf_t)
        in_specs.append(pl.BlockSpec((1, 1, N_HEADS, past), lambda i, j: (l, i, 0, 0)))
        out_shape.append(jax.ShapeDtypeStruct((b, N_HEADS, past), F32))
        out_specs.append(pl.BlockSpec((1, N_HEADS, past), lambda i, j: (i, 0, 0)))
    outs = pl.pallas_call(
        functools.partial(_logf_kernel, tt=tt, has_past=has_past),
        grid=(b, nt), in_specs=in_specs, out_specs=out_specs, out_shape=out_shape,
        scratch_shapes=[pltpu.VMEM((N_HEADS, LANE), F32)],
        compiler_params=_params("parallel", "arbitrary"),
        name="logf_cumsum",
    )(*args)
    return outs[0], outs[1], (outs[2] if has_past else None)


def _lru_kernel(cx_ref, cg_ref, buf0_ref, h0_ref, cw_ref, cb_ref, wr_ref, br_ref, wi_ref, bi_ref,
                lam_ref, oc_ref, nbuf_ref, hl_ref, tail_sc, h_sc, *, tt):
    w = cx_ref.shape[-1]
    blk = w // LRU_BLOCKS
    pad = SUBLANE - (CONV_WIDTH - 1)

    @pl.when(pl.program_id(1) == 0)
    def _():
        tail_sc[0:pad, :] = jnp.zeros((pad, w), F32)
        tail_sc[pad:SUBLANE, :] = buf0_ref[0]
        h_sc[...] = h0_ref[0]

    full = jnp.concatenate([tail_sc[...], cx_ref[0]], axis=0)
    cw = cw_ref[...]
    xc = cb_ref[...]
    for j in range(CONV_WIDTH):
        xc = xc + full[pad + j:pad + j + tt] * cw[j:j + 1]
    tail_sc[...] = full[tt:tt + SUBLANE]
    nbuf_ref[0] = full[tt + pad:tt + SUBLANE]

    xb = xc.astype(BF16)

    def gate(w_ref, b_ref):
        parts = [jnp.dot(xb[:, g * blk:(g + 1) * blk], w_ref[g].astype(BF16), preferred_element_type=F32)
                 for g in range(LRU_BLOCKS)]
        return jax.nn.sigmoid(jnp.concatenate(parts, axis=1) + b_ref[...])

    r = gate(wr_ref, br_ref)
    i = gate(wi_ref, bi_ref)
    nl = -lam_ref[...]
    softplus = jnp.maximum(nl, 0.0) + jnp.log1p(jnp.exp(-jnp.abs(nl)))
    log_a = (-LRU_C * r) * softplus
    a = jnp.exp(log_a)
    u = jnp.sqrt(-jnp.tanh(log_a) * (a * a + 1.0)) * (i * xc)

    row = lax.broadcasted_iota(jnp.int32, (tt, w), 0)
    s = 1
    while s < tt:
        valid = row >= s
        u = jnp.where(valid, a * pltpu.roll(u, s, 0) + u, u)
        a = jnp.where(valid, a * pltpu.roll(a, s, 0), a)
        s *= 2
    h = u + a * h_sc[...]
    h_sc[...] = h[tt - 1:tt]
    hl_ref[0] = h[tt - 1:tt]
    oc_ref[0] = (h * _gelu(cg_ref[0])).astype(oc_ref.dtype)


def _lru(zcb, conv_buf, h0, conv_w, conv_b, w_r, b_r, w_i, b_i, lam):
    b, t, _ = zcb.shape
    w = BRANCH_WIDTH
    tt = _tile(t, 256, SUBLANE)
    vec = pl.BlockSpec((1, w), lambda i, j: (0, 0))
    mat = pl.BlockSpec((LRU_BLOCKS, w // LRU_BLOCKS, w // LRU_BLOCKS), lambda i, j: (0, 0, 0))
    return pl.pallas_call(
        functools.partial(_lru_kernel, tt=tt),
        grid=(b, t // tt),
        in_specs=[pl.BlockSpec((1, tt, w), lambda i, j: (i, j, REST_CX)),
                  pl.BlockSpec((1, tt, w), lambda i, j: (i, j, REST_CG)),
                  pl.BlockSpec((1, CONV_WIDTH - 1, w), lambda i, j: (i, 0, 0)),
                  pl.BlockSpec((1, 1, w), lambda i, j: (i, 0, 0)),
                  pl.BlockSpec((CONV_WIDTH, w), lambda i, j: (0, 0)),
                  vec, mat, vec, mat, vec, vec],
        out_specs=[pl.BlockSpec((1, tt, w), lambda i, j: (i, j, 0)),
                   pl.BlockSpec((1, CONV_WIDTH - 1, w), lambda i, j: (i, 0, 0)),
                   pl.BlockSpec((1, 1, w), lambda i, j: (i, 0, 0))],
        out_shape=[jax.ShapeDtypeStruct((b, t, w), BF16),
                   jax.ShapeDtypeStruct((b, CONV_WIDTH - 1, w), F32),
                   jax.ShapeDtypeStruct((b, 1, w), F32)],
        scratch_shapes=[pltpu.VMEM((SUBLANE, w), F32), pltpu.VMEM((1, w), F32)],
        compiler_params=_params("parallel", "arbitrary"),
        name="conv_lru",
    )(zcb, zcb, conv_buf, h0.reshape(b, 1, w), conv_w, conv_b.reshape(1, w), w_r, b_r.reshape(1, w),
      w_i, b_i.reshape(1, w), lam.reshape(1, w))


def _gmlp_kernel(zu_ref, zv_ref, lng_ref, lnb_ref, ws_ref, bst_ref, ob_ref, *vn_refs, ct):
    u = _gelu(zu_ref[0])
    v = _gelu(zv_ref[0])
    w = v.shape[-1]
    gw = w // GMLP_GROUPS
    d = v - jnp.mean(v, axis=-1, keepdims=True)
    vn = d * lax.rsqrt(jnp.mean(d * d, axis=-1, keepdims=True) + EPS) * lng_ref[...] + lnb_ref[...]
    if vn_refs:
        vn_refs[0][0] = vn
    vb = _pad_rows(vn, GMLP_CHUNK).astype(BF16)
    tril = (lax.broadcasted_iota(jnp.int32, (GMLP_CHUNK, GMLP_CHUNK), 0)
            >= lax.broadcasted_iota(jnp.int32, (GMLP_CHUNK, GMLP_CHUNK), 1))
    parts = []
    for g in range(GMLP_GROUPS):
        wg = jnp.where(tril, ws_ref[g], 0.0).astype(BF16)
        s = jnp.dot(wg, vb[:, g * gw:(g + 1) * gw], preferred_element_type=F32) + bst_ref[:, g:g + 1]
        parts.append(s[:ct])
    ob_ref[0] = (u * jnp.concatenate(parts, axis=1)).astype(ob_ref.dtype)


def _gmlp(zcb, ln_g, ln_b, ws, bs, want_rows):
    b, t, _ = zcb.shape
    w = BRANCH_WIDTH
    ct = min(t, GMLP_CHUNK)
    vec = pl.BlockSpec((1, w), lambda i, j: (0, 0))
    row_out = pl.BlockSpec((1, ct, w), lambda i, j: (i, j, 0))
    out_shape = [jax.ShapeDtypeStruct((b, t, w), BF16)]
    out_specs = [row_out]
    if want_rows:
        out_shape.append(jax.ShapeDtypeStruct((b, t, w), F32))
        out_specs.append(row_out)
    outs = pl.pallas_call(
        functools.partial(_gmlp_kernel, ct=ct),
        grid=(b, t // ct),
        in_specs=[pl.BlockSpec((1, ct, w), lambda i, j: (i, j, REST_BU)),
                  pl.BlockSpec((1, ct, w), lambda i, j: (i, j, REST_BV)),
                  vec, vec,
                  pl.BlockSpec((GMLP_GROUPS, GMLP_CHUNK, GMLP_CHUNK), lambda i, j: (0, 0, 0)),
                  pl.BlockSpec((GMLP_CHUNK, GMLP_GROUPS), lambda i, j: (0, 0))],
        out_specs=out_specs, out_shape=out_shape,
        compiler_params=_params("parallel", "parallel"),
        name="gmlp",
    )(zcb, zcb, ln_g.reshape(1, w), ln_b.reshape(1, w), ws, bs.T)
    return (outs[0], outs[1]) if want_rows else (outs[0], None)


def _mixers(h, l, p, caches, bias):
    b, t, d = h.shape
    m = b * t
    hf = h.reshape(m, d)
    bw = BRANCH_WIDTH
    w_in_t = p['w_in_t']
    lam_init = 0.8 - 0.6 * math.exp(-0.3 * l)

    z_qkv = _matmul_nt(hf, w_in_t, l, 0, OFF_DF, F32).reshape(b, t, OFF_DF)
    zcb = _matmul_nt(hf, w_in_t, l, OFF_CX, 4 * bw, F32).reshape(b, t, 4 * bw)
    ka = z_qkv[:, :, 1 * bw:2 * bw]
    va = z_qkv[:, :, 2 * bw:3 * bw]
    kd = z_qkv[:, :, 4 * bw:5 * bw]
    vd = z_qkv[:, :, 5 * bw:6 * bw]

    if caches is None:
        logf_t, cum, _ = _logf(hf, b, w_in_t, l, p['b_forget'][l], None)
        tq = bias.shape[2]
        o_a = _attn_a_prompt(z_qkv, bias, p['lambda_qk'][l], p['diff_norm_g'][l], tq=tq, lam_init=lam_init)
        o_d = _attn_d_prompt(z_qkv, cum, tq=tq)
        conv_buf = jnp.zeros((b, CONV_WIDTH - 1, bw), F32)
        lru_h0 = jnp.zeros((b, bw), F32)
    else:
        a_k_t, a_v, d_k, d_v, d_logf_t, state_conv, state_lru = caches
        logf_t, cum_new, cum_past = _logf(hf, b, w_in_t, l, p['b_forget'][l], d_logf_t)
        logf_t = logf_t[:, :, :t]
        cq = cum_new[:, :, :t].reshape(b, N_HEADS, t, 1)
        o_a = _attn_a_sample(z_qkv, a_k_t, a_v, l, bias[0], bias[1], p['lambda_qk'][l], p['diff_norm_g'][l],
                             lam_init=lam_init)
        o_d = _attn_d_sample(z_qkv, d_k, d_v, l, cq, cum_past, cum_new)
        conv_buf, lru_h0 = state_conv[l], state_lru[l]

    o_c, new_buf, h_last = _lru(zcb, conv_buf, lru_h0, p['conv_w'][l], p['conv_b'][l], p['w_rgate'][l],
                                p['b_rgate'][l], p['w_igate'][l], p['b_igate'][l], p['lru_lambda'][l])
    o_b, v_rows = _gmlp(zcb, p['gmlp_ln_g'][l], p['gmlp_ln_b'][l], p['gmlp_ws'][l], p['gmlp_bs'][l],
                        want_rows=caches is not None)

    y = _merge(hf, [o.reshape(m, bw) for o in (o_a, o_b, o_c, o_d)], p['w_gate'], p['w_out'], l)
    states = (ka.reshape(b, t, N_HEADS, 2, HEAD_DIM_A), va.reshape(b, t, N_HEADS, HEAD_W),
              kd.reshape(b, t, N_HEADS, HEAD_W), vd.reshape(b, t, N_HEADS, HEAD_W),
              logf_t.transpose(0, 2, 1), new_buf, h_last.reshape(b, bw), v_rows)
    return y, states


def _ffn(h, l, j, p):
    b, t, d = h.shape
    idx = l * 2 + j
    mid = _ffn_up(h.reshape(b * t, d), p['w_ffn_gate'], p['w_ffn_up'], idx)
    return _matmul(mid, p['w_ffn_down'], idx, F32, tm_target=1024, tn_target=256).reshape(b, t, d)


def _run_group(x, mods, p, caches, bias):
    depth = len(mods)
    states = []
    _, h = _norm(x, pre=(mods[0], p['g_pre'][0], 0))
    for l in range(depth):
        mod, g_pre, g_post = mods[l], p['g_pre'][l], p['g_post'][l]
        x, h = _norm(x, _ffn(h, l, 0, p), post=(mod, g_post, 0, 0.5), pre=(mod, g_pre, 1))
        y, st = _mixers(h, l, p, caches, bias)
        states.append(st)
        x, h = _norm(x, y.reshape(x.shape), post=(mod, g_post, 1, 1.0), pre=(mod, g_pre, 2))
        nxt = (mods[l + 1], p['g_pre'][l + 1], 0) if l + 1 < depth else None
        x, h = _norm(x, _ffn(h, l, 1, p), post=(mod, g_post, 2, 0.5), pre=nxt)
    return x, states


def kernel(x_prompt, x_sample, c_prompt, c_sample, cache_a_k, cache_a_v, cache_d_k, cache_d_v, cache_d_logf, state_conv, state_lru, w_ada, b_ada, g_pre, g_post, w_ffn_gate, w_ffn_up, w_ffn_down, w_in, w_out, rel_bias, lambda_qk, diff_norm_g, gmlp_ln_g, gmlp_ln_b, gmlp_ws, gmlp_bs, conv_w, conv_b, w_rgate, b_rgate, w_igate, b_igate, lru_lambda, b_forget):
    depth, d_model, _ = w_in.shape
    bp, seq, _ = x_prompt.shape
    bs, dec_seq, _ = x_sample.shape
    past = cache_a_k.shape[2]
    d_ff = w_ffn_gate.shape[-1]
    p = {'g_pre': g_pre, 'g_post': g_post,
         'w_ffn_gate': w_ffn_gate.reshape(depth * 2, d_model, d_ff),
         'w_ffn_up': w_ffn_up.reshape(depth * 2, d_model, d_ff),
         'w_ffn_down': w_ffn_down.reshape(depth * 2, d_ff, d_model).astype(BF16),
         'w_in_t': jnp.swapaxes(w_in, 1, 2),
         'w_gate': jnp.swapaxes(w_in, 1, 2)[:, OFF_GATE:].astype(BF16),
         'w_out': w_out.reshape(depth * N_BRANCH, BRANCH_WIDTH, d_model),
         'lambda_qk': lambda_qk, 'diff_norm_g': diff_norm_g, 'gmlp_ln_g': gmlp_ln_g,
         'gmlp_ln_b': gmlp_ln_b, 'gmlp_ws': gmlp_ws, 'gmlp_bs': gmlp_bs, 'conv_w': conv_w,
         'conv_b': conv_b, 'w_rgate': w_rgate, 'b_rgate': b_rgate, 'w_igate': w_igate,
         'b_igate': b_igate, 'lru_lambda': lru_lambda, 'b_forget': b_forget}

    mod = _ada(jnp.concatenate([c_prompt, c_sample], axis=0), w_ada, b_ada)
    mod = mod.reshape(depth, bp + bs, 3 * N_SUB, d_model)
    mods_p = [mod[l, :bp] for l in range(depth)]
    mods_s = [mod[l, bp:] for l in range(depth)]

    tq = _tile(seq, 256, LANE)
    assert tq + 1 >= T5_SATURATION and tq % CHUNK == 0
    bias_p = _bias_tiles(rel_bias, tq, tq, [(0, 0), (tq, 0), None], True, seq)
    xp, st_p = _run_group(x_prompt, mods_p, p, None, bias_p)

    bias_s = (_bias_tiles(rel_bias, dec_seq, past, [(past, 0)], False, past + dec_seq),
              _bias_tiles(rel_bias, dec_seq, LANE, [(past, past)], False, past + dec_seq))
    caches = (cache_a_k.transpose(0, 1, 3, 4, 5, 2).reshape(depth, bs, N_HEADS, HEAD_W, past),
              cache_a_v, cache_d_k, cache_d_v,
              cache_d_logf.transpose(0, 1, 3, 2), state_conv, state_lru)
    xs, st_s = _run_group(x_sample, mods_s, p, caches, bias_s)

    np_ = [jnp.stack([st_p[l][j] for l in range(depth)]) for j in range(7)]
    ns_ = [jnp.stack([st_s[l][j] for l in range(depth)]) for j in range(8)]
    return (xp, xs, *np_, *ns_)
```

```python
import functools
import math

import jax
import jax.numpy as jnp
from jax import lax
from jax.experimental import pallas as pl
from jax.experimental.pallas import tpu as pltpu

F32 = jnp.float32
BF16 = jnp.bfloat16

LANE = 128
SUBLANE = 8
VMEM_LIMIT_BYTES = 56 * 1024 * 1024

CHUNK = 64
CHUNK_SHIFT = 6
N_HEADS = 8
HEAD_W = 128
HEAD_DIM_A = 64
N_BRANCH = 4
BRANCH_WIDTH = 1024
LRU_BLOCKS = 8
LRU_C = 8.0
CONV_WIDTH = 4
GMLP_GROUPS = 8
GMLP_CHUNK = 128
NUM_BUCKETS = 32
N_SUB = 3
EPS = 1e-6
NEG_INF = -1e30
LOG2E = 1.0 / math.log(2.0)
T5_LARGE_STARTS = (12, 16, 23, 32, 46, 64, 91)
T5_SATURATION = 91
T5_FAR_BUCKET = NUM_BUCKETS // 2 - 1

OFF_DF = 6 * BRANCH_WIDTH
OFF_CX = OFF_DF + N_HEADS
OFF_GATE = OFF_CX + 4 * BRANCH_WIDTH
REST_CX, REST_CG, REST_BU, REST_BV = 0, 1, 2, 3

NT = (((1,), (1,)), ((), ()))
TN = (((0,), (0,)), ((), ()))


def _params(*sem):
    return pltpu.CompilerParams(dimension_semantics=sem, vmem_limit_bytes=VMEM_LIMIT_BYTES)


def _tile(n, target, mult):
    best = None
    for t in range(mult, min(n, target) + 1, mult):
        if n % t == 0:
            best = t
    return n if best is None else best


def _resident(shape, index_map):
    return pl.BlockSpec(shape, index_map, pipeline_mode=pl.Buffered(1))


def _gelu(x):
    return x * (0.5 * (1.0 + jnp.tanh(math.sqrt(2.0 / math.pi) * (x + 0.044715 * (x * x * x)))))


def _rms(x, g):
    return x * lax.rsqrt(jnp.mean(x * x, axis=-1, keepdims=True) + EPS) * g


def _pad_rows(x, rows):
    if x.shape[0] == rows:
        return x
    return jnp.concatenate([x, jnp.zeros((rows - x.shape[0],) + x.shape[1:], x.dtype)], axis=0)


def _lane_cumsum(x):
    n = x.shape[-1]
    lane = lax.broadcasted_iota(jnp.int32, x.shape, x.ndim - 1)
    s = 1
    while s < n:
        x = x + jnp.where(lane >= s, pltpu.roll(x, s, x.ndim - 1), 0.0)
        s *= 2
    return x


def _ada_kernel(c_ref, w_ref, b_ref, o_ref):
    c = c_ref[...]
    a = (c * jax.nn.sigmoid(c)).astype(BF16)
    o_ref[0] = jnp.dot(a, w_ref[0].astype(BF16), preferred_element_type=F32) + b_ref[0]


def _ada(c_all, w_ada, b_ada):
    depth, d, n = w_ada.shape
    bt = c_all.shape[0]
    tn = _tile(n, 512, LANE)
    return pl.pallas_call(
        _ada_kernel,
        grid=(depth, n // tn),
        in_specs=[pl.BlockSpec((bt, d), lambda l, j: (0, 0)),
                  pl.BlockSpec((1, d, tn), lambda l, j: (l, 0, j)),
                  pl.BlockSpec((1, 1, tn), lambda l, j: (l, 0, j))],
        out_specs=pl.BlockSpec((1, bt, tn), lambda l, j: (l, 0, j)),
        out_shape=jax.ShapeDtypeStruct((depth, bt, n), F32),
        compiler_params=_params("parallel", "parallel"),
        name="ada",
    )(c_all, w_ada, b_ada.reshape(depth, 1, n))


def _norm_kernel(*refs, k_post, coef, k_pre):
    refs = list(refs)
    x_ref = refs.pop(0)
    x = x_ref[0]
    if k_post is not None:
        y_ref, mpost_ref, gpost_ref = refs.pop(0), refs.pop(0), refs.pop(0)
    if k_pre is not None:
        mpre_ref, gpre_ref = refs.pop(0), refs.pop(0)
    if k_post is not None:
        xo_ref = refs.pop(0)
        m = mpost_ref[0]
        gate = m[3 * k_post + 2:3 * k_post + 3]
        x = x + (coef * gate) * _rms(y_ref[0], gpost_ref[k_post:k_post + 1, :])
        xo_ref[0] = x
    if k_pre is not None:
        ho_ref = refs.pop(0)
        m = mpre_ref[0]
        shift = m[3 * k_pre:3 * k_pre + 1]
        scale = m[3 * k_pre + 1:3 * k_pre + 2]
        ho_ref[0] = (_rms(x, gpre_ref[k_pre:k_pre + 1, :]) * (1.0 + scale) + shift).astype(BF16)


def _norm(x, y=None, post=None, pre=None):
    b, t, d = x.shape
    tt = _tile(t, 256, SUBLANE)
    row = pl.BlockSpec((1, tt, d), lambda i, j: (i, j, 0))
    mod_spec = pl.BlockSpec((1, 3 * N_SUB, d), lambda i, j: (i, 0, 0))
    g_spec = pl.BlockSpec((N_SUB, d), lambda i, j: (0, 0))
    args, in_specs, out_shape, out_specs = [x], [row], [], []
    if post is not None:
        args += [y, post[0], post[1]]
        in_specs += [row, mod_spec, g_spec]
        out_shape.append(jax.ShapeDtypeStruct((b, t, d), F32))
        out_specs.append(row)
    if pre is not None:
        args += [pre[0], pre[1]]
        in_specs += [mod_spec, g_spec]
        out_shape.append(jax.ShapeDtypeStruct((b, t, d), BF16))
        out_specs.append(row)
    outs = pl.pallas_call(
        functools.partial(_norm_kernel, k_post=None if post is None else post[2],
                          coef=None if post is None else post[3],
                          k_pre=None if pre is None else pre[2]),
        grid=(b, t // tt), in_specs=in_specs, out_specs=out_specs, out_shape=out_shape,
        compiler_params=_params("parallel", "parallel"),
        name="norm",
    )(*args)
    outs = list(outs)
    x_new = outs.pop(0) if post is not None else None
    h = outs.pop(0) if pre is not None else None
    return x_new, h


def _mm_kernel(x_ref, w_ref, o_ref):
    o_ref[...] = jnp.dot(x_ref[...], w_ref[0].astype(BF16),
                         preferred_element_type=F32).astype(o_ref.dtype)


def _matmul(x, w, l, out_dtype, tm_target, tn_target):
    m, k = x.shape
    n = w.shape[-1]
    tm = _tile(m, tm_target, 16)
    tn = _tile(n, tn_target, LANE)
    return pl.pallas_call(
        _mm_kernel,
        grid=(m // tm, n // tn),
        in_specs=[_resident((tm, k), lambda i, j: (i, 0)),
                  pl.BlockSpec((1, k, tn), lambda i, j: (l, 0, j))],
        out_specs=pl.BlockSpec((tm, tn), lambda i, j: (i, j)),
        out_shape=jax.ShapeDtypeStruct((m, n), out_dtype),
        compiler_params=_params("parallel", "arbitrary"),
        name="matmul",
    )(x, w)


def _mm_nt_kernel(x_ref, w_ref, *o_refs):
    z = lax.dot_general(x_ref[...], w_ref[...].astype(BF16), NT, preferred_element_type=F32)
    for o_ref in o_refs:
        o_ref[...] = z.astype(o_ref.dtype)


def _matmul_nt(x, wt, l, row0, nrows, out_dtypes, tm_target=2048, tn_target=512):
    m, k = x.shape
    tm = _tile(m, tm_target, 16)
    tn = _tile(nrows, tn_target, LANE)
    return pl.pallas_call(
        _mm_nt_kernel,
        grid=(m // tm, nrows // tn),
        in_specs=[_resident((tm, k), lambda i, j: (i, 0)),
                  pl.BlockSpec((None, pl.Element(tn), pl.Element(k)),
                               lambda i, j: (l, pl.multiple_of(row0 + j * tn, SUBLANE), 0))],
        out_specs=[pl.BlockSpec((tm, tn), lambda i, j: (i, j)) for _ in out_dtypes],
        out_shape=[jax.ShapeDtypeStruct((m, nrows), dt) for dt in out_dtypes],
        compiler_params=_params("parallel", "arbitrary"),
        name="matmul_nt",
    )(x, wt)


def _ffn_up_kernel(x_ref, wg_ref, wu_ref, o_ref):
    x = x_ref[...]
    g = jnp.dot(x, wg_ref[0].astype(BF16), preferred_element_type=F32)
    u = jnp.dot(x, wu_ref[0].astype(BF16), preferred_element_type=F32)
    o_ref[...] = ((g * jax.nn.sigmoid(g)) * u).astype(o_ref.dtype)


def _ffn_up(x, wg, wu, l):
    m, k = x.shape
    n = wg.shape[-1]
    tm = _tile(m, 2048, 16)
    tn = _tile(n, 256, LANE)
    w_spec = pl.BlockSpec((1, k, tn), lambda i, j: (l, 0, j))
    return pl.pallas_call(
        _ffn_up_kernel,
        grid=(m // tm, n // tn),
        in_specs=[_resident((tm, k), lambda i, j: (i, 0)), w_spec, w_spec],
        out_specs=pl.BlockSpec((tm, tn), lambda i, j: (i, j)),
        out_shape=jax.ShapeDtypeStruct((m, n), BF16),
        compiler_params=_params("parallel", "arbitrary"),
        name="ffn_up",
    )(x, wg, wu)


def _merge_kernel(h_ref, oa_ref, ob_ref, oc_ref, od_ref, wg0, wg1, wg2, wg3, wo0, wo1, wo2, wo3, y_ref):
    h = h_ref[...]
    y = None
    for o_ref, wg, wo in ((oa_ref, wg0, wo0), (ob_ref, wg1, wo1), (oc_ref, wg2, wo2), (od_ref, wg3, wo3)):
        gate = jax.nn.sigmoid(lax.dot_general(h, wg[0], NT, preferred_element_type=F32))
        proj = jnp.dot(o_ref[...], wo[0].astype(BF16), preferred_element_type=F32)
        y = gate * proj if y is None else y + gate * proj
    y_ref[...] = y


def _merge(h, branches, w_gate, w_out, l):
    m, d = h.shape
    n = w_out.shape[-1]
    bw = branches[0].shape[-1]
    tm = _tile(m, 1024, 16)
    tn = _tile(n, 256, LANE)
    nj = n // tn

    def gate_spec(i):
        return pl.BlockSpec((1, tn, d), lambda a, j: (l, i * nj + j, 0))

    def out_spec(i):
        return pl.BlockSpec((1, bw, tn), lambda a, j: (l * N_BRANCH + i, 0, j))

    return pl.pallas_call(
        _merge_kernel,
        grid=(m // tm, nj),
        in_specs=[_resident((tm, d), lambda a, j: (a, 0))] + [_resident((tm, bw), lambda a, j: (a, 0))] * N_BRANCH
        + [gate_spec(i) for i in range(N_BRANCH)] + [out_spec(i) for i in range(N_BRANCH)],
        out_specs=pl.BlockSpec((tm, tn), lambda a, j: (a, j)),
        out_shape=jax.ShapeDtypeStruct((m, n), F32),
        compiler_params=_params("parallel", "arbitrary"),
        name="merge",
    )(h, *branches, w_gate, w_gate, w_gate, w_gate, w_out, w_out, w_out, w_out)


def _bias_kernel(tab_ref, o_ref, *, nq, nk, tiles, transposed, kv_end):
    h = pl.program_id(0)
    shape = (nk, nq) if transposed else (nq, nk)
    q_i = lax.broadcasted_iota(jnp.int32, shape, 1 if transposed else 0)
    k_i = lax.broadcasted_iota(jnp.int32, shape, 0 if transposed else 1)
    for c, tile in enumerate(tiles):
        for mp in range(2):
            if tile is None:
                val = jnp.full(shape, NEG_INF, F32)
            else:
                q0, k0 = tile
                q_pos, k_pos = q0 + q_i, k0 + k_i
                rel = k_pos - q_pos
                n = jnp.abs(rel)
                large = NUM_BUCKETS // 4
                for start in T5_LARGE_STARTS:
                    large = large + jnp.where(n >= start, 1, 0)
                bucket = jnp.where(rel > 0, NUM_BUCKETS // 2, 0) + jnp.where(n < NUM_BUCKETS // 4, n, large)
                val = jnp.zeros(shape, F32)
                for b in range(NUM_BUCKETS):
                    val = jnp.where(bucket == b, tab_ref[b * (2 * N_HEADS) + h * 2 + mp], val)
                val = val - tab_ref[T5_FAR_BUCKET * (2 * N_HEADS) + h * 2 + mp]
                visible = jnp.logical_and((k_pos >> CHUNK_SHIFT) <= (q_pos >> CHUNK_SHIFT), k_pos < kv_end)
                val = jnp.where(visible, val, NEG_INF)
            if transposed:
                o_ref[0, c, :, mp * nq:(mp + 1) * nq] = val
            else:
                o_ref[0, c, mp * nq:(mp + 1) * nq, :] = val


def _bias_tiles(rel_bias, nq, nk, tiles, transposed, kv_end):
    shape = (nk, 2 * nq) if transposed else (2 * nq, nk)
    return pl.pallas_call(
        functools.partial(_bias_kernel, nq=nq, nk=nk, tiles=tuple(tiles), transposed=transposed, kv_end=kv_end),
        grid=(N_HEADS,),
        in_specs=[pl.BlockSpec(memory_space=pltpu.SMEM)],
        out_specs=pl.BlockSpec((1, len(tiles)) + shape, lambda h: (h, 0, 0, 0)),
        out_shape=jax.ShapeDtypeStruct((N_HEADS, len(tiles)) + shape, F32),
        compiler_params=_params("parallel"),
        name="bias_tiles",
    )(rel_bias.reshape(-1))


def _lambda(lq_ref, lam_init):
    lq = lq_ref[...]
    return (jnp.exp(jnp.sum(lq[0:1] * lq[1:2], axis=-1, keepdims=True))
            - jnp.exp(jnp.sum(lq[2:3] * lq[3:4], axis=-1, keepdims=True)) + lam_init)


def _split_maps(q):
    lane = lax.broadcasted_iota(jnp.int32, q.shape, 1)
    q = q * (HEAD_DIM_A ** -0.5)
    return jnp.concatenate([jnp.where(lane < HEAD_DIM_A, q, 0.0),
                            jnp.where(lane >= HEAD_DIM_A, q, 0.0)], axis=0).astype(BF16)


def _flash_tile_t(carries, acc_sc, scores, values, query_terms=None):
    out, probs, alphas = [], [], []
    for g, ((m, l), s) in enumerate(zip(carries, scores)):
        if query_terms is None:
            m_new = jnp.maximum(m, jnp.max(s, axis=0, keepdims=True))
            p = jnp.exp(s - m_new)
            alpha = jnp.exp(m - m_new)
        else:
            m_new = jnp.maximum(m, jnp.max(s, axis=0, keepdims=True) + query_terms[g])
            p = jnp.exp2(s - (m_new - query_terms[g]))
            alpha = jnp.exp2(m - m_new)
        out.append((m_new, alpha * l + jnp.sum(p, axis=0, keepdims=True)))
        probs.append(p.astype(BF16))
        alphas.append(alpha)
    for g, (v, p, alpha) in enumerate(zip(values, probs, alphas)):
        acc_sc[g] = alpha * acc_sc[g] + lax.dot_general(v, p, TN, preferred_element_type=F32)
    return tuple(out)


def _flash_init(acc_ref):
    acc_ref[...] = jnp.zeros(acc_ref.shape, F32)
    nq = acc_ref.shape[-1]
    return jnp.full((1, nq), NEG_INF, F32), jnp.zeros((1, nq), F32)


HEADS_PER_STEP = 8


def _head(x, g):
    return x[:, g * HEAD_W:(g + 1) * HEAD_W]


def _attn_a_prompt_kernel(lq_ref, q_ref, k_ref, v_ref, bias_ref, g_ref, o_ref, acc_sc, *, tq, lam_init):
    qi = pl.program_id(2)
    heads = range(HEADS_PER_STEP)
    q_all = q_ref[0]
    q2 = [_split_maps(_head(q_all, g)) for g in heads]

    def tile(kt, carries, bias_tile):
        k0 = pl.multiple_of(kt * tq, tq)
        k_all = k_ref[0, pl.ds(k0, tq), :]
        v_all = v_ref[0, pl.ds(k0, tq), :]
        scores = [lax.dot_general(_head(k_all, g).astype(BF16), q2[g], NT, preferred_element_type=F32)
                  for g in heads]
        if bias_tile is not None:
            scores = [s + bias_ref[g, bias_tile] for g, s in enumerate(scores)]
        return _flash_tile_t(carries, acc_sc, scores, [_head(v_all, g).astype(BF16) for g in heads])

    carries = tile(qi, tuple(_flash_init(acc_sc.at[g]) for g in heads), 0)
    carries = tile(jnp.maximum(qi - 1, 0), carries, jnp.where(qi == 0, 2, 1))
    carries = lax.fori_loop(0, qi - 1, lambda kt, c: tile(kt, c, None), carries)
    lam = _lambda(lq_ref, lam_init)
    for g in heads:
        o = acc_sc[g] * (1.0 / carries[g][1])
        o = (o[:, :tq] - lam * o[:, tq:]).T
        o_ref[0, :, g * HEAD_W:(g + 1) * HEAD_W] = (_rms(o, g_ref[...]) * (1.0 - lam_init)).astype(o_ref.dtype)


def _attn_a_prompt(z_qkv, bias, lq, g_norm, *, tq, lam_init):
    b, t, _ = z_qkv.shape
    gw = HEADS_PER_STEP * HEAD_W
    nblk = BRANCH_WIDTH // gw
    return pl.pallas_call(
        functools.partial(_attn_a_prompt_kernel, tq=tq, lam_init=lam_init),
        grid=(b, nblk, t // tq),
        in_specs=[pl.BlockSpec((4, HEAD_DIM_A), lambda i, h, j: (0, 0)),
                  pl.BlockSpec((1, tq, gw), lambda i, h, j: (i, j, h)),
                  _resident((1, t, gw), lambda i, h, j: (i, 0, nblk + h)),
                  _resident((1, t, gw), lambda i, h, j: (i, 0, 2 * nblk + h)),
                  _resident((HEADS_PER_STEP, 3, tq, 2 * tq), lambda i, h, j: (h, 0, 0, 0)),
                  pl.BlockSpec((1, HEAD_W), lambda i, h, j: (0, 0))],
        out_specs=pl.BlockSpec((1, tq, gw), lambda i, h, j: (i, j, h)),
        out_shape=jax.ShapeDtypeStruct((b, t, N_HEADS * HEAD_W), BF16),
        scratch_shapes=[pltpu.VMEM((HEADS_PER_STEP, HEAD_W, 2 * tq), F32)],
        compiler_params=_params("parallel", "parallel", "arbitrary"),
        name="attn_a_prompt",
    )(lq, z_qkv, z_qkv, z_qkv, bias, g_norm.reshape(1, HEAD_W))


def _attn_d_prompt_kernel(q_ref, k_ref, v_ref, cq_ref, ck_ref, o_ref, ckb_sc, acc_sc, *, tq, t):
    h0 = pl.program_id(1) * HEADS_PER_STEP
    qi = pl.program_id(2)
    heads = range(HEADS_PER_STEP)

    @pl.when(qi == 0)
    def _():
        for g in heads:
            ck = ck_ref[0, pl.ds(h0 + g, 1), :] * LOG2E
            for c in range(t // LANE):
                ckb_sc[g, c * LANE:(c + 1) * LANE, :] = jnp.broadcast_to(
                    ck[:, c * LANE:(c + 1) * LANE], (LANE, LANE)).T

    q_all = q_ref[0]
    q = [_head(q_all, g).astype(BF16) for g in heads]
    cq = [cq_ref[0, pl.ds(h0 + g, 1), :] * LOG2E for g in heads]
    scale = HEAD_W ** -0.5 * LOG2E

    def tile(kt, carries, diagonal):
        k0 = pl.multiple_of(kt * tq, tq)
        k_all = k_ref[0, pl.ds(k0, tq), :]
        v_all = v_ref[0, pl.ds(k0, tq), :]
        scores = [lax.dot_general(_head(k_all, g).astype(BF16), q[g], NT, preferred_element_type=F32)
                  for g in heads]
        for g in heads:
            ckb = ckb_sc[g, pl.ds(k0, tq), :]
            s = scores[g] * scale - jnp.concatenate([ckb] * (tq // LANE), axis=1)
            if diagonal:
                key = lax.broadcasted_iota(jnp.int32, (tq, tq), 0)
                qry = lax.broadcasted_iota(jnp.int32, (tq, tq), 1)
                s = jnp.where(key <= qry, s, NEG_INF)
            scores[g] = s
        return _flash_tile_t(carries, acc_sc, scores, [_head(v_all, g).astype(BF16) for g in heads], cq)

    carries = tile(qi, tuple(_flash_init(acc_sc.at[g]) for g in heads), True)
    carries = lax.fori_loop(0, qi, lambda kt, c: tile(kt, c, False), carries)
    for g in heads:
        o = acc_sc[g] * (1.0 / carries[g][1])
        o_ref[0, :, g * HEAD_W:(g + 1) * HEAD_W] = o.T.astype(o_ref.dtype)


def _attn_d_prompt(z_qkv, cum, *, tq):
    b, t, _ = z_qkv.shape
    gw = HEADS_PER_STEP * HEAD_W
    nblk = BRANCH_WIDTH // gw
    return pl.pallas_call(
        functools.partial(_attn_d_prompt_kernel, tq=tq, t=t),
        grid=(b, nblk, t // tq),
        in_specs=[pl.BlockSpec((1, tq, gw), lambda i, h, j: (i, j, 3 * nblk + h)),
                  _resident((1, t, gw), lambda i, h, j: (i, 0, 4 * nblk + h)),
                  _resident((1, t, gw), lambda i, h, j: (i, 0, 5 * nblk + h)),
                  pl.BlockSpec((1, N_HEADS, tq), lambda i, h, j: (i, 0, j)),
                  pl.BlockSpec((1, N_HEADS, t), lambda i, h, j: (i, 0, 0))],
        out_specs=pl.BlockSpec((1, tq, gw), lambda i, h, j: (i, j, h)),
        out_shape=jax.ShapeDtypeStruct((b, t, N_HEADS * HEAD_W), BF16),
        scratch_shapes=[pltpu.VMEM((HEADS_PER_STEP, t, LANE), F32),
                        pltpu.VMEM((HEADS_PER_STEP, HEAD_W, tq), F32)],
        compiler_params=_params("parallel", "arbitrary", "arbitrary"),
        name="attn_d_prompt",
    )(z_qkv, z_qkv, z_qkv, cum, cum)


def _softmax_two(s_p, s_n, v_p, v_n):
    m = jnp.maximum(jnp.max(s_p, axis=-1, keepdims=True), jnp.max(s_n, axis=-1, keepdims=True))
    p_p = jnp.exp(s_p - m)
    p_n = jnp.exp(s_n - m)
    l = jnp.sum(p_p, axis=-1, keepdims=True) + jnp.sum(p_n, axis=-1, keepdims=True)
    acc = (jnp.dot(p_p.astype(BF16), v_p, preferred_element_type=F32)
           + jnp.dot(p_n.astype(BF16), v_n, preferred_element_type=F32))
    return acc / l


def _attn_a_sample_kernel(lq_ref, q_ref, kt_ref, vp_ref, kn_ref, vn_ref, bp_ref, bn_ref, g_ref, o_ref, *,
                          t, lam_init):
    lam = _lambda(lq_ref, lam_init)
    q_all, kn_all, vn_all = q_ref[0], kn_ref[0], vn_ref[0]
    for h in range(N_HEADS):
        q2 = _split_maps(_head(q_all, h))
        kn = _pad_rows(_head(kn_all, h), LANE).astype(BF16)
        vn = _pad_rows(_head(vn_all, h), LANE).astype(BF16)
        s_p = jnp.dot(q2, kt_ref[0, 0, h].astype(BF16), preferred_element_type=F32) + bp_ref[h, 0]
        s_n = lax.dot_general(q2, kn, NT, preferred_element_type=F32) + bn_ref[h, 0]
        o = _softmax_two(s_p, s_n, vp_ref[0, 0, :, h, :].astype(BF16), vn)
        o = o[:t] - lam * o[t:]
        o_ref[0, :, h * HEAD_W:(h + 1) * HEAD_W] = (_rms(o, g_ref[...]) * (1.0 - lam_init)).astype(o_ref.dtype)


def _attn_a_sample(z_qkv, k_cache_t, v_cache, l, bias_past, bias_new, lq, g_norm, *, lam_init):
    b, t, _ = z_qkv.shape
    past = v_cache.shape[2]
    bw = BRANCH_WIDTH
    return pl.pallas_call(
        functools.partial(_attn_a_sample_kernel, t=t, lam_init=lam_init),
        grid=(b,),
        in_specs=[pl.BlockSpec((4, HEAD_DIM_A), lambda i: (0, 0)),
                  pl.BlockSpec((1, t, bw), lambda i: (i, 0, 0)),
                  pl.BlockSpec((1, 1, N_HEADS, HEAD_W, past), lambda i: (l, i, 0, 0, 0)),
                  pl.BlockSpec((1, 1, past, N_HEADS, HEAD_W), lambda i: (l, i, 0, 0, 0)),
                  pl.BlockSpec((1, t, bw), lambda i: (i, 0, 1)),
                  pl.BlockSpec((1, t, bw), lambda i: (i, 0, 2)),
                  pl.BlockSpec((N_HEADS, 1, 2 * t, past), lambda i: (0, 0, 0, 0)),
                  pl.BlockSpec((N_HEADS, 1, 2 * t, LANE), lambda i: (0, 0, 0, 0)),
                  pl.BlockSpec((1, HEAD_W), lambda i: (0, 0))],
        out_specs=pl.BlockSpec((1, t, bw), lambda i: (i, 0, 0)),
        out_shape=jax.ShapeDtypeStruct((b, t, bw), BF16),
        compiler_params=_params("parallel"),
        name="attn_a_sample",
    )(lq, z_qkv, k_cache_t, v_cache, z_qkv, z_qkv, bias_past, bias_new, g_norm.reshape(1, HEAD_W))


def _attn_d_sample_kernel(q_ref, kp_ref, vp_ref, kn_ref, vn_ref, cq_ref, ckp_ref, ckn_ref, o_ref, *, t):
    scale = HEAD_W ** -0.5
    qry = lax.broadcasted_iota(jnp.int32, (t, LANE), 0)
    key = lax.broadcasted_iota(jnp.int32, (t, LANE), 1)
    q_all, kn_all, vn_all = q_ref[0], kn_ref[0], vn_ref[0]
    for h in range(N_HEADS):
        q = _head(q_all, h).astype(BF16)
        kn = _pad_rows(_head(kn_all, h), LANE).astype(BF16)
        vn = _pad_rows(_head(vn_all, h), LANE).astype(BF16)
        cq = cq_ref[0, h]
        s_p = lax.dot_general(q, kp_ref[0, 0, :, h, :].astype(BF16), NT, preferred_element_type=F32) * scale
        s_p = s_p + cq - ckp_ref[0, h:h + 1, :]
        s_n = lax.dot_general(q, kn, NT, preferred_element_type=F32) * scale
        s_n = s_n + cq - ckn_ref[0, h:h + 1, :]
        s_n = jnp.where(key <= qry, s_n, NEG_INF)
        o = _softmax_two(s_p, s_n, vp_ref[0, 0, :, h, :].astype(BF16), vn)
        o_ref[0, :, h * HEAD_W:(h + 1) * HEAD_W] = o.astype(o_ref.dtype)


def _attn_d_sample(z_qkv, k_cache, v_cache, l, cq, cum_past, cum_new):
    b, t, _ = z_qkv.shape
    past = k_cache.shape[2]
    bw = BRANCH_WIDTH
    cache_spec = pl.BlockSpec((1, 1, past, N_HEADS, HEAD_W), lambda i: (l, i, 0, 0, 0))
    return pl.pallas_call(
        functools.partial(_attn_d_sample_kernel, t=t),
        grid=(b,),
        in_specs=[pl.BlockSpec((1, t, bw), lambda i: (i, 0, 3)),
                  cache_spec, cache_spec,
                  pl.BlockSpec((1, t, bw), lambda i: (i, 0, 4)),
                  pl.BlockSpec((1, t, bw), lambda i: (i, 0, 5)),
                  pl.BlockSpec((1, N_HEADS, t, 1), lambda i: (i, 0, 0, 0)),
                  pl.BlockSpec((1, N_HEADS, past), lambda i: (i, 0, 0)),
                  pl.BlockSpec((1, N_HEADS, LANE), lambda i: (i, 0, 0))],
        out_specs=pl.BlockSpec((1, t, bw), lambda i: (i, 0, 0)),
        out_shape=jax.ShapeDtypeStruct((b, t, bw), BF16),
        compiler_params=_params("parallel"),
        name="attn_d_sample",
    )(z_qkv, k_cache, v_cache, z_qkv, z_qkv, cq, cum_past, cum_new)


def _logf_kernel(*refs, tt, has_past):
    refs = list(refs)
    h_ref, wf_ref, bf_ref = refs.pop(0), refs.pop(0), refs.pop(0)
    past_ref = refs.pop(0) if has_past else None
    logf_ref, cum_ref = refs.pop(0), refs.pop(0)
    cump_ref = refs.pop(0) if has_past else None
    carry_sc = refs.pop(0)

    @pl.when(pl.program_id(1) == 0)
    def _():
        if has_past:
            cump = _lane_cumsum(past_ref[0, 0])
            cump_ref[0] = cump
            carry_sc[...] = jnp.broadcast_to(cump[:, cump.shape[1] - 1:], carry_sc.shape)
        else:
            carry_sc[...] = jnp.zeros(carry_sc.shape, F32)

    ttp = logf_ref.shape[-1]
    hb = _pad_rows(h_ref[...], ttp)
    z = lax.dot_general(wf_ref[...].astype(BF16), hb, NT, preferred_element_type=F32) + bf_ref[...]
    logf = jnp.minimum(z, 0.0) - jnp.log1p(jnp.exp(-jnp.abs(z)))
    if tt < ttp:
        logf = jnp.where(lax.broadcasted_iota(jnp.int32, logf.shape, 1) < tt, logf, 0.0)
    logf_ref[0] = logf
    cum = _lane_cumsum(logf) + carry_sc[:, 0:1]
    cum_ref[0] = cum
    carry_sc[...] = jnp.broadcast_to(cum[:, ttp - 1:], carry_sc.shape)


def _logf(hf, b, w_in_t, l, b_forget, past_logf_t):
    m, d = hf.shape
    t = m // b
    tt = _tile(t, 1024, LANE) if t >= LANE else t
    ttp = max(tt, LANE)
    nt = t // tt
    has_past = past_logf_t is not None
    args = [hf, w_in_t, b_forget.reshape(N_HEADS, 1)]
    in_specs = [pl.BlockSpec((tt, d), lambda i, j: (i * nt + j, 0)),
                pl.BlockSpec((None, pl.Element(N_HEADS), pl.Element(d)), lambda i, j: (l, OFF_DF, 0)),
                pl.BlockSpec((N_HEADS, 1), lambda i, j: (0, 0))]
    out_shape = [jax.ShapeDtypeStruct((b, N_HEADS, nt * ttp), F32)] * 2
    out_specs = [pl.BlockSpec((1, N_HEADS, ttp), lambda i, j: (i, 0, j))] * 2
    if has_past:
        past = past_logf_t.shape[-1]
        args.append(past_logf_t)
        in_specs.append(pl.BlockSpec((1, 1, N_HEADS, past), lambda i, j: (l, i, 0, 0)))
        out_shape.append(jax.ShapeDtypeStruct((b, N_HEADS, past), F32))
        out_specs.append(pl.BlockSpec((1, N_HEADS, past), lambda i, j: (i, 0, 0)))
    outs = pl.pallas_call(
        functools.partial(_logf_kernel, tt=tt, has_past=has_past),
        grid=(b, nt), in_specs=in_specs, out_specs=out_specs, out_shape=out_shape,
        scratch_shapes=[pltpu.VMEM((N_HEADS, LANE), F32)],
        compiler_params=_params("parallel", "arbitrary"),
        name="logf_cumsum",
    )(*args)
    return outs[0], outs[1], (outs[2] if has_past else None)


def _lru_kernel(cx_ref, cg_ref, buf0_ref, h0_ref, cw_ref, cb_ref, wr_ref, br_ref, wi_ref, bi_ref,
                lam_ref, oc_ref, nbuf_ref, hl_ref, tail_sc, h_sc, *, tt):
    w = cx_ref.shape[-1]
    blk = w // LRU_BLOCKS
    pad = SUBLANE - (CONV_WIDTH - 1)

    @pl.when(pl.program_id(1) == 0)
    def _():
        tail_sc[0:pad, :] = jnp.zeros((pad, w), F32)
        tail_sc[pad:SUBLANE, :] = buf0_ref[0]
        h_sc[...] = h0_ref[0]

    full = jnp.concatenate([tail_sc[...], cx_ref[0]], axis=0)
    cw = cw_ref[...]
    xc = cb_ref[...]
    for j in range(CONV_WIDTH):
        xc = xc + full[pad + j:pad + j + tt] * cw[j:j + 1]
    tail_sc[...] = full[tt:tt + SUBLANE]
    nbuf_ref[0] = full[tt + pad:tt + SUBLANE]

    xb = xc.astype(BF16)

    def gate(w_ref, b_ref):
        parts = [jnp.dot(xb[:, g * blk:(g + 1) * blk], w_ref[g].astype(BF16), preferred_element_type=F32)
                 for g in range(LRU_BLOCKS)]
        return jax.nn.sigmoid(jnp.concatenate(parts, axis=1) + b_ref[...])

    r = gate(wr_ref, br_ref)
    i = gate(wi_ref, bi_ref)
    nl = -lam_ref[...]
    softplus = jnp.maximum(nl, 0.0) + jnp.log1p(jnp.exp(-jnp.abs(nl)))
    log_a = (-LRU_C * r) * softplus
    a = jnp.exp(log_a)
    u = jnp.sqrt(-jnp.tanh(log_a) * (a * a + 1.0)) * (i * xc)

    row = lax.broadcasted_iota(jnp.int32, (tt, w), 0)
    s = 1
    while s < tt:
        valid = row >= s
        u = jnp.where(valid, a * pltpu.roll(u, s, 0) + u, u)
        a = jnp.where(valid, a * pltpu.roll(a, s, 0), a)
        s *= 2
    h = u + a * h_sc[...]
    h_sc[...] = h[tt - 1:tt]
    hl_ref[0] = h[tt - 1:tt]
    oc_ref[0] = (h * _gelu(cg_ref[0])).astype(oc_ref.dtype)


def _lru(zcb, conv_buf, h0, conv_w, conv_b, w_r, b_r, w_i, b_i, lam):
    b, t, _ = zcb.shape
    w = BRANCH_WIDTH
    tt = _tile(t, 256, SUBLANE)
    vec = pl.BlockSpec((1, w), lambda i, j: (0, 0))
    mat = pl.BlockSpec((LRU_BLOCKS, w // LRU_BLOCKS, w // LRU_BLOCKS), lambda i, j: (0, 0, 0))
    return pl.pallas_call(
        functools.partial(_lru_kernel, tt=tt),
        grid=(b, t // tt),
        in_specs=[pl.BlockSpec((1, tt, w), lambda i, j: (i, j, REST_CX)),
                  pl.BlockSpec((1, tt, w), lambda i, j: (i, j, REST_CG)),
                  pl.BlockSpec((1, CONV_WIDTH - 1, w), lambda i, j: (i, 0, 0)),
                  pl.BlockSpec((1, 1, w), lambda i, j: (i, 0, 0)),
                  pl.BlockSpec((CONV_WIDTH, w), lambda i, j: (0, 0)),
                  vec, mat, vec, mat, vec, vec],
        out_specs=[pl.BlockSpec((1, tt, w), lambda i, j: (i, j, 0)),
                   pl.BlockSpec((1, CONV_WIDTH - 1, w), lambda i, j: (i, 0, 0)),
                   pl.BlockSpec((1, 1, w), lambda i, j: (i, 0, 0))],
        out_shape=[jax.ShapeDtypeStruct((b, t, w), BF16),
                   jax.ShapeDtypeStruct((b, CONV_WIDTH - 1, w), F32),
                   jax.ShapeDtypeStruct((b, 1, w), F32)],
        scratch_shapes=[pltpu.VMEM((SUBLANE, w), F32), pltpu.VMEM((1, w), F32)],
        compiler_params=_params("parallel", "arbitrary"),
        name="conv_lru",
    )(zcb, zcb, conv_buf, h0.reshape(b, 1, w), conv_w, conv_b.reshape(1, w), w_r, b_r.reshape(1, w),
      w_i, b_i.reshape(1, w), lam.reshape(1, w))


def _gmlp_kernel(zu_ref, zv_ref, lng_ref, lnb_ref, ws_ref, bst_ref, ob_ref, *vn_refs, ct):
    u = _gelu(zu_ref[0])
    v = _gelu(zv_ref[0])
    w = v.shape[-1]
    gw = w // GMLP_GROUPS
    d = v - jnp.mean(v, axis=-1, keepdims=True)
    vn = d * lax.rsqrt(jnp.mean(d * d, axis=-1, keepdims=True) + EPS) * lng_ref[...] + lnb_ref[...]
    if vn_refs:
        vn_refs[0][0] = vn
    vb = _pad_rows(vn, GMLP_CHUNK).astype(BF16)
    tril = (lax.broadcasted_iota(jnp.int32, (GMLP_CHUNK, GMLP_CHUNK), 0)
            >= lax.broadcasted_iota(jnp.int32, (GMLP_CHUNK, GMLP_CHUNK), 1))
    parts = []
    for g in range(GMLP_GROUPS):
        wg = jnp.where(tril, ws_ref[g], 0.0).astype(BF16)
        s = jnp.dot(wg, vb[:, g * gw:(g + 1) * gw], preferred_element_type=F32) + bst_ref[:, g:g + 1]
        parts.append(s[:ct])
    ob_ref[0] = (u * jnp.concatenate(parts, axis=1)).astype(ob_ref.dtype)


def _gmlp(zcb, ln_g, ln_b, ws, bs, want_rows):
    b, t, _ = zcb.shape
    w = BRANCH_WIDTH
    ct = min(t, GMLP_CHUNK)
    vec = pl.BlockSpec((1, w), lambda i, j: (0, 0))
    row_out = pl.BlockSpec((1, ct, w), lambda i, j: (i, j, 0))
    out_shape = [jax.ShapeDtypeStruct((b, t, w), BF16)]
    out_specs = [row_out]
    if want_rows:
        out_shape.append(jax.ShapeDtypeStruct((b, t, w), F32))
        out_specs.append(row_out)
    outs = pl.pallas_call(
        functools.partial(_gmlp_kernel, ct=ct),
        grid=(b, t // ct),
        in_specs=[pl.BlockSpec((1, ct, w), lambda i, j: (i, j, REST_BU)),
                  pl.BlockSpec((1, ct, w), lambda i, j: (i, j, REST_BV)),
                  vec, vec,
                  pl.BlockSpec((GMLP_GROUPS, GMLP_CHUNK, GMLP_CHUNK), lambda i, j: (0, 0, 0)),
                  pl.BlockSpec((GMLP_CHUNK, GMLP_GROUPS), lambda i, j: (0, 0))],
        out_specs=out_specs, out_shape=out_shape,
        compiler_params=_params("parallel", "parallel"),
        name="gmlp",
    )(zcb, zcb, ln_g.reshape(1, w), ln_b.reshape(1, w), ws, bs.T)
    return (outs[0], outs[1]) if want_rows else (outs[0], None)


def _mixers(h, l, p, caches, bias):
    b, t, d = h.shape
    m = b * t
    hf = h.reshape(m, d)
    bw = BRANCH_WIDTH
    w_in_t = p['w_in_t']
    lam_init = 0.8 - 0.6 * math.exp(-0.3 * l)

    qkv = _matmul_nt(hf, w_in_t, l, 0, OFF_DF, [F32, BF16] if caches is None else [F32])
    z_qkv = qkv[0].reshape(b, t, OFF_DF)
    zcb = _matmul_nt(hf, w_in_t, l, OFF_CX, 4 * bw, [F32])[0].reshape(b, t, 4 * bw)
    ka = z_qkv[:, :, 1 * bw:2 * bw]
    va = z_qkv[:, :, 2 * bw:3 * bw]
    kd = z_qkv[:, :, 4 * bw:5 * bw]
    vd = z_qkv[:, :, 5 * bw:6 * bw]

    if caches is None:
        logf_t, cum, _ = _logf(hf, b, w_in_t, l, p['b_forget'][l], None)
        tq = bias.shape[2]
        z_bf = qkv[1].reshape(b, t, OFF_DF)
        o_a = _attn_a_prompt(z_bf, bias, p['lambda_qk'][l], p['diff_norm_g'][l], tq=tq, lam_init=lam_init)
        o_d = _attn_d_prompt(z_bf, cum, tq=tq)
        conv_buf = jnp.zeros((b, CONV_WIDTH - 1, bw), F32)
        lru_h0 = jnp.zeros((b, bw), F32)
    else:
        a_k_t, a_v, d_k, d_v, d_logf_t, state_conv, state_lru = caches
        logf_t, cum_new, cum_past = _logf(hf, b, w_in_t, l, p['b_forget'][l], d_logf_t)
        logf_t = logf_t[:, :, :t]
        cq = cum_new[:, :, :t].reshape(b, N_HEADS, t, 1)
        o_a = _attn_a_sample(z_qkv, a_k_t, a_v, l, bias[0], bias[1], p['lambda_qk'][l], p['diff_norm_g'][l],
                             lam_init=lam_init)
        o_d = _attn_d_sample(z_qkv, d_k, d_v, l, cq, cum_past, cum_new)
        conv_buf, lru_h0 = state_conv[l], state_lru[l]

    o_c, new_buf, h_last = _lru(zcb, conv_buf, lru_h0, p['conv_w'][l], p['conv_b'][l], p['w_rgate'][l],
                                p['b_rgate'][l], p['w_igate'][l], p['b_igate'][l], p['lru_lambda'][l])
    o_b, v_rows = _gmlp(zcb, p['gmlp_ln_g'][l], p['gmlp_ln_b'][l], p['gmlp_ws'][l], p['gmlp_bs'][l],
                        want_rows=caches is not None)

    y = _merge(hf, [o.reshape(m, bw) for o in (o_a, o_b, o_c, o_d)], p['w_gate'], p['w_out'], l)
    states = (ka.reshape(b, t, N_HEADS, 2, HEAD_DIM_A), va.reshape(b, t, N_HEADS, HEAD_W),
              kd.reshape(b, t, N_HEADS, HEAD_W), vd.reshape(b, t, N_HEADS, HEAD_W),
              logf_t.transpose(0, 2, 1), new_buf, h_last.reshape(b, bw), v_rows)
    return y, states


def _ffn(h, l, j, p):
    b, t, d = h.shape
    idx = l * 2 + j
    mid = _ffn_up(h.reshape(b * t, d), p['w_ffn_gate'], p['w_ffn_up'], idx)
    return _matmul(mid, p['w_ffn_down'], idx, F32, tm_target=1024, tn_target=256).reshape(b, t, d)


def _run_group(x, mods, p, caches, bias):
    depth = len(mods)
    states = []
    _, h = _norm(x, pre=(mods[0], p['g_pre'][0], 0))
    for l in range(depth):
        mod, g_pre, g_post = mods[l], p['g_pre'][l], p['g_post'][l]
        x, h = _norm(x, _ffn(h, l, 0, p), post=(mod, g_post, 0, 0.5), pre=(mod, g_pre, 1))
        y, st = _mixers(h, l, p, caches, bias)
        states.append(st)
        x, h = _norm(x, y.reshape(x.shape), post=(mod, g_post, 1, 1.0), pre=(mod, g_pre, 2))
        nxt = (mods[l + 1], p['g_pre'][l + 1], 0) if l + 1 < depth else None
        x, h = _norm(x, _ffn(h, l, 1, p), post=(mod, g_post, 2, 0.5), pre=nxt)
    return x, states


def kernel(x_prompt, x_sample, c_prompt, c_sample, cache_a_k, cache_a_v, cache_d_k, cache_d_v, cache_d_logf, state_conv, state_lru, w_ada, b_ada, g_pre, g_post, w_ffn_gate, w_ffn_up, w_ffn_down, w_in, w_out, rel_bias, lambda_qk, diff_norm_g, gmlp_ln_g, gmlp_ln_b, gmlp_ws, gmlp_bs, conv_w, conv_b, w_rgate, b_rgate, w_igate, b_igate, lru_lambda, b_forget):
    depth, d_model, _ = w_in.shape
    bp, seq, _ = x_prompt.shape
    bs, dec_seq, _ = x_sample.shape
    past = cache_a_k.shape[2]
    d_ff = w_ffn_gate.shape[-1]
    p = {'g_pre': g_pre, 'g_post': g_post,
         'w_ffn_gate': w_ffn_gate.reshape(depth * 2, d_model, d_ff),
         'w_ffn_up': w_ffn_up.reshape(depth * 2, d_model, d_ff),
         'w_ffn_down': w_ffn_down.reshape(depth * 2, d_ff, d_model).astype(BF16),
         'w_in_t': jnp.swapaxes(w_in, 1, 2),
         'w_gate': jnp.swapaxes(w_in, 1, 2)[:, OFF_GATE:].astype(BF16),
         'w_out': w_out.reshape(depth * N_BRANCH, BRANCH_WIDTH, d_model),
         'lambda_qk': lambda_qk, 'diff_norm_g': diff_norm_g, 'gmlp_ln_g': gmlp_ln_g,
         'gmlp_ln_b': gmlp_ln_b, 'gmlp_ws': gmlp_ws, 'gmlp_bs': gmlp_bs, 'conv_w': conv_w,
         'conv_b': conv_b, 'w_rgate': w_rgate, 'b_rgate': b_rgate, 'w_igate': w_igate,
         'b_igate': b_igate, 'lru_lambda': lru_lambda, 'b_forget': b_forget}

    mod = _ada(jnp.concatenate([c_prompt, c_sample], axis=0), w_ada, b_ada)
    mod = mod.reshape(depth, bp + bs, 3 * N_SUB, d_model)
    mods_p = [mod[l, :bp] for l in range(depth)]
    mods_s = [mod[l, bp:] for l in range(depth)]

    tq = _tile(seq, 256, LANE)
    assert tq + 1 >= T5_SATURATION and tq % CHUNK == 0
    bias_p = _bias_tiles(rel_bias, tq, tq, [(0, 0), (tq, 0), None], True, seq)
    xp, st_p = _run_group(x_prompt, mods_p, p, None, bias_p)

    bias_s = (_bias_tiles(rel_bias, dec_seq, past, [(past, 0)], False, past + dec_seq),
              _bias_tiles(rel_bias, dec_seq, LANE, [(past, past)], False, past + dec_seq))
    caches = (cache_a_k.transpose(0, 1, 3, 4, 5, 2).reshape(depth, bs, N_HEADS, HEAD_W, past),
              cache_a_v, cache_d_k, cache_d_v,
              cache_d_logf.transpose(0, 1, 3, 2), state_conv, state_lru)
    xs, st_s = _run_group(x_sample, mods_s, p, caches, bias_s)

    np_ = [jnp.stack([st_p[l][j] for l in range(depth)]) for j in range(7)]
    ns_ = [jnp.stack([st_s[l][j] for l in range(depth)]) for j in range(8)]
    return (xp, xs, *np_, *ns_)
```

```python
import functools
import math

import jax
import jax.numpy as jnp
from jax import lax
from jax.experimental import pallas as pl
from jax.experimental.pallas import tpu as pltpu

F32 = jnp.float32
BF16 = jnp.bfloat16

LANE = 128
SUBLANE = 8
VMEM_LIMIT_BYTES = 56 * 1024 * 1024

CHUNK = 64
CHUNK_SHIFT = 6
N_HEADS = 8
HEAD_W = 128
HEAD_DIM_A = 64
N_BRANCH = 4
BRANCH_WIDTH = 1024
LRU_BLOCKS = 8
LRU_C = 8.0
CONV_WIDTH = 4
GMLP_GROUPS = 8
GMLP_CHUNK = 128
NUM_BUCKETS = 32
N_SUB = 3
EPS = 1e-6
NEG_INF = -1e30
LOG2E = 1.0 / math.log(2.0)
T5_LARGE_STARTS = (12, 16, 23, 32, 46, 64, 91)
T5_SATURATION = 91
T5_FAR_BUCKET = NUM_BUCKETS // 2 - 1

OFF_DF = 6 * BRANCH_WIDTH
OFF_CX = OFF_DF + N_HEADS
OFF_GATE = OFF_CX + 4 * BRANCH_WIDTH
REST_CX, REST_CG, REST_BU, REST_BV = 0, 1, 2, 3

NT = (((1,), (1,)), ((), ()))
TN = (((0,), (0,)), ((), ()))


def _params(*sem):
    return pltpu.CompilerParams(dimension_semantics=sem, vmem_limit_bytes=VMEM_LIMIT_BYTES)


def _tile(n, target, mult):
    best = None
    for t in range(mult, min(n, target) + 1, mult):
        if n % t == 0:
            best = t
    return n if best is None else best


def _resident(shape, index_map):
    return pl.BlockSpec(shape, index_map, pipeline_mode=pl.Buffered(1))


def _gelu(x):
    return x * (0.5 * (1.0 + jnp.tanh(math.sqrt(2.0 / math.pi) * (x + 0.044715 * (x * x * x)))))


def _rms(x, g):
    return x * lax.rsqrt(jnp.mean(x * x, axis=-1, keepdims=True) + EPS) * g


def _pad_rows(x, rows):
    if x.shape[0] == rows:
        return x
    return jnp.concatenate([x, jnp.zeros((rows - x.shape[0],) + x.shape[1:], x.dtype)], axis=0)


def _lane_cumsum(x):
    n = x.shape[-1]
    lane = lax.broadcasted_iota(jnp.int32, x.shape, x.ndim - 1)
    s = 1
    while s < n:
        x = x + jnp.where(lane >= s, pltpu.roll(x, s, x.ndim - 1), 0.0)
        s *= 2
    return x


def _ada_kernel(c_ref, w_ref, b_ref, o_ref):
    c = c_ref[...]
    a = (c * jax.nn.sigmoid(c)).astype(BF16)
    o_ref[0] = jnp.dot(a, w_ref[0].astype(BF16), preferred_element_type=F32) + b_ref[0]


def _ada(c_all, w_ada, b_ada):
    depth, d, n = w_ada.shape
    bt = c_all.shape[0]
    tn = _tile(n, 512, LANE)
    return pl.pallas_call(
        _ada_kernel,
        grid=(depth, n // tn),
        in_specs=[pl.BlockSpec((bt, d), lambda l, j: (0, 0)),
                  pl.BlockSpec((1, d, tn), lambda l, j: (l, 0, j)),
                  pl.BlockSpec((1, 1, tn), lambda l, j: (l, 0, j))],
        out_specs=pl.BlockSpec((1, bt, tn), lambda l, j: (l, 0, j)),
        out_shape=jax.ShapeDtypeStruct((depth, bt, n), F32),
        compiler_params=_params("parallel", "parallel"),
        name="ada",
    )(c_all, w_ada, b_ada.reshape(depth, 1, n))


def _norm_kernel(*refs, k_post, coef, k_pre):
    refs = list(refs)
    x_ref = refs.pop(0)
    x = x_ref[0]
    if k_post is not None:
        y_ref, mpost_ref, gpost_ref = refs.pop(0), refs.pop(0), refs.pop(0)
    if k_pre is not None:
        mpre_ref, gpre_ref = refs.pop(0), refs.pop(0)
    if k_post is not None:
        xo_ref = refs.pop(0)
        m = mpost_ref[0]
        gate = m[3 * k_post + 2:3 * k_post + 3]
        x = x + (coef * gate) * _rms(y_ref[0], gpost_ref[k_post:k_post + 1, :])
        xo_ref[0] = x
    if k_pre is not None:
        ho_ref = refs.pop(0)
        m = mpre_ref[0]
        shift = m[3 * k_pre:3 * k_pre + 1]
        scale = m[3 * k_pre + 1:3 * k_pre + 2]
        ho_ref[0] = (_rms(x, gpre_ref[k_pre:k_pre + 1, :]) * (1.0 + scale) + shift).astype(BF16)


def _norm(x, y=None, post=None, pre=None):
    b, t, d = x.shape
    tt = _tile(t, 256, SUBLANE)
    row = pl.BlockSpec((1, tt, d), lambda i, j: (i, j, 0))
    mod_spec = pl.BlockSpec((1, 3 * N_SUB, d), lambda i, j: (i, 0, 0))
    g_spec = pl.BlockSpec((N_SUB, d), lambda i, j: (0, 0))
    args, in_specs, out_shape, out_specs = [x], [row], [], []
    if post is not None:
        args += [y, post[0], post[1]]
        in_specs += [row, mod_spec, g_spec]
        out_shape.append(jax.ShapeDtypeStruct((b, t, d), F32))
        out_specs.append(row)
    if pre is not None:
        args += [pre[0], pre[1]]
        in_specs += [mod_spec, g_spec]
        out_shape.append(jax.ShapeDtypeStruct((b, t, d), BF16))
        out_specs.append(row)
    outs = pl.pallas_call(
        functools.partial(_norm_kernel, k_post=None if post is None else post[2],
                          coef=None if post is None else post[3],
                          k_pre=None if pre is None else pre[2]),
        grid=(b, t // tt), in_specs=in_specs, out_specs=out_specs, out_shape=out_shape,
        compiler_params=_params("parallel", "parallel"),
        name="norm",
    )(*args)
    outs = list(outs)
    x_new = outs.pop(0) if post is not None else None
    h = outs.pop(0) if pre is not None else None
    return x_new, h


def _mm_kernel(x_ref, w_ref, *refs):
    if len(refs) == 3:
        refs[2][...] = refs[0][...].astype(BF16)
    o_ref = refs[-2] if len(refs) == 3 else refs[0]
    o_ref[...] = jnp.dot(x_ref[...], w_ref[0].astype(BF16),
                         preferred_element_type=F32).astype(o_ref.dtype)


def _matmul(x, w, l, out_dtype, tm_target, tn_target, cast_rows=None):
    m, k = x.shape
    n = w.shape[-1]
    tm = _tile(m, tm_target, 16)
    tn = _tile(n, tn_target, LANE)
    nj = n // tn
    args = [x, w]
    in_specs = [_resident((tm, k), lambda i, j: (i, 0)),
                pl.BlockSpec((1, k, tn), lambda i, j: (l, 0, j))]
    out_specs = [pl.BlockSpec((tm, tn), lambda i, j: (i, j))]
    out_shape = [jax.ShapeDtypeStruct((m, n), out_dtype)]
    if cast_rows is not None:
        wt, layer, row0, nrows = cast_rows
        slab = nrows // ((m // tm) * nj)
        assert slab * (m // tm) * nj == nrows and slab % 16 == 0
        d = wt.shape[-1]
        args.append(wt)
        in_specs.append(pl.BlockSpec(
            (None, pl.Element(slab), pl.Element(d)),
            lambda i, j: (layer, pl.multiple_of(row0 + (i * nj + j) * slab, SUBLANE), 0)))
        out_specs.append(pl.BlockSpec((slab, d), lambda i, j: (i * nj + j, 0)))
        out_shape.append(jax.ShapeDtypeStruct((nrows, d), BF16))
    outs = pl.pallas_call(
        _mm_kernel,
        grid=(m // tm, nj),
        in_specs=in_specs, out_specs=out_specs, out_shape=out_shape,
        compiler_params=_params("parallel", "arbitrary"),
        name="matmul",
    )(*args)
    return outs if cast_rows is not None else outs[0]


def _mm_nt_kernel(x_ref, w_ref, *o_refs):
    z = lax.dot_general(x_ref[...], w_ref[...].astype(BF16), NT, preferred_element_type=F32)
    for o_ref in o_refs:
        o_ref[...] = z.astype(o_ref.dtype)


def _matmul_nt(x, wt, l, row0, nrows, out_dtypes, tm_target=2048, tn_target=512):
    m, k = x.shape
    tm = _tile(m, tm_target, 16)
    tn = _tile(nrows, tn_target, LANE)
    return pl.pallas_call(
        _mm_nt_kernel,
        grid=(m // tm, nrows // tn),
        in_specs=[_resident((tm, k), lambda i, j: (i, 0)),
                  pl.BlockSpec((None, pl.Element(tn), pl.Element(k)),
                               lambda i, j: (l, pl.multiple_of(row0 + j * tn, SUBLANE), 0))],
        out_specs=[pl.BlockSpec((tm, tn), lambda i, j: (i, j)) for _ in out_dtypes],
        out_shape=[jax.ShapeDtypeStruct((m, nrows), dt) for dt in out_dtypes],
        compiler_params=_params("parallel", "arbitrary"),
        name="matmul_nt",
    )(x, wt)


def _ffn_up_kernel(x_ref, wg_ref, wu_ref, *refs):
    if len(refs) == 3:
        refs[2][...] = refs[0][0].astype(BF16)
    o_ref = refs[-2] if len(refs) == 3 else refs[0]
    x = x_ref[...]
    g = jnp.dot(x, wg_ref[0].astype(BF16), preferred_element_type=F32)
    u = jnp.dot(x, wu_ref[0].astype(BF16), preferred_element_type=F32)
    o_ref[...] = ((g * jax.nn.sigmoid(g)) * u).astype(o_ref.dtype)


def _ffn_up(x, wg, wu, l, w_down=None):
    m, k = x.shape
    n = wg.shape[-1]
    tm = _tile(m, 2048, 16)
    tn = _tile(n, 256, LANE)
    nj = n // tn
    w_spec = pl.BlockSpec((1, k, tn), lambda i, j: (l, 0, j))
    args = [x, wg, wu]
    in_specs = [_resident((tm, k), lambda i, j: (i, 0)), w_spec, w_spec]
    out_specs = [pl.BlockSpec((tm, tn), lambda i, j: (i, j))]
    out_shape = [jax.ShapeDtypeStruct((m, n), BF16)]
    if w_down is not None:
        d = w_down.shape[-1]
        slab = n // ((m // tm) * nj)
        assert slab * (m // tm) * nj == n and slab % 16 == 0
        args.append(w_down)
        in_specs.append(pl.BlockSpec((1, slab, d), lambda i, j: (l, i * nj + j, 0)))
        out_specs.append(pl.BlockSpec((slab, d), lambda i, j: (i * nj + j, 0)))
        out_shape.append(jax.ShapeDtypeStruct((n, d), BF16))
    outs = pl.pallas_call(
        _ffn_up_kernel,
        grid=(m // tm, nj),
        in_specs=in_specs, out_specs=out_specs, out_shape=out_shape,
        compiler_params=_params("parallel", "arbitrary"),
        name="ffn_up",
    )(*args)
    return outs if w_down is not None else outs[0]


def _merge_kernel(h_ref, oa_ref, ob_ref, oc_ref, od_ref, wg0, wg1, wg2, wg3, wo0, wo1, wo2, wo3, y_ref):
    h = h_ref[...]
    y = None
    for o_ref, wg, wo in ((oa_ref, wg0, wo0), (ob_ref, wg1, wo1), (oc_ref, wg2, wo2), (od_ref, wg3, wo3)):
        gate = jax.nn.sigmoid(lax.dot_general(h, wg[0], NT, preferred_element_type=F32))
        proj = jnp.dot(o_ref[...], wo[0].astype(BF16), preferred_element_type=F32)
        y = gate * proj if y is None else y + gate * proj
    y_ref[...] = y


def _merge(h, branches, w_gate, w_out, l):
    m, d = h.shape
    n = w_out.shape[-1]
    bw = branches[0].shape[-1]
    tm = _tile(m, 1024, 16)
    tn = _tile(n, 256, LANE)
    nj = n // tn

    def gate_spec(i):
        return pl.BlockSpec((1, tn, d), lambda a, j: (0, i * nj + j, 0))

    def out_spec(i):
        return pl.BlockSpec((1, bw, tn), lambda a, j: (l * N_BRANCH + i, 0, j))

    return pl.pallas_call(
        _merge_kernel,
        grid=(m // tm, nj),
        in_specs=[_resident((tm, d), lambda a, j: (a, 0))] + [_resident((tm, bw), lambda a, j: (a, 0))] * N_BRANCH
        + [gate_spec(i) for i in range(N_BRANCH)] + [out_spec(i) for i in range(N_BRANCH)],
        out_specs=pl.BlockSpec((tm, tn), lambda a, j: (a, j)),
        out_shape=jax.ShapeDtypeStruct((m, n), F32),
        compiler_params=_params("parallel", "arbitrary"),
        name="merge",
    )(h, *branches, w_gate, w_gate, w_gate, w_gate, w_out, w_out, w_out, w_out)


def _bias_kernel(tab_ref, o_ref, *, nq, nk, tiles, transposed, kv_end):
    h = pl.program_id(0)
    shape = (nk, nq) if transposed else (nq, nk)
    q_i = lax.broadcasted_iota(jnp.int32, shape, 1 if transposed else 0)
    k_i = lax.broadcasted_iota(jnp.int32, shape, 0 if transposed else 1)
    for c, tile in enumerate(tiles):
        for mp in range(2):
            if tile is None:
                val = jnp.full(shape, NEG_INF, F32)
            else:
                q0, k0 = tile
                q_pos, k_pos = q0 + q_i, k0 + k_i
                rel = k_pos - q_pos
                n = jnp.abs(rel)
                large = NUM_BUCKETS // 4
                for start in T5_LARGE_STARTS:
                    large = large + jnp.where(n >= start, 1, 0)
                bucket = jnp.where(rel > 0, NUM_BUCKETS // 2, 0) + jnp.where(n < NUM_BUCKETS // 4, n, large)
                val = jnp.zeros(shape, F32)
                for b in range(NUM_BUCKETS):
                    val = jnp.where(bucket == b, tab_ref[b * (2 * N_HEADS) + h * 2 + mp], val)
                val = val - tab_ref[T5_FAR_BUCKET * (2 * N_HEADS) + h * 2 + mp]
                visible = jnp.logical_and((k_pos >> CHUNK_SHIFT) <= (q_pos >> CHUNK_SHIFT), k_pos < kv_end)
                val = jnp.where(visible, val, NEG_INF)
            if transposed:
                o_ref[0, c, :, mp * nq:(mp + 1) * nq] = val
            else:
                o_ref[0, c, mp * nq:(mp + 1) * nq, :] = val


def _bias_tiles(rel_bias, nq, nk, tiles, transposed, kv_end):
    shape = (nk, 2 * nq) if transposed else (2 * nq, nk)
    return pl.pallas_call(
        functools.partial(_bias_kernel, nq=nq, nk=nk, tiles=tuple(tiles), transposed=transposed, kv_end=kv_end),
        grid=(N_HEADS,),
        in_specs=[pl.BlockSpec(memory_space=pltpu.SMEM)],
        out_specs=pl.BlockSpec((1, len(tiles)) + shape, lambda h: (h, 0, 0, 0)),
        out_shape=jax.ShapeDtypeStruct((N_HEADS, len(tiles)) + shape, F32),
        compiler_params=_params("parallel"),
        name="bias_tiles",
    )(rel_bias.reshape(-1))


def _lambda(lq_ref, lam_init):
    lq = lq_ref[...]
    return (jnp.exp(jnp.sum(lq[0:1] * lq[1:2], axis=-1, keepdims=True))
            - jnp.exp(jnp.sum(lq[2:3] * lq[3:4], axis=-1, keepdims=True)) + lam_init)


def _split_maps(q):
    lane = lax.broadcasted_iota(jnp.int32, q.shape, 1)
    q = q * (HEAD_DIM_A ** -0.5)
    return jnp.concatenate([jnp.where(lane < HEAD_DIM_A, q, 0.0),
                            jnp.where(lane >= HEAD_DIM_A, q, 0.0)], axis=0).astype(BF16)


def _flash_tile_t(carries, acc_sc, scores, values, query_terms=None):
    out, probs, alphas = [], [], []
    for g, ((m, l), s) in enumerate(zip(carries, scores)):
        if query_terms is None:
            m_new = jnp.maximum(m, jnp.max(s, axis=0, keepdims=True))
            p = jnp.exp(s - m_new)
            alpha = jnp.exp(m - m_new)
        else:
            m_new = jnp.maximum(m, jnp.max(s, axis=0, keepdims=True) + query_terms[g])
            p = jnp.exp2(s - (m_new - query_terms[g]))
            alpha = jnp.exp2(m - m_new)
        out.append((m_new, alpha * l + jnp.sum(p, axis=0, keepdims=True)))
        probs.append(p.astype(BF16))
        alphas.append(alpha)
    for g, (v, p, alpha) in enumerate(zip(values, probs, alphas)):
        acc_sc[g] = alpha * acc_sc[g] + lax.dot_general(v, p, TN, preferred_element_type=F32)
    return tuple(out)


def _flash_init(acc_ref):
    acc_ref[...] = jnp.zeros(acc_ref.shape, F32)
    nq = acc_ref.shape[-1]
    return jnp.full((1, nq), NEG_INF, F32), jnp.zeros((1, nq), F32)


HEADS_PER_STEP = 8


def _head(x, g):
    return x[:, g * HEAD_W:(g + 1) * HEAD_W]


def _attn_a_prompt_kernel(lq_ref, q_ref, k_ref, v_ref, bias_ref, g_ref, o_ref, acc_sc, *, tq, lam_init):
    qi = pl.program_id(2)
    heads = range(HEADS_PER_STEP)
    q_all = q_ref[0]
    q2 = [_split_maps(_head(q_all, g)) for g in heads]

    def tile(kt, carries, bias_tile):
        k0 = pl.multiple_of(kt * tq, tq)
        k_all = k_ref[0, pl.ds(k0, tq), :]
        v_all = v_ref[0, pl.ds(k0, tq), :]
        scores = [lax.dot_general(_head(k_all, g).astype(BF16), q2[g], NT, preferred_element_type=F32)
                  for g in heads]
        if bias_tile is not None:
            scores = [s + bias_ref[g, bias_tile] for g, s in enumerate(scores)]
        return _flash_tile_t(carries, acc_sc, scores, [_head(v_all, g).astype(BF16) for g in heads])

    carries = tile(qi, tuple(_flash_init(acc_sc.at[g]) for g in heads), 0)
    carries = tile(jnp.maximum(qi - 1, 0), carries, jnp.where(qi == 0, 2, 1))
    carries = lax.fori_loop(0, qi - 1, lambda kt, c: tile(kt, c, None), carries)
    lam = _lambda(lq_ref, lam_init)
    for g in heads:
        o = acc_sc[g] * (1.0 / carries[g][1])
        o = (o[:, :tq] - lam * o[:, tq:]).T
        o_ref[0, :, g * HEAD_W:(g + 1) * HEAD_W] = (_rms(o, g_ref[...]) * (1.0 - lam_init)).astype(o_ref.dtype)


def _attn_a_prompt(z_qkv, bias, lq, g_norm, *, tq, lam_init):
    b, t, _ = z_qkv.shape
    gw = HEADS_PER_STEP * HEAD_W
    nblk = BRANCH_WIDTH // gw
    return pl.pallas_call(
        functools.partial(_attn_a_prompt_kernel, tq=tq, lam_init=lam_init),
        grid=(b, nblk, t // tq),
        in_specs=[pl.BlockSpec((4, HEAD_DIM_A), lambda i, h, j: (0, 0)),
                  pl.BlockSpec((1, tq, gw), lambda i, h, j: (i, j, h)),
                  _resident((1, t, gw), lambda i, h, j: (i, 0, nblk + h)),
                  _resident((1, t, gw), lambda i, h, j: (i, 0, 2 * nblk + h)),
                  _resident((HEADS_PER_STEP, 3, tq, 2 * tq), lambda i, h, j: (h, 0, 0, 0)),
                  pl.BlockSpec((1, HEAD_W), lambda i, h, j: (0, 0))],
        out_specs=pl.BlockSpec((1, tq, gw), lambda i, h, j: (i, j, h)),
        out_shape=jax.ShapeDtypeStruct((b, t, N_HEADS * HEAD_W), BF16),
        scratch_shapes=[pltpu.VMEM((HEADS_PER_STEP, HEAD_W, 2 * tq), F32)],
        compiler_params=_params("parallel", "parallel", "arbitrary"),
        name="attn_a_prompt",
    )(lq, z_qkv, z_qkv, z_qkv, bias, g_norm.reshape(1, HEAD_W))


def _attn_d_prompt_kernel(q_ref, k_ref, v_ref, cq_ref, ck_ref, o_ref, ckb_sc, acc_sc, *, tq, t):
    h0 = pl.program_id(1) * HEADS_PER_STEP
    qi = pl.program_id(2)
    heads = range(HEADS_PER_STEP)

    @pl.when(qi == 0)
    def _():
        for g in heads:
            ck = ck_ref[0, pl.ds(h0 + g, 1), :] * LOG2E
            for c in range(t // LANE):
                ckb_sc[g, c * LANE:(c + 1) * LANE, :] = jnp.broadcast_to(
                    ck[:, c * LANE:(c + 1) * LANE], (LANE, LANE)).T

    q_all = q_ref[0]
    q = [_head(q_all, g).astype(BF16) for g in heads]
    cq = [cq_ref[0, pl.ds(h0 + g, 1), :] * LOG2E for g in heads]
    scale = HEAD_W ** -0.5 * LOG2E

    def tile(kt, carries, diagonal):
        k0 = pl.multiple_of(kt * tq, tq)
        k_all = k_ref[0, pl.ds(k0, tq), :]
        v_all = v_ref[0, pl.ds(k0, tq), :]
        scores = [lax.dot_general(_head(k_all, g).astype(BF16), q[g], NT, preferred_element_type=F32)
                  for g in heads]
        for g in heads:
            ckb = ckb_sc[g, pl.ds(k0, tq), :]
            s = scores[g] * scale - jnp.concatenate([ckb] * (tq // LANE), axis=1)
            if diagonal:
                key = lax.broadcasted_iota(jnp.int32, (tq, tq), 0)
                qry = lax.broadcasted_iota(jnp.int32, (tq, tq), 1)
                s = jnp.where(key <= qry, s, NEG_INF)
            scores[g] = s
        return _flash_tile_t(carries, acc_sc, scores, [_head(v_all, g).astype(BF16) for g in heads], cq)

    carries = tile(qi, tuple(_flash_init(acc_sc.at[g]) for g in heads), True)
    carries = lax.fori_loop(0, qi, lambda kt, c: tile(kt, c, False), carries)
    for g in heads:
        o = acc_sc[g] * (1.0 / carries[g][1])
        o_ref[0, :, g * HEAD_W:(g + 1) * HEAD_W] = o.T.astype(o_ref.dtype)


def _attn_d_prompt(z_qkv, cum, *, tq):
    b, t, _ = z_qkv.shape
    gw = HEADS_PER_STEP * HEAD_W
    nblk = BRANCH_WIDTH // gw
    return pl.pallas_call(
        functools.partial(_attn_d_prompt_kernel, tq=tq, t=t),
        grid=(b, nblk, t // tq),
        in_specs=[pl.BlockSpec((1, tq, gw), lambda i, h, j: (i, j, 3 * nblk + h)),
                  _resident((1, t, gw), lambda i, h, j: (i, 0, 4 * nblk + h)),
                  _resident((1, t, gw), lambda i, h, j: (i, 0, 5 * nblk + h)),
                  pl.BlockSpec((1, N_HEADS, tq), lambda i, h, j: (i, 0, j)),
                  pl.BlockSpec((1, N_HEADS, t), lambda i, h, j: (i, 0, 0))],
        out_specs=pl.BlockSpec((1, tq, gw), lambda i, h, j: (i, j, h)),
        out_shape=jax.ShapeDtypeStruct((b, t, N_HEADS * HEAD_W), BF16),
        scratch_shapes=[pltpu.VMEM((HEADS_PER_STEP, t, LANE), F32),
                        pltpu.VMEM((HEADS_PER_STEP, HEAD_W, tq), F32)],
        compiler_params=_params("parallel", "arbitrary", "arbitrary"),
        name="attn_d_prompt",
    )(z_qkv, z_qkv, z_qkv, cum, cum)


def _softmax_two(s_p, s_n, v_p, v_n):
    m = jnp.maximum(jnp.max(s_p, axis=-1, keepdims=True), jnp.max(s_n, axis=-1, keepdims=True))
    p_p = jnp.exp(s_p - m)
    p_n = jnp.exp(s_n - m)
    l = jnp.sum(p_p, axis=-1, keepdims=True) + jnp.sum(p_n, axis=-1, keepdims=True)
    acc = (jnp.dot(p_p.astype(BF16), v_p, preferred_element_type=F32)
           + jnp.dot(p_n.astype(BF16), v_n, preferred_element_type=F32))
    return acc / l


def _attn_a_sample_kernel(lq_ref, q_ref, kt_ref, vp_ref, kn_ref, vn_ref, bp_ref, bn_ref, g_ref, o_ref, *,
                          t, lam_init):
    lam = _lambda(lq_ref, lam_init)
    q_all, kn_all, vn_all = q_ref[0], kn_ref[0], vn_ref[0]
    for h in range(N_HEADS):
        q2 = _split_maps(_head(q_all, h))
        kn = _pad_rows(_head(kn_all, h), LANE).astype(BF16)
        vn = _pad_rows(_head(vn_all, h), LANE).astype(BF16)
        s_p = jnp.dot(q2, kt_ref[0, 0, h].astype(BF16), preferred_element_type=F32) + bp_ref[h, 0]
        s_n = lax.dot_general(q2, kn, NT, preferred_element_type=F32) + bn_ref[h, 0]
        o = _softmax_two(s_p, s_n, vp_ref[0, 0, :, h, :].astype(BF16), vn)
        o = o[:t] - lam * o[t:]
        o_ref[0, :, h * HEAD_W:(h + 1) * HEAD_W] = (_rms(o, g_ref[...]) * (1.0 - lam_init)).astype(o_ref.dtype)


def _attn_a_sample(z_qkv, k_cache_t, v_cache, l, bias_past, bias_new, lq, g_norm, *, lam_init):
    b, t, _ = z_qkv.shape
    past = v_cache.shape[2]
    bw = BRANCH_WIDTH
    return pl.pallas_call(
        functools.partial(_attn_a_sample_kernel, t=t, lam_init=lam_init),
        grid=(b,),
        in_specs=[pl.BlockSpec((4, HEAD_DIM_A), lambda i: (0, 0)),
                  pl.BlockSpec((1, t, bw), lambda i: (i, 0, 0)),
                  pl.BlockSpec((1, 1, N_HEADS, HEAD_W, past), lambda i: (l, i, 0, 0, 0)),
                  pl.BlockSpec((1, 1, past, N_HEADS, HEAD_W), lambda i: (l, i, 0, 0, 0)),
                  pl.BlockSpec((1, t, bw), lambda i: (i, 0, 1)),
                  pl.BlockSpec((1, t, bw), lambda i: (i, 0, 2)),
                  pl.BlockSpec((N_HEADS, 1, 2 * t, past), lambda i: (0, 0, 0, 0)),
                  pl.BlockSpec((N_HEADS, 1, 2 * t, LANE), lambda i: (0, 0, 0, 0)),
                  pl.BlockSpec((1, HEAD_W), lambda i: (0, 0))],
        out_specs=pl.BlockSpec((1, t, bw), lambda i: (i, 0, 0)),
        out_shape=jax.ShapeDtypeStruct((b, t, bw), BF16),
        compiler_params=_params("parallel"),
        name="attn_a_sample",
    )(lq, z_qkv, k_cache_t, v_cache, z_qkv, z_qkv, bias_past, bias_new, g_norm.reshape(1, HEAD_W))


def _attn_d_sample_kernel(q_ref, kp_ref, vp_ref, kn_ref, vn_ref, cq_ref, ckp_ref, ckn_ref, o_ref, *, t):
    scale = HEAD_W ** -0.5
    qry = lax.broadcasted_iota(jnp.int32, (t, LANE), 0)
    key = lax.broadcasted_iota(jnp.int32, (t, LANE), 1)
    q_all, kn_all, vn_all = q_ref[0], kn_ref[0], vn_ref[0]
    for h in range(N_HEADS):
        q = _head(q_all, h).astype(BF16)
        kn = _pad_rows(_head(kn_all, h), LANE).astype(BF16)
        vn = _pad_rows(_head(vn_all, h), LANE).astype(BF16)
        cq = cq_ref[0, h]
        s_p = lax.dot_general(q, kp_ref[0, 0, :, h, :].astype(BF16), NT, preferred_element_type=F32) * scale
        s_p = s_p + cq - ckp_ref[0, h:h + 1, :]
        s_n = lax.dot_general(q, kn, NT, preferred_element_type=F32) * scale
        s_n = s_n + cq - ckn_ref[0, h:h + 1, :]
        s_n = jnp.where(key <= qry, s_n, NEG_INF)
        o = _softmax_two(s_p, s_n, vp_ref[0, 0, :, h, :].astype(BF16), vn)
        o_ref[0, :, h * HEAD_W:(h + 1) * HEAD_W] = o.astype(o_ref.dtype)


def _attn_d_sample(z_qkv, k_cache, v_cache, l, cq, cum_past, cum_new):
    b, t, _ = z_qkv.shape
    past = k_cache.shape[2]
    bw = BRANCH_WIDTH
    cache_spec = pl.BlockSpec((1, 1, past, N_HEADS, HEAD_W), lambda i: (l, i, 0, 0, 0))
    return pl.pallas_call(
        functools.partial(_attn_d_sample_kernel, t=t),
        grid=(b,),
        in_specs=[pl.BlockSpec((1, t, bw), lambda i: (i, 0, 3)),
                  cache_spec, cache_spec,
                  pl.BlockSpec((1, t, bw), lambda i: (i, 0, 4)),
                  pl.BlockSpec((1, t, bw), lambda i: (i, 0, 5)),
                  pl.BlockSpec((1, N_HEADS, t, 1), lambda i: (i, 0, 0, 0)),
                  pl.BlockSpec((1, N_HEADS, past), lambda i: (i, 0, 0)),
                  pl.BlockSpec((1, N_HEADS, LANE), lambda i: (i, 0, 0))],
        out_specs=pl.BlockSpec((1, t, bw), lambda i: (i, 0, 0)),
        out_shape=jax.ShapeDtypeStruct((b, t, bw), BF16),
        compiler_params=_params("parallel"),
        name="attn_d_sample",
    )(z_qkv, k_cache, v_cache, z_qkv, z_qkv, cq, cum_past, cum_new)


def _logf_kernel(*refs, tt, has_past):
    refs = list(refs)
    h_ref, wf_ref, bf_ref = refs.pop(0), refs.pop(0), refs.pop(0)
    past_ref = refs.pop(0) if has_past else None
    logf_ref, cum_ref = refs.pop(0), refs.pop(0)
    cump_ref = refs.pop(0) if has_past else None
    carry_sc = refs.pop(0)

    @pl.when(pl.program_id(1) == 0)
    def _():
        if has_past:
            cump = _lane_cumsum(past_ref[0, 0])
            cump_ref[0] = cump
            carry_sc[...] = jnp.broadcast_to(cump[:, cump.shape[1] - 1:], carry_sc.shape)
        else:
            carry_sc[...] = jnp.zeros(carry_sc.shape, F32)

    ttp = logf_ref.shape[-1]
    hb = _pad_rows(h_ref[...], ttp)
    z = lax.dot_general(wf_ref[...].astype(BF16), hb, NT, preferred_element_type=F32) + bf_ref[...]
    logf = jnp.minimum(z, 0.0) - jnp.log1p(jnp.exp(-jnp.abs(z)))
    if tt < ttp:
        logf = jnp.where(lax.broadcasted_iota(jnp.int32, logf.shape, 1) < tt, logf, 0.0)
    logf_ref[0] = logf
    cum = _lane_cumsum(logf) + carry_sc[:, 0:1]
    cum_ref[0] = cum
    carry_sc[...] = jnp.broadcast_to(cum[:, ttp - 1:], carry_sc.shape)


def _logf(hf, b, w_in_t, l, b_forget, past_logf_t):
    m, d = hf.shape
    t = m // b
    tt = _tile(t, 1024, LANE) if t >= LANE else t
    ttp = max(tt, LANE)
    nt = t // tt
    has_past = past_logf_t is not None
    args = [hf, w_in_t, b_forget.reshape(N_HEADS, 1)]
    in_specs = [pl.BlockSpec((tt, d), lambda i, j: (i * nt + j, 0)),
                pl.BlockSpec((None, pl.Element(N_HEADS), pl.Element(d)), lambda i, j: (l, OFF_DF, 0)),
                pl.BlockSpec((N_HEADS, 1), lambda i, j: (0, 0))]
    out_shape = [jax.ShapeDtypeStruct((b, N_HEADS, nt * ttp), F32)] * 2
    out_specs = [pl.BlockSpec((1, N_HEADS, ttp), lambda i, j: (i, 0, j))] * 2
    if has_past:
        past = past_logf_t.shape[-1]
        args.append(past_logf_t)
        in_specs.append(pl.BlockSpec((1, 1, N_HEADS, past), lambda i, j: (l, i, 0, 0)))
        out_shape.append(jax.ShapeDtypeStruct((b, N_HEADS, past), F32))
        out_specs.append(pl.BlockSpec((1, N_HEADS, past), lambda i, j: (i, 0, 0)))
    outs = pl.pallas_call(
        functools.partial(_logf_kernel, tt=tt, has_past=has_past),
        grid=(b, nt), in_specs=in_specs, out_specs=out_specs, out_shape=out_shape,
        scratch_shapes=[pltpu.VMEM((N_HEADS, LANE), F32)],
        compiler_params=_params("parallel", "arbitrary"),
        name="logf_cumsum",
    )(*args)
    return outs[0], outs[1], (outs[2] if has_past else None)


def _lru_kernel(cx_ref, cg_ref, buf0_ref, h0_ref, cw_ref, cb_ref, wr_ref, br_ref, wi_ref, bi_ref,
                lam_ref, oc_ref, nbuf_ref, hl_ref, tail_sc, h_sc, *, tt):
    w = cx_ref.shape[-1]
    blk = w // LRU_BLOCKS
    pad = SUBLANE - (CONV_WIDTH - 1)

    @pl.when(pl.program_id(1) == 0)
    def _():
        tail_sc[0:pad, :] = jnp.zeros((pad, w), F32)
        tail_sc[pad:SUBLANE, :] = buf0_ref[0]
        h_sc[...] = h0_ref[0]

    full = jnp.concatenate([tail_sc[...], cx_ref[0]], axis=0)
    cw = cw_ref[...]
    xc = cb_ref[...]
    for j in range(CONV_WIDTH):
        xc = xc + full[pad + j:pad + j + tt] * cw[j:j + 1]
    tail_sc[...] = full[tt:tt + SUBLANE]
    nbuf_ref[0] = full[tt + pad:tt + SUBLANE]

    xb = xc.astype(BF16)

    def gate(w_ref, b_ref):
        parts = [jnp.dot(xb[:, g * blk:(g + 1) * blk], w_ref[g].astype(BF16), preferred_element_type=F32)
                 for g in range(LRU_BLOCKS)]
        return jax.nn.sigmoid(jnp.concatenate(parts, axis=1) + b_ref[...])

    r = gate(wr_ref, br_ref)
    i = gate(wi_ref, bi_ref)
    nl = -lam_ref[...]
    softplus = jnp.maximum(nl, 0.0) + jnp.log1p(jnp.exp(-jnp.abs(nl)))
    log_a = (-LRU_C * r) * softplus
    a = jnp.exp(log_a)
    u = jnp.sqrt(-jnp.tanh(log_a) * (a * a + 1.0)) * (i * xc)

    row = lax.broadcasted_iota(jnp.int32, (tt, w), 0)
    s = 1
    while s < tt:
        valid = row >= s
        u = jnp.where(valid, a * pltpu.roll(u, s, 0) + u, u)
        a = jnp.where(valid, a * pltpu.roll(a, s, 0), a)
        s *= 2
    h = u + a * h_sc[...]
    h_sc[...] = h[tt - 1:tt]
    hl_ref[0] = h[tt - 1:tt]
    oc_ref[0] = (h * _gelu(cg_ref[0])).astype(oc_ref.dtype)


def _lru(zcb, conv_buf, h0, conv_w, conv_b, w_r, b_r, w_i, b_i, lam):
    b, t, _ = zcb.shape
    w = BRANCH_WIDTH
    tt = _tile(t, 256, SUBLANE)
    vec = pl.BlockSpec((1, w), lambda i, j: (0, 0))
    mat = pl.BlockSpec((LRU_BLOCKS, w // LRU_BLOCKS, w // LRU_BLOCKS), lambda i, j: (0, 0, 0))
    return pl.pallas_call(
        functools.partial(_lru_kernel, tt=tt),
        grid=(b, t // tt),
        in_specs=[pl.BlockSpec((1, tt, w), lambda i, j: (i, j, REST_CX)),
                  pl.BlockSpec((1, tt, w), lambda i, j: (i, j, REST_CG)),
                  pl.BlockSpec((1, CONV_WIDTH - 1, w), lambda i, j: (i, 0, 0)),
                  pl.BlockSpec((1, 1, w), lambda i, j: (i, 0, 0)),
                  pl.BlockSpec((CONV_WIDTH, w), lambda i, j: (0, 0)),
                  vec, mat, vec, mat, vec, vec],
        out_specs=[pl.BlockSpec((1, tt, w), lambda i, j: (i, j, 0)),
                   pl.BlockSpec((1, CONV_WIDTH - 1, w), lambda i, j: (i, 0, 0)),
                   pl.BlockSpec((1, 1, w), lambda i, j: (i, 0, 0))],
        out_shape=[jax.ShapeDtypeStruct((b, t, w), BF16),
                   jax.ShapeDtypeStruct((b, CONV_WIDTH - 1, w), F32),
                   jax.ShapeDtypeStruct((b, 1, w), F32)],
        scratch_shapes=[pltpu.VMEM((SUBLANE, w), F32), pltpu.VMEM((1, w), F32)],
        compiler_params=_params("parallel", "arbitrary"),
        name="conv_lru",
    )(zcb, zcb, conv_buf, h0.reshape(b, 1, w), conv_w, conv_b.reshape(1, w), w_r, b_r.reshape(1, w),
      w_i, b_i.reshape(1, w), lam.reshape(1, w))


def _gmlp_kernel(zu_ref, zv_ref, lng_ref, lnb_ref, ws_ref, bst_ref, ob_ref, *vn_refs, ct):
    u = _gelu(zu_ref[0])
    v = _gelu(zv_ref[0])
    w = v.shape[-1]
    gw = w // GMLP_GROUPS
    d = v - jnp.mean(v, axis=-1, keepdims=True)
    vn = d * lax.rsqrt(jnp.mean(d * d, axis=-1, keepdims=True) + EPS) * lng_ref[...] + lnb_ref[...]
    if vn_refs:
        vn_refs[0][0] = vn
    vb = _pad_rows(vn, GMLP_CHUNK).astype(BF16)
    tril = (lax.broadcasted_iota(jnp.int32, (GMLP_CHUNK, GMLP_CHUNK), 0)
            >= lax.broadcasted_iota(jnp.int32, (GMLP_CHUNK, GMLP_CHUNK), 1))
    parts = []
    for g in range(GMLP_GROUPS):
        wg = jnp.where(tril, ws_ref[g], 0.0).astype(BF16)
        s = jnp.dot(wg, vb[:, g * gw:(g + 1) * gw], preferred_element_type=F32) + bst_ref[:, g:g + 1]
        parts.append(s[:ct])
    ob_ref[0] = (u * jnp.concatenate(parts, axis=1)).astype(ob_ref.dtype)


def _gmlp(zcb, ln_g, ln_b, ws, bs, want_rows):
    b, t, _ = zcb.shape
    w = BRANCH_WIDTH
    ct = min(t, GMLP_CHUNK)
    vec = pl.BlockSpec((1, w), lambda i, j: (0, 0))
    row_out = pl.BlockSpec((1, ct, w), lambda i, j: (i, j, 0))
    out_shape = [jax.ShapeDtypeStruct((b, t, w), BF16)]
    out_specs = [row_out]
    if want_rows:
        out_shape.append(jax.ShapeDtypeStruct((b, t, w), F32))
        out_specs.append(row_out)
    outs = pl.pallas_call(
        functools.partial(_gmlp_kernel, ct=ct),
        grid=(b, t // ct),
        in_specs=[pl.BlockSpec((1, ct, w), lambda i, j: (i, j, REST_BU)),
                  pl.BlockSpec((1, ct, w), lambda i, j: (i, j, REST_BV)),
                  vec, vec,
                  pl.BlockSpec((GMLP_GROUPS, GMLP_CHUNK, GMLP_CHUNK), lambda i, j: (0, 0, 0)),
                  pl.BlockSpec((GMLP_CHUNK, GMLP_GROUPS), lambda i, j: (0, 0))],
        out_specs=out_specs, out_shape=out_shape,
        compiler_params=_params("parallel", "parallel"),
        name="gmlp",
    )(zcb, zcb, ln_g.reshape(1, w), ln_b.reshape(1, w), ws, bs.T)
    return (outs[0], outs[1]) if want_rows else (outs[0], None)


def _mixers(h, l, p, caches, bias):
    b, t, d = h.shape
    m = b * t
    hf = h.reshape(m, d)
    bw = BRANCH_WIDTH
    w_in_t = p['w_in_t']
    lam_init = 0.8 - 0.6 * math.exp(-0.3 * l)

    qkv = _matmul_nt(hf, w_in_t, l, 0, OFF_DF, [F32, BF16] if caches is None else [F32])
    z_qkv = qkv[0].reshape(b, t, OFF_DF)
    zcb = _matmul_nt(hf, w_in_t, l, OFF_CX, 4 * bw, [F32])[0].reshape(b, t, 4 * bw)
    ka = z_qkv[:, :, 1 * bw:2 * bw]
    va = z_qkv[:, :, 2 * bw:3 * bw]
    kd = z_qkv[:, :, 4 * bw:5 * bw]
    vd = z_qkv[:, :, 5 * bw:6 * bw]

    if caches is None:
        logf_t, cum, _ = _logf(hf, b, w_in_t, l, p['b_forget'][l], None)
        tq = bias.shape[2]
        z_bf = qkv[1].reshape(b, t, OFF_DF)
        o_a = _attn_a_prompt(z_bf, bias, p['lambda_qk'][l], p['diff_norm_g'][l], tq=tq, lam_init=lam_init)
        o_d = _attn_d_prompt(z_bf, cum, tq=tq)
        conv_buf = jnp.zeros((b, CONV_WIDTH - 1, bw), F32)
        lru_h0 = jnp.zeros((b, bw), F32)
    else:
        a_k_t, a_v, d_k, d_v, d_logf_t, state_conv, state_lru = caches
        logf_t, cum_new, cum_past = _logf(hf, b, w_in_t, l, p['b_forget'][l], d_logf_t)
        logf_t = logf_t[:, :, :t]
        cq = cum_new[:, :, :t].reshape(b, N_HEADS, t, 1)
        o_a = _attn_a_sample(z_qkv, a_k_t, a_v, l, bias[0], bias[1], p['lambda_qk'][l], p['diff_norm_g'][l],
                             lam_init=lam_init)
        o_d = _attn_d_sample(z_qkv, d_k, d_v, l, cq, cum_past, cum_new)
        conv_buf, lru_h0 = state_conv[l], state_lru[l]

    o_c, new_buf, h_last = _lru(zcb, conv_buf, lru_h0, p['conv_w'][l], p['conv_b'][l], p['w_rgate'][l],
                                p['b_rgate'][l], p['w_igate'][l], p['b_igate'][l], p['lru_lambda'][l])
    o_b, v_rows = _gmlp(zcb, p['gmlp_ln_g'][l], p['gmlp_ln_b'][l], p['gmlp_ws'][l], p['gmlp_bs'][l],
                        want_rows=caches is not None)

    y = _merge(hf, [o.reshape(m, bw) for o in (o_a, o_b, o_c, o_d)], p['bf16_weights']['gate', l],
               p['w_out'], l)
    states = (ka.reshape(b, t, N_HEADS, 2, HEAD_DIM_A), va.reshape(b, t, N_HEADS, HEAD_W),
              kd.reshape(b, t, N_HEADS, HEAD_W), vd.reshape(b, t, N_HEADS, HEAD_W),
              logf_t.transpose(0, 2, 1), new_buf, h_last.reshape(b, bw), v_rows)
    return y, states


def _ffn(h, l, j, p):
    b, t, d = h.shape
    idx = l * 2 + j
    x = h.reshape(b * t, d)
    bf16_weights = p['bf16_weights']
    if ('down', idx) in bf16_weights:
        mid = _ffn_up(x, p['w_ffn_gate'], p['w_ffn_up'], idx)
    else:
        mid, w_down = _ffn_up(x, p['w_ffn_gate'], p['w_ffn_up'], idx, w_down=p['w_ffn_down'])
        bf16_weights['down', idx] = w_down[None]
    w_down = bf16_weights['down', idx]
    if ('gate', l) in bf16_weights:
        y = _matmul(mid, w_down, 0, F32, tm_target=1024, tn_target=256)
    else:
        y, w_gate = _matmul(mid, w_down, 0, F32, tm_target=1024, tn_target=256,
                            cast_rows=(p['w_in_t'], l, OFF_GATE, N_BRANCH * d))
        bf16_weights['gate', l] = w_gate[None]
    return y.reshape(b, t, d)


def _run_group(x, mods, p, caches, bias):
    depth = len(mods)
    states = []
    _, h = _norm(x, pre=(mods[0], p['g_pre'][0], 0))
    for l in range(depth):
        mod, g_pre, g_post = mods[l], p['g_pre'][l], p['g_post'][l]
        x, h = _norm(x, _ffn(h, l, 0, p), post=(mod, g_post, 0, 0.5), pre=(mod, g_pre, 1))
        y, st = _mixers(h, l, p, caches, bias)
        states.append(st)
        x, h = _norm(x, y.reshape(x.shape), post=(mod, g_post, 1, 1.0), pre=(mod, g_pre, 2))
        nxt = (mods[l + 1], p['g_pre'][l + 1], 0) if l + 1 < depth else None
        x, h = _norm(x, _ffn(h, l, 1, p), post=(mod, g_post, 2, 0.5), pre=nxt)
    return x, states


def kernel(x_prompt, x_sample, c_prompt, c_sample, cache_a_k, cache_a_v, cache_d_k, cache_d_v, cache_d_logf, state_conv, state_lru, w_ada, b_ada, g_pre, g_post, w_ffn_gate, w_ffn_up, w_ffn_down, w_in, w_out, rel_bias, lambda_qk, diff_norm_g, gmlp_ln_g, gmlp_ln_b, gmlp_ws, gmlp_bs, conv_w, conv_b, w_rgate, b_rgate, w_igate, b_igate, lru_lambda, b_forget):
    depth, d_model, _ = w_in.shape
    bp, seq, _ = x_prompt.shape
    bs, dec_seq, _ = x_sample.shape
    past = cache_a_k.shape[2]
    d_ff = w_ffn_gate.shape[-1]
    p = {'g_pre': g_pre, 'g_post': g_post,
         'w_ffn_gate': w_ffn_gate.reshape(depth * 2, d_model, d_ff),
         'w_ffn_up': w_ffn_up.reshape(depth * 2, d_model, d_ff),
         'w_ffn_down': w_ffn_down.reshape(depth * 2, d_ff, d_model),
         'w_in_t': jnp.swapaxes(w_in, 1, 2),
         'bf16_weights': {},
         'w_out': w_out.reshape(depth * N_BRANCH, BRANCH_WIDTH, d_model),
         'lambda_qk': lambda_qk, 'diff_norm_g': diff_norm_g, 'gmlp_ln_g': gmlp_ln_g,
         'gmlp_ln_b': gmlp_ln_b, 'gmlp_ws': gmlp_ws, 'gmlp_bs': gmlp_bs, 'conv_w': conv_w,
         'conv_b': conv_b, 'w_rgate': w_rgate, 'b_rgate': b_rgate, 'w_igate': w_igate,
         'b_igate': b_igate, 'lru_lambda': lru_lambda, 'b_forget': b_forget}

    mod = _ada(jnp.concatenate([c_prompt, c_sample], axis=0), w_ada, b_ada)
    mod = mod.reshape(depth, bp + bs, 3 * N_SUB, d_model)
    mods_p = [mod[l, :bp] for l in range(depth)]
    mods_s = [mod[l, bp:] for l in range(depth)]

    tq = _tile(seq, 256, LANE)
    assert tq + 1 >= T5_SATURATION and tq % CHUNK == 0
    bias_p = _bias_tiles(rel_bias, tq, tq, [(0, 0), (tq, 0), None], True, seq)
    xp, st_p = _run_group(x_prompt, mods_p, p, None, bias_p)

    bias_s = (_bias_tiles(rel_bias, dec_seq, past, [(past, 0)], False, past + dec_seq),
              _bias_tiles(rel_bias, dec_seq, LANE, [(past, past)], False, past + dec_seq))
    caches = (cache_a_k.transpose(0, 1, 3, 4, 5, 2).reshape(depth, bs, N_HEADS, HEAD_W, past),
              cache_a_v, cache_d_k, cache_d_v,
              cache_d_logf.transpose(0, 1, 3, 2), state_conv, state_lru)
    xs, st_s = _run_group(x_sample, mods_s, p, caches, bias_s)

    np_ = [jnp.stack([st_p[l][j] for l in range(depth)]) for j in range(7)]
    ns_ = [jnp.stack([st_s[l][j] for l in range(depth)]) for j in range(8)]
    return (xp, xs, *np_, *ns_)
```

```python
import functools
import math

import jax
import jax.numpy as jnp
from jax import lax
from jax.experimental import pallas as pl
from jax.experimental.pallas import tpu as pltpu

F32 = jnp.float32
BF16 = jnp.bfloat16

LANE = 128
SUBLANE = 8
VMEM_LIMIT_BYTES = 56 * 1024 * 1024

CHUNK = 64
CHUNK_SHIFT = 6
N_HEADS = 8
HEAD_W = 128
HEAD_DIM_A = 64
N_BRANCH = 4
BRANCH_WIDTH = 1024
LRU_BLOCKS = 8
LRU_C = 8.0
CONV_WIDTH = 4
GMLP_GROUPS = 8
GMLP_CHUNK = 128
NUM_BUCKETS = 32
N_SUB = 3
EPS = 1e-6
NEG_INF = -1e30
LOG2E = 1.0 / math.log(2.0)
T5_LARGE_STARTS = (12, 16, 23, 32, 46, 64, 91)
T5_SATURATION = 91
T5_FAR_BUCKET = NUM_BUCKETS // 2 - 1

OFF_DF = 6 * BRANCH_WIDTH
OFF_CX = OFF_DF + N_HEADS
OFF_GATE = OFF_CX + 4 * BRANCH_WIDTH
REST_CX, REST_CG, REST_BU, REST_BV = 0, 1, 2, 3

NT = (((1,), (1,)), ((), ()))
TN = (((0,), (0,)), ((), ()))


def _params(*sem):
    return pltpu.CompilerParams(dimension_semantics=sem, vmem_limit_bytes=VMEM_LIMIT_BYTES)


def _tile(n, target, mult):
    best = None
    for t in range(mult, min(n, target) + 1, mult):
        if n % t == 0:
            best = t
    return n if best is None else best


def _resident(shape, index_map):
    return pl.BlockSpec(shape, index_map, pipeline_mode=pl.Buffered(1))


def _gelu(x):
    return x * (0.5 * (1.0 + jnp.tanh(math.sqrt(2.0 / math.pi) * (x + 0.044715 * (x * x * x)))))


def _rms(x, g):
    return x * lax.rsqrt(jnp.mean(x * x, axis=-1, keepdims=True) + EPS) * g


def _pad_rows(x, rows):
    if x.shape[0] == rows:
        return x
    return jnp.concatenate([x, jnp.zeros((rows - x.shape[0],) + x.shape[1:], x.dtype)], axis=0)


def _lane_cumsum(x):
    n = x.shape[-1]
    lane = lax.broadcasted_iota(jnp.int32, x.shape, x.ndim - 1)
    s = 1
    while s < n:
        x = x + jnp.where(lane >= s, pltpu.roll(x, s, x.ndim - 1), 0.0)
        s *= 2
    return x


def _ada_kernel(c_ref, w_ref, b_ref, o_ref):
    c = c_ref[...]
    a = (c * jax.nn.sigmoid(c)).astype(BF16)
    o_ref[0] = jnp.dot(a, w_ref[0].astype(BF16), preferred_element_type=F32) + b_ref[0]


def _ada(c_all, w_ada, b_ada):
    depth, d, n = w_ada.shape
    bt = c_all.shape[0]
    tn = _tile(n, 512, LANE)
    return pl.pallas_call(
        _ada_kernel,
        grid=(depth, n // tn),
        in_specs=[pl.BlockSpec((bt, d), lambda l, j: (0, 0)),
                  pl.BlockSpec((1, d, tn), lambda l, j: (l, 0, j)),
                  pl.BlockSpec((1, 1, tn), lambda l, j: (l, 0, j))],
        out_specs=pl.BlockSpec((1, bt, tn), lambda l, j: (l, 0, j)),
        out_shape=jax.ShapeDtypeStruct((depth, bt, n), F32),
        compiler_params=_params("parallel", "parallel"),
        name="ada",
    )(c_all, w_ada, b_ada.reshape(depth, 1, n))


def _norm_kernel(*refs, k_post, coef, k_pre):
    refs = list(refs)
    x_ref = refs.pop(0)
    x = x_ref[0]
    if k_post is not None:
        y_ref, mpost_ref, gpost_ref = refs.pop(0), refs.pop(0), refs.pop(0)
    if k_pre is not None:
        mpre_ref, gpre_ref = refs.pop(0), refs.pop(0)
    if k_post is not None:
        xo_ref = refs.pop(0)
        m = mpost_ref[0]
        gate = m[3 * k_post + 2:3 * k_post + 3]
        x = x + (coef * gate) * _rms(y_ref[0], gpost_ref[k_post:k_post + 1, :])
        xo_ref[0] = x
    if k_pre is not None:
        ho_ref = refs.pop(0)
        m = mpre_ref[0]
        shift = m[3 * k_pre:3 * k_pre + 1]
        scale = m[3 * k_pre + 1:3 * k_pre + 2]
        ho_ref[0] = (_rms(x, gpre_ref[k_pre:k_pre + 1, :]) * (1.0 + scale) + shift).astype(BF16)


def _norm(x, y=None, post=None, pre=None):
    b, t, d = x.shape
    tt = _tile(t, 256, SUBLANE)
    row = pl.BlockSpec((1, tt, d), lambda i, j: (i, j, 0))
    mod_spec = pl.BlockSpec((1, 3 * N_SUB, d), lambda i, j: (i, 0, 0))
    g_spec = pl.BlockSpec((N_SUB, d), lambda i, j: (0, 0))
    args, in_specs, out_shape, out_specs = [x], [row], [], []
    if post is not None:
        args += [y, post[0], post[1]]
        in_specs += [row, mod_spec, g_spec]
        out_shape.append(jax.ShapeDtypeStruct((b, t, d), F32))
        out_specs.append(row)
    if pre is not None:
        args += [pre[0], pre[1]]
        in_specs += [mod_spec, g_spec]
        out_shape.append(jax.ShapeDtypeStruct((b, t, d), BF16))
        out_specs.append(row)
    outs = pl.pallas_call(
        functools.partial(_norm_kernel, k_post=None if post is None else post[2],
                          coef=None if post is None else post[3],
                          k_pre=None if pre is None else pre[2]),
        grid=(b, t // tt), in_specs=in_specs, out_specs=out_specs, out_shape=out_shape,
        compiler_params=_params("parallel", "parallel"),
        name="norm",
    )(*args)
    outs = list(outs)
    x_new = outs.pop(0) if post is not None else None
    h = outs.pop(0) if pre is not None else None
    return x_new, h


def _mm_kernel(x_ref, w_ref, *refs):
    if len(refs) == 3:
        refs[2][...] = refs[0][...].astype(BF16)
    o_ref = refs[-2] if len(refs) == 3 else refs[0]
    o_ref[...] = jnp.dot(x_ref[...], w_ref[0].astype(BF16),
                         preferred_element_type=F32).astype(o_ref.dtype)


def _matmul(x, w, l, out_dtype, tm_target, tn_target, cast_rows=None):
    m, k = x.shape
    n = w.shape[-1]
    tm = _tile(m, tm_target, 16)
    tn = _tile(n, tn_target, LANE)
    nj = n // tn
    args = [x, w]
    in_specs = [_resident((tm, k), lambda i, j: (i, 0)),
                pl.BlockSpec((1, k, tn), lambda i, j: (l, 0, j))]
    out_specs = [pl.BlockSpec((tm, tn), lambda i, j: (i, j))]
    out_shape = [jax.ShapeDtypeStruct((m, n), out_dtype)]
    if cast_rows is not None:
        wt, layer, row0, nrows = cast_rows
        slab = nrows // ((m // tm) * nj)
        assert slab * (m // tm) * nj == nrows and slab % 16 == 0
        d = wt.shape[-1]
        args.append(wt)
        in_specs.append(pl.BlockSpec(
            (None, pl.Element(slab), pl.Element(d)),
            lambda i, j: (layer, pl.multiple_of(row0 + (i * nj + j) * slab, SUBLANE), 0)))
        out_specs.append(pl.BlockSpec((slab, d), lambda i, j: (i * nj + j, 0)))
        out_shape.append(jax.ShapeDtypeStruct((nrows, d), BF16))
    outs = pl.pallas_call(
        _mm_kernel,
        grid=(m // tm, nj),
        in_specs=in_specs, out_specs=out_specs, out_shape=out_shape,
        compiler_params=_params("parallel", "arbitrary"),
        name="matmul",
    )(*args)
    return outs if cast_rows is not None else outs[0]


def _mm_nt_kernel(x_ref, w_ref, *o_refs):
    z = lax.dot_general(x_ref[...], w_ref[...].astype(BF16), NT, preferred_element_type=F32)
    for o_ref in o_refs:
        o_ref[...] = z.astype(o_ref.dtype)


def _matmul_nt(x, wt, l, row0, nrows, out_dtypes, tm_target=2048, tn_target=512):
    m, k = x.shape
    tm = _tile(m, tm_target, 16)
    tn = _tile(nrows, tn_target, LANE)
    return pl.pallas_call(
        _mm_nt_kernel,
        grid=(m // tm, nrows // tn),
        in_specs=[_resident((tm, k), lambda i, j: (i, 0)),
                  pl.BlockSpec((None, pl.Element(tn), pl.Element(k)),
                               lambda i, j: (l, pl.multiple_of(row0 + j * tn, SUBLANE), 0))],
        out_specs=[pl.BlockSpec((tm, tn), lambda i, j: (i, j)) for _ in out_dtypes],
        out_shape=[jax.ShapeDtypeStruct((m, nrows), dt) for dt in out_dtypes],
        compiler_params=_params("parallel", "arbitrary"),
        name="matmul_nt",
    )(x, wt)


def _ffn_up_kernel(x_ref, wg_ref, wu_ref, *refs):
    if len(refs) == 3:
        refs[2][...] = refs[0][0].astype(BF16)
    o_ref = refs[-2] if len(refs) == 3 else refs[0]
    x = x_ref[...]
    g = jnp.dot(x, wg_ref[0].astype(BF16), preferred_element_type=F32)
    u = jnp.dot(x, wu_ref[0].astype(BF16), preferred_element_type=F32)
    o_ref[...] = ((g * jax.nn.sigmoid(g)) * u).astype(o_ref.dtype)


def _ffn_up(x, wg, wu, l, w_down=None):
    m, k = x.shape
    n = wg.shape[-1]
    tm = _tile(m, 2048, 16)
    tn = _tile(n, 256, LANE)
    nj = n // tn
    w_spec = pl.BlockSpec((1, k, tn), lambda i, j: (l, 0, j))
    args = [x, wg, wu]
    in_specs = [_resident((tm, k), lambda i, j: (i, 0)), w_spec, w_spec]
    out_specs = [pl.BlockSpec((tm, tn), lambda i, j: (i, j))]
    out_shape = [jax.ShapeDtypeStruct((m, n), BF16)]
    if w_down is not None:
        d = w_down.shape[-1]
        slab = n // ((m // tm) * nj)
        assert slab * (m // tm) * nj == n and slab % 16 == 0
        args.append(w_down)
        in_specs.append(pl.BlockSpec((1, slab, d), lambda i, j: (l, i * nj + j, 0)))
        out_specs.append(pl.BlockSpec((slab, d), lambda i, j: (i * nj + j, 0)))
        out_shape.append(jax.ShapeDtypeStruct((n, d), BF16))
    outs = pl.pallas_call(
        _ffn_up_kernel,
        grid=(m // tm, nj),
        in_specs=in_specs, out_specs=out_specs, out_shape=out_shape,
        compiler_params=_params("parallel", "arbitrary"),
        name="ffn_up",
    )(*args)
    return outs if w_down is not None else outs[0]


def _merge_kernel(h_ref, oa_ref, ob_ref, oc_ref, od_ref, wg0, wg1, wg2, wg3, wo0, wo1, wo2, wo3, y_ref):
    h = h_ref[...]
    y = None
    for o_ref, wg, wo in ((oa_ref, wg0, wo0), (ob_ref, wg1, wo1), (oc_ref, wg2, wo2), (od_ref, wg3, wo3)):
        gate = jax.nn.sigmoid(lax.dot_general(h, wg[0], NT, preferred_element_type=F32))
        proj = jnp.dot(o_ref[...], wo[0].astype(BF16), preferred_element_type=F32)
        y = gate * proj if y is None else y + gate * proj
    y_ref[...] = y


def _merge(h, branches, w_gate, w_out, l):
    m, d = h.shape
    n = w_out.shape[-1]
    bw = branches[0].shape[-1]
    tm = _tile(m, 1024, 16)
    tn = _tile(n, 256, LANE)
    nj = n // tn

    def gate_spec(i):
        return pl.BlockSpec((1, tn, d), lambda a, j: (0, i * nj + j, 0))

    def out_spec(i):
        return pl.BlockSpec((1, bw, tn), lambda a, j: (l * N_BRANCH + i, 0, j))

    return pl.pallas_call(
        _merge_kernel,
        grid=(m // tm, nj),
        in_specs=[_resident((tm, d), lambda a, j: (a, 0))] + [_resident((tm, bw), lambda a, j: (a, 0))] * N_BRANCH
        + [gate_spec(i) for i in range(N_BRANCH)] + [out_spec(i) for i in range(N_BRANCH)],
        out_specs=pl.BlockSpec((tm, tn), lambda a, j: (a, j)),
        out_shape=jax.ShapeDtypeStruct((m, n), F32),
        compiler_params=_params("parallel", "arbitrary"),
        name="merge",
    )(h, *branches, w_gate, w_gate, w_gate, w_gate, w_out, w_out, w_out, w_out)


def _bias_kernel(tab_ref, o_ref, *, nq, nk, tiles, transposed, kv_end):
    h = pl.program_id(0)
    shape = (nk, nq) if transposed else (nq, nk)
    q_i = lax.broadcasted_iota(jnp.int32, shape, 1 if transposed else 0)
    k_i = lax.broadcasted_iota(jnp.int32, shape, 0 if transposed else 1)
    for c, tile in enumerate(tiles):
        for mp in range(2):
            if tile is None:
                val = jnp.full(shape, NEG_INF, F32)
            else:
                q0, k0 = tile
                q_pos, k_pos = q0 + q_i, k0 + k_i
                rel = k_pos - q_pos
                n = jnp.abs(rel)
                large = NUM_BUCKETS // 4
                for start in T5_LARGE_STARTS:
                    large = large + jnp.where(n >= start, 1, 0)
                bucket = jnp.where(rel > 0, NUM_BUCKETS // 2, 0) + jnp.where(n < NUM_BUCKETS // 4, n, large)
                val = jnp.zeros(shape, F32)
                for b in range(NUM_BUCKETS):
                    val = jnp.where(bucket == b, tab_ref[b * (2 * N_HEADS) + h * 2 + mp], val)
                val = val - tab_ref[T5_FAR_BUCKET * (2 * N_HEADS) + h * 2 + mp]
                visible = jnp.logical_and((k_pos >> CHUNK_SHIFT) <= (q_pos >> CHUNK_SHIFT), k_pos < kv_end)
                val = jnp.where(visible, val, NEG_INF)
            if transposed:
                o_ref[0, c, :, mp * nq:(mp + 1) * nq] = val
            else:
                o_ref[0, c, mp * nq:(mp + 1) * nq, :] = val


def _bias_tiles(rel_bias, nq, nk, tiles, transposed, kv_end):
    shape = (nk, 2 * nq) if transposed else (2 * nq, nk)
    return pl.pallas_call(
        functools.partial(_bias_kernel, nq=nq, nk=nk, tiles=tuple(tiles), transposed=transposed, kv_end=kv_end),
        grid=(N_HEADS,),
        in_specs=[pl.BlockSpec(memory_space=pltpu.SMEM)],
        out_specs=pl.BlockSpec((1, len(tiles)) + shape, lambda h: (h, 0, 0, 0)),
        out_shape=jax.ShapeDtypeStruct((N_HEADS, len(tiles)) + shape, F32),
        compiler_params=_params("parallel"),
        name="bias_tiles",
    )(rel_bias.reshape(-1))


def _lambda(lq_ref, lam_init):
    lq = lq_ref[...]
    return (jnp.exp(jnp.sum(lq[0:1] * lq[1:2], axis=-1, keepdims=True))
            - jnp.exp(jnp.sum(lq[2:3] * lq[3:4], axis=-1, keepdims=True)) + lam_init)


def _split_maps(q):
    lane = lax.broadcasted_iota(jnp.int32, q.shape, 1)
    q = q * (HEAD_DIM_A ** -0.5)
    return jnp.concatenate([jnp.where(lane < HEAD_DIM_A, q, 0.0),
                            jnp.where(lane >= HEAD_DIM_A, q, 0.0)], axis=0).astype(BF16)


def _flash_tile_t(carries, acc_sc, scores, values, query_terms=None):
    out, probs, alphas = [], [], []
    for g, ((m, l), s) in enumerate(zip(carries, scores)):
        if query_terms is None:
            m_new = jnp.maximum(m, jnp.max(s, axis=0, keepdims=True))
            p = jnp.exp(s - m_new)
            alpha = jnp.exp(m - m_new)
        else:
            m_new = jnp.maximum(m, jnp.max(s, axis=0, keepdims=True) + query_terms[g])
            p = jnp.exp2(s - (m_new - query_terms[g]))
            alpha = jnp.exp2(m - m_new)
        out.append((m_new, alpha * l + jnp.sum(p, axis=0, keepdims=True)))
        probs.append(p.astype(BF16))
        alphas.append(alpha)
    for g, (v, p, alpha) in enumerate(zip(values, probs, alphas)):
        acc_sc[g] = alpha * acc_sc[g] + lax.dot_general(v, p, TN, preferred_element_type=F32)
    return tuple(out)


def _flash_init(acc_ref):
    acc_ref[...] = jnp.zeros(acc_ref.shape, F32)
    nq = acc_ref.shape[-1]
    return jnp.full((1, nq), NEG_INF, F32), jnp.zeros((1, nq), F32)


HEADS_PER_STEP = 8


def _head(x, g):
    return x[:, g * HEAD_W:(g + 1) * HEAD_W]


def _attn_a_prompt_kernel(lq_ref, q_ref, k_ref, v_ref, bias_ref, g_ref, o_ref, acc_sc, *, tq, lam_init):
    qi = pl.program_id(2)
    heads = range(HEADS_PER_STEP)
    q_all = q_ref[0]
    q2 = [_split_maps(_head(q_all, g)) for g in heads]

    def tile(kt, carries, bias_tile):
        k0 = pl.multiple_of(kt * tq, tq)
        k_all = k_ref[0, pl.ds(k0, tq), :]
        v_all = v_ref[0, pl.ds(k0, tq), :]
        scores = [lax.dot_general(_head(k_all, g).astype(BF16), q2[g], NT, preferred_element_type=F32)
                  for g in heads]
        if bias_tile is not None:
            scores = [s + bias_ref[g, bias_tile] for g, s in enumerate(scores)]
        return _flash_tile_t(carries, acc_sc, scores, [_head(v_all, g).astype(BF16) for g in heads])

    carries = tile(qi, tuple(_flash_init(acc_sc.at[g]) for g in heads), 0)
    carries = tile(jnp.maximum(qi - 1, 0), carries, jnp.where(qi == 0, 2, 1))
    carries = lax.fori_loop(0, qi - 1, lambda kt, c: tile(kt, c, None), carries)
    lam = _lambda(lq_ref, lam_init)
    for g in heads:
        o = acc_sc[g] * (1.0 / carries[g][1])
        o = (o[:, :tq] - lam * o[:, tq:]).T
        o_ref[0, :, g * HEAD_W:(g + 1) * HEAD_W] = (_rms(o, g_ref[...]) * (1.0 - lam_init)).astype(o_ref.dtype)


def _attn_a_prompt(z_qkv, bias, lq, g_norm, *, tq, lam_init):
    b, t, _ = z_qkv.shape
    gw = HEADS_PER_STEP * HEAD_W
    nblk = BRANCH_WIDTH // gw
    return pl.pallas_call(
        functools.partial(_attn_a_prompt_kernel, tq=tq, lam_init=lam_init),
        grid=(b, nblk, t // tq),
        in_specs=[pl.BlockSpec((4, HEAD_DIM_A), lambda i, h, j: (0, 0)),
                  pl.BlockSpec((1, tq, gw), lambda i, h, j: (i, j, h)),
                  _resident((1, t, gw), lambda i, h, j: (i, 0, nblk + h)),
                  _resident((1, t, gw), lambda i, h, j: (i, 0, 2 * nblk + h)),
                  _resident((HEADS_PER_STEP, 3, tq, 2 * tq), lambda i, h, j: (h, 0, 0, 0)),
                  pl.BlockSpec((1, HEAD_W), lambda i, h, j: (0, 0))],
        out_specs=pl.BlockSpec((1, tq, gw), lambda i, h, j: (i, j, h)),
        out_shape=jax.ShapeDtypeStruct((b, t, N_HEADS * HEAD_W), BF16),
        scratch_shapes=[pltpu.VMEM((HEADS_PER_STEP, HEAD_W, 2 * tq), F32)],
        compiler_params=_params("parallel", "parallel", "arbitrary"),
        name="attn_a_prompt",
    )(lq, z_qkv, z_qkv, z_qkv, bias, g_norm.reshape(1, HEAD_W))


def _attn_d_prompt_kernel(q_ref, k_ref, v_ref, cq_ref, ck_ref, o_ref, ckb_sc, acc_sc, *, tq, t):
    h0 = pl.program_id(1) * HEADS_PER_STEP
    qi = pl.program_id(2)
    heads = range(HEADS_PER_STEP)

    @pl.when(qi == 0)
    def _():
        for g in heads:
            ck = ck_ref[0, pl.ds(h0 + g, 1), :] * LOG2E
            for c in range(t // LANE):
                ckb_sc[g, c * LANE:(c + 1) * LANE, :] = jnp.broadcast_to(
                    ck[:, c * LANE:(c + 1) * LANE], (LANE, LANE)).T

    q_all = q_ref[0]
    q = [_head(q_all, g).astype(BF16) for g in heads]
    cq = [cq_ref[0, pl.ds(h0 + g, 1), :] * LOG2E for g in heads]
    scale = HEAD_W ** -0.5 * LOG2E

    def tile(kt, carries, diagonal):
        k0 = pl.multiple_of(kt * tq, tq)
        k_all = k_ref[0, pl.ds(k0, tq), :]
        v_all = v_ref[0, pl.ds(k0, tq), :]
        scores = [lax.dot_general(_head(k_all, g).astype(BF16), q[g], NT, preferred_element_type=F32)
                  for g in heads]
        for g in heads:
            ckb = ckb_sc[g, pl.ds(k0, tq), :]
            s = scores[g] * scale - jnp.concatenate([ckb] * (tq // LANE), axis=1)
            if diagonal:
                key = lax.broadcasted_iota(jnp.int32, (tq, tq), 0)
                qry = lax.broadcasted_iota(jnp.int32, (tq, tq), 1)
                s = jnp.where(key <= qry, s, NEG_INF)
            scores[g] = s
        return _flash_tile_t(carries, acc_sc, scores, [_head(v_all, g).astype(BF16) for g in heads], cq)

    carries = tile(qi, tuple(_flash_init(acc_sc.at[g]) for g in heads), True)
    carries = lax.fori_loop(0, qi, lambda kt, c: tile(kt, c, False), carries)
    for g in heads:
        o = acc_sc[g] * (1.0 / carries[g][1])
        o_ref[0, :, g * HEAD_W:(g + 1) * HEAD_W] = o.T.astype(o_ref.dtype)


def _attn_d_prompt(z_qkv, cum, *, tq):
    b, t, _ = z_qkv.shape
    gw = HEADS_PER_STEP * HEAD_W
    nblk = BRANCH_WIDTH // gw
    return pl.pallas_call(
        functools.partial(_attn_d_prompt_kernel, tq=tq, t=t),
        grid=(b, nblk, t // tq),
        in_specs=[pl.BlockSpec((1, tq, gw), lambda i, h, j: (i, j, 3 * nblk + h)),
                  _resident((1, t, gw), lambda i, h, j: (i, 0, 4 * nblk + h)),
                  _resident((1, t, gw), lambda i, h, j: (i, 0, 5 * nblk + h)),
                  pl.BlockSpec((1, N_HEADS, tq), lambda i, h, j: (i, 0, j)),
                  pl.BlockSpec((1, N_HEADS, t), lambda i, h, j: (i, 0, 0))],
        out_specs=pl.BlockSpec((1, tq, gw), lambda i, h, j: (i, j, h)),
        out_shape=jax.ShapeDtypeStruct((b, t, N_HEADS * HEAD_W), BF16),
        scratch_shapes=[pltpu.VMEM((HEADS_PER_STEP, t, LANE), F32),
                        pltpu.VMEM((HEADS_PER_STEP, HEAD_W, tq), F32)],
        compiler_params=_params("parallel", "arbitrary", "arbitrary"),
        name="attn_d_prompt",
    )(z_qkv, z_qkv, z_qkv, cum, cum)


def _softmax_two(s_p, s_n, v_p, v_n):
    m = jnp.maximum(jnp.max(s_p, axis=-1, keepdims=True), jnp.max(s_n, axis=-1, keepdims=True))
    p_p = jnp.exp(s_p - m)
    p_n = jnp.exp(s_n - m)
    l = jnp.sum(p_p, axis=-1, keepdims=True) + jnp.sum(p_n, axis=-1, keepdims=True)
    acc = (jnp.dot(p_p.astype(BF16), v_p, preferred_element_type=F32)
           + jnp.dot(p_n.astype(BF16), v_n, preferred_element_type=F32))
    return acc / l


def _attn_a_sample_kernel(lq_ref, q_ref, kt_ref, vp_ref, kn_ref, vn_ref, bp_ref, bn_ref, g_ref, o_ref, *,
                          t, lam_init):
    lam = _lambda(lq_ref, lam_init)
    q_all, kn_all, vn_all = q_ref[0], kn_ref[0], vn_ref[0]
    for h in range(N_HEADS):
        q2 = _split_maps(_head(q_all, h))
        kn = _pad_rows(_head(kn_all, h), LANE).astype(BF16)
        vn = _pad_rows(_head(vn_all, h), LANE).astype(BF16)
        s_p = jnp.dot(q2, kt_ref[0, 0, h].astype(BF16), preferred_element_type=F32) + bp_ref[h, 0]
        s_n = lax.dot_general(q2, kn, NT, preferred_element_type=F32) + bn_ref[h, 0]
        o = _softmax_two(s_p, s_n, vp_ref[0, 0, :, h, :].astype(BF16), vn)
        o = o[:t] - lam * o[t:]
        o_ref[0, :, h * HEAD_W:(h + 1) * HEAD_W] = (_rms(o, g_ref[...]) * (1.0 - lam_init)).astype(o_ref.dtype)


def _attn_a_sample(z_qkv, k_cache_t, v_cache, l, bias_past, bias_new, lq, g_norm, *, lam_init):
    b, t, _ = z_qkv.shape
    past = v_cache.shape[2]
    bw = BRANCH_WIDTH
    return pl.pallas_call(
        functools.partial(_attn_a_sample_kernel, t=t, lam_init=lam_init),
        grid=(b,),
        in_specs=[pl.BlockSpec((4, HEAD_DIM_A), lambda i: (0, 0)),
                  pl.BlockSpec((1, t, bw), lambda i: (i, 0, 0)),
                  pl.BlockSpec((1, 1, N_HEADS, HEAD_W, past), lambda i: (l, i, 0, 0, 0)),
                  pl.BlockSpec((1, 1, past, N_HEADS, HEAD_W), lambda i: (l, i, 0, 0, 0)),
                  pl.BlockSpec((1, t, bw), lambda i: (i, 0, 1)),
                  pl.BlockSpec((1, t, bw), lambda i: (i, 0, 2)),
                  pl.BlockSpec((N_HEADS, 1, 2 * t, past), lambda i: (0, 0, 0, 0)),
                  pl.BlockSpec((N_HEADS, 1, 2 * t, LANE), lambda i: (0, 0, 0, 0)),
                  pl.BlockSpec((1, HEAD_W), lambda i: (0, 0))],
        out_specs=pl.BlockSpec((1, t, bw), lambda i: (i, 0, 0)),
        out_shape=jax.ShapeDtypeStruct((b, t, bw), BF16),
        compiler_params=_params("parallel"),
        name="attn_a_sample",
    )(lq, z_qkv, k_cache_t, v_cache, z_qkv, z_qkv, bias_past, bias_new, g_norm.reshape(1, HEAD_W))


def _attn_d_sample_kernel(q_ref, kp_ref, vp_ref, kn_ref, vn_ref, cq_ref, ckp_ref, ckn_ref, o_ref, *, t):
    heads = range(N_HEADS)
    past = kp_ref.shape[2]
    scale = HEAD_W ** -0.5
    q_all, kn_all, vn_all = q_ref[0], kn_ref[0], vn_ref[0]
    q = jnp.concatenate([_head(q_all, h) for h in heads], axis=0).astype(BF16)
    cq = jnp.concatenate([cq_ref[0, h] for h in heads], axis=0)
    k_all = kp_ref[0, 0].reshape(past * N_HEADS, HEAD_W).astype(BF16)
    v_all = vp_ref[0, 0].reshape(past * N_HEADS, HEAD_W).astype(BF16)
    s_p = lax.dot_general(q, k_all, NT, preferred_element_type=F32) * scale + cq - ckp_ref[0]
    row_head = jnp.concatenate([jnp.full((t, 1), h, jnp.int32) for h in heads], axis=0)
    col_head = lax.broadcasted_iota(jnp.int32, (1, past * N_HEADS), 1) & (N_HEADS - 1)
    s_p = jnp.where(row_head == col_head, s_p, NEG_INF)

    kn = [_pad_rows(_head(kn_all, h), LANE).astype(BF16) for h in heads]
    vn = [_pad_rows(_head(vn_all, h), LANE).astype(BF16) for h in heads]
    s_n = jnp.concatenate([lax.dot_general(q[h * t:(h + 1) * t], kn[h], NT, preferred_element_type=F32)
                           for h in heads], axis=0) * scale
    ckn = jnp.concatenate([jnp.broadcast_to(ckn_ref[0, h:h + 1, :], (t, LANE)) for h in heads], axis=0)
    qry = jnp.concatenate([lax.broadcasted_iota(jnp.int32, (t, LANE), 0)] * N_HEADS, axis=0)
    key = lax.broadcasted_iota(jnp.int32, (N_HEADS * t, LANE), 1)
    s_n = jnp.where(key <= qry, s_n + cq - ckn, NEG_INF)

    m = jnp.maximum(jnp.max(s_p, axis=-1, keepdims=True), jnp.max(s_n, axis=-1, keepdims=True))
    p_p = jnp.exp(s_p - m)
    p_n = jnp.exp(s_n - m)
    l_sum = jnp.sum(p_p, axis=-1, keepdims=True) + jnp.sum(p_n, axis=-1, keepdims=True)
    p_n = p_n.astype(BF16)
    acc = jnp.dot(p_p.astype(BF16), v_all, preferred_element_type=F32)
    acc = acc + jnp.concatenate([jnp.dot(p_n[h * t:(h + 1) * t], vn[h], preferred_element_type=F32)
                                 for h in heads], axis=0)
    o = acc / l_sum
    for h in heads:
        o_ref[0, :, h * HEAD_W:(h + 1) * HEAD_W] = o[h * t:(h + 1) * t].astype(o_ref.dtype)


def _attn_d_sample(z_qkv, k_cache, v_cache, l, cq, cum_past, cum_new):
    b, t, _ = z_qkv.shape
    past = k_cache.shape[2]
    bw = BRANCH_WIDTH
    assert N_HEADS & (N_HEADS - 1) == 0
    cache_spec = pl.BlockSpec((1, 1, past, N_HEADS, HEAD_W), lambda i: (l, i, 0, 0, 0))
    return pl.pallas_call(
        functools.partial(_attn_d_sample_kernel, t=t),
        grid=(b,),
        in_specs=[pl.BlockSpec((1, t, bw), lambda i: (i, 0, 3)),
                  cache_spec, cache_spec,
                  pl.BlockSpec((1, t, bw), lambda i: (i, 0, 4)),
                  pl.BlockSpec((1, t, bw), lambda i: (i, 0, 5)),
                  pl.BlockSpec((1, N_HEADS, t, 1), lambda i: (i, 0, 0, 0)),
                  pl.BlockSpec((1, 1, past * N_HEADS), lambda i: (i, 0, 0)),
                  pl.BlockSpec((1, N_HEADS, LANE), lambda i: (i, 0, 0))],
        out_specs=pl.BlockSpec((1, t, bw), lambda i: (i, 0, 0)),
        out_shape=jax.ShapeDtypeStruct((b, t, bw), BF16),
        compiler_params=_params("parallel"),
        name="attn_d_sample",
    )(z_qkv, k_cache, v_cache, z_qkv, z_qkv, cq, cum_past, cum_new)


def _logf_kernel(*refs, tt, has_past):
    refs = list(refs)
    h_ref, wf_ref, bf_ref = refs.pop(0), refs.pop(0), refs.pop(0)
    past_ref = refs.pop(0) if has_past else None
    logf_ref, cum_ref = refs.pop(0), refs.pop(0)
    cump_ref = refs.pop(0) if has_past else None
    carry_sc = refs.pop(0)

    @pl.when(pl.program_id(1) == 0)
    def _():
        if has_past:
            cump = _lane_cumsum(past_ref[0, 0])
            cump_ref[0] = cump
            carry_sc[...] = jnp.broadcast_to(cump[:, cump.shape[1] - 1:], carry_sc.shape)
        else:
            carry_sc[...] = jnp.zeros(carry_sc.shape, F32)

    ttp = logf_ref.shape[-1]
    hb = _pad_rows(h_ref[...], ttp)
    z = lax.dot_general(wf_ref[...].astype(BF16), hb, NT, preferred_element_type=F32) + bf_ref[...]
    logf = jnp.minimum(z, 0.0) - jnp.log1p(jnp.exp(-jnp.abs(z)))
    if tt < ttp:
        logf = jnp.where(lax.broadcasted_iota(jnp.int32, logf.shape, 1) < tt, logf, 0.0)
    logf_ref[0] = logf
    cum = _lane_cumsum(logf) + carry_sc[:, 0:1]
    cum_ref[0] = cum
    carry_sc[...] = jnp.broadcast_to(cum[:, ttp - 1:], carry_sc.shape)


def _logf(hf, b, w_in_t, l, b_forget, past_logf_t):
    m, d = hf.shape
    t = m // b
    tt = _tile(t, 1024, LANE) if t >= LANE else t
    ttp = max(tt, LANE)
    nt = t // tt
    has_past = past_logf_t is not None
    args = [hf, w_in_t, b_forget.reshape(N_HEADS, 1)]
    in_specs = [pl.BlockSpec((tt, d), lambda i, j: (i * nt + j, 0)),
                pl.BlockSpec((None, pl.Element(N_HEADS), pl.Element(d)), lambda i, j: (l, OFF_DF, 0)),
                pl.BlockSpec((N_HEADS, 1), lambda i, j: (0, 0))]
    out_shape = [jax.ShapeDtypeStruct((b, N_HEADS, nt * ttp), F32)] * 2
    out_specs = [pl.BlockSpec((1, N_HEADS, ttp), lambda i, j: (i, 0, j))] * 2
    if has_past:
        past = past_logf_t.shape[-1]
        args.append(past_logf_t)
        in_specs.append(pl.BlockSpec((1, 1, N_HEADS, past), lambda i, j: (l, i, 0, 0)))
        out_shape.append(jax.ShapeDtypeStruct((b, N_HEADS, past), F32))
        out_specs.append(pl.BlockSpec((1, N_HEADS, past), lambda i, j: (i, 0, 0)))
    outs = pl.pallas_call(
        functools.partial(_logf_kernel, tt=tt, has_past=has_past),
        grid=(b, nt), in_specs=in_specs, out_specs=out_specs, out_shape=out_shape,
        scratch_shapes=[pltpu.VMEM((N_HEADS, LANE), F32)],
        compiler_params=_params("parallel", "arbitrary"),
        name="logf_cumsum",
    )(*args)
    return outs[0], outs[1], (outs[2] if has_past else None)


def _lru_kernel(cx_ref, cg_ref, buf0_ref, h0_ref, cw_ref, cb_ref, wr_ref, br_ref, wi_ref, bi_ref,
                lam_ref, oc_ref, nbuf_ref, hl_ref, tail_sc, h_sc, *, tt):
    w = cx_ref.shape[-1]
    blk = w // LRU_BLOCKS
    pad = SUBLANE - (CONV_WIDTH - 1)

    @pl.when(pl.program_id(1) == 0)
    def _():
        tail_sc[0:pad, :] = jnp.zeros((pad, w), F32)
        tail_sc[pad:SUBLANE, :] = buf0_ref[0]
        h_sc[...] = h0_ref[0]

    full = jnp.concatenate([tail_sc[...], cx_ref[0]], axis=0)
    cw = cw_ref[...]
    xc = cb_ref[...]
    for j in range(CONV_WIDTH):
        xc = xc + full[pad + j:pad + j + tt] * cw[j:j + 1]
    tail_sc[...] = full[tt:tt + SUBLANE]
    nbuf_ref[0] = full[tt + pad:tt + SUBLANE]

    xb = xc.astype(BF16)

    def gate(w_ref, b_ref):
        parts = [jnp.dot(xb[:, g * blk:(g + 1) * blk], w_ref[g].astype(BF16), preferred_element_type=F32)
                 for g in range(LRU_BLOCKS)]
        return jax.nn.sigmoid(jnp.concatenate(parts, axis=1) + b_ref[...])

    r = gate(wr_ref, br_ref)
    i = gate(wi_ref, bi_ref)
    nl = -lam_ref[...]
    softplus = jnp.maximum(nl, 0.0) + jnp.log1p(jnp.exp(-jnp.abs(nl)))
    log_a = (-LRU_C * r) * softplus
    a = jnp.exp(log_a)
    u = jnp.sqrt(-jnp.tanh(log_a) * (a * a + 1.0)) * (i * xc)

    in_group = lax.broadcasted_iota(jnp.int32, (tt, w), 0) & (SUBLANE - 1)
    s = 1
    while s < SUBLANE:
        valid = in_group >= s
        u = jnp.where(valid, a * pltpu.roll(u, s, 0) + u, u)
        a = jnp.where(valid, a * pltpu.roll(a, s, 0), a)
        s *= 2
    h_prev = h_sc[...]
    groups = []
    for g in range(tt // SUBLANE):
        rows = slice(g * SUBLANE, (g + 1) * SUBLANE)
        groups.append(u[rows] + a[rows] * h_prev)
        h_prev = groups[-1][SUBLANE - 1:SUBLANE]
    h = jnp.concatenate(groups, axis=0)
    h_sc[...] = h_prev
    hl_ref[0] = h_prev
    oc_ref[0] = (h * _gelu(cg_ref[0])).astype(oc_ref.dtype)


def _lru(zcb, conv_buf, h0, conv_w, conv_b, w_r, b_r, w_i, b_i, lam):
    b, t, _ = zcb.shape
    w = BRANCH_WIDTH
    tt = _tile(t, 256, SUBLANE)
    vec = pl.BlockSpec((1, w), lambda i, j: (0, 0))
    mat = pl.BlockSpec((LRU_BLOCKS, w // LRU_BLOCKS, w // LRU_BLOCKS), lambda i, j: (0, 0, 0))
    return pl.pallas_call(
        functools.partial(_lru_kernel, tt=tt),
        grid=(b, t // tt),
        in_specs=[pl.BlockSpec((1, tt, w), lambda i, j: (i, j, REST_CX)),
                  pl.BlockSpec((1, tt, w), lambda i, j: (i, j, REST_CG)),
                  pl.BlockSpec((1, CONV_WIDTH - 1, w), lambda i, j: (i, 0, 0)),
                  pl.BlockSpec((1, 1, w), lambda i, j: (i, 0, 0)),
                  pl.BlockSpec((CONV_WIDTH, w), lambda i, j: (0, 0)),
                  vec, mat, vec, mat, vec, vec],
        out_specs=[pl.BlockSpec((1, tt, w), lambda i, j: (i, j, 0)),
                   pl.BlockSpec((1, CONV_WIDTH - 1, w), lambda i, j: (i, 0, 0)),
                   pl.BlockSpec((1, 1, w), lambda i, j: (i, 0, 0))],
        out_shape=[jax.ShapeDtypeStruct((b, t, w), BF16),
                   jax.ShapeDtypeStruct((b, CONV_WIDTH - 1, w), F32),
                   jax.ShapeDtypeStruct((b, 1, w), F32)],
        scratch_shapes=[pltpu.VMEM((SUBLANE, w), F32), pltpu.VMEM((1, w), F32)],
        compiler_params=_params("parallel", "arbitrary"),
        name="conv_lru",
    )(zcb, zcb, conv_buf, h0.reshape(b, 1, w), conv_w, conv_b.reshape(1, w), w_r, b_r.reshape(1, w),
      w_i, b_i.reshape(1, w), lam.reshape(1, w))


def _gmlp_kernel(zu_ref, zv_ref, lng_ref, lnb_ref, ws_ref, bst_ref, ob_ref, *vn_refs, ct):
    u = _gelu(zu_ref[0])
    v = _gelu(zv_ref[0])
    w = v.shape[-1]
    gw = w // GMLP_GROUPS
    d = v - jnp.mean(v, axis=-1, keepdims=True)
    vn = d * lax.rsqrt(jnp.mean(d * d, axis=-1, keepdims=True) + EPS) * lng_ref[...] + lnb_ref[...]
    if vn_refs:
        vn_refs[0][0] = vn
    vb = _pad_rows(vn, GMLP_CHUNK).astype(BF16)
    tril = (lax.broadcasted_iota(jnp.int32, (GMLP_CHUNK, GMLP_CHUNK), 0)
            >= lax.broadcasted_iota(jnp.int32, (GMLP_CHUNK, GMLP_CHUNK), 1))
    parts = []
    for g in range(GMLP_GROUPS):
        wg = jnp.where(tril, ws_ref[g], 0.0).astype(BF16)
        s = jnp.dot(wg, vb[:, g * gw:(g + 1) * gw], preferred_element_type=F32) + bst_ref[:, g:g + 1]
        parts.append(s[:ct])
    ob_ref[0] = (u * jnp.concatenate(parts, axis=1)).astype(ob_ref.dtype)


def _gmlp(zcb, ln_g, ln_b, ws, bs, want_rows):
    b, t, _ = zcb.shape
    w = BRANCH_WIDTH
    ct = min(t, GMLP_CHUNK)
    vec = pl.BlockSpec((1, w), lambda i, j: (0, 0))
    row_out = pl.BlockSpec((1, ct, w), lambda i, j: (i, j, 0))
    out_shape = [jax.ShapeDtypeStruct((b, t, w), BF16)]
    out_specs = [row_out]
    if want_rows:
        out_shape.append(jax.ShapeDtypeStruct((b, t, w), F32))
        out_specs.append(row_out)
    outs = pl.pallas_call(
        functools.partial(_gmlp_kernel, ct=ct),
        grid=(b, t // ct),
        in_specs=[pl.BlockSpec((1, ct, w), lambda i, j: (i, j, REST_BU)),
                  pl.BlockSpec((1, ct, w), lambda i, j: (i, j, REST_BV)),
                  vec, vec,
                  pl.BlockSpec((GMLP_GROUPS, GMLP_CHUNK, GMLP_CHUNK), lambda i, j: (0, 0, 0)),
                  pl.BlockSpec((GMLP_CHUNK, GMLP_GROUPS), lambda i, j: (0, 0))],
        out_specs=out_specs, out_shape=out_shape,
        compiler_params=_params("parallel", "parallel"),
        name="gmlp",
    )(zcb, zcb, ln_g.reshape(1, w), ln_b.reshape(1, w), ws, bs.T)
    return (outs[0], outs[1]) if want_rows else (outs[0], None)


def _mixers(h, l, p, caches, bias):
    b, t, d = h.shape
    m = b * t
    hf = h.reshape(m, d)
    bw = BRANCH_WIDTH
    w_in_t = p['w_in_t']
    lam_init = 0.8 - 0.6 * math.exp(-0.3 * l)

    qkv = _matmul_nt(hf, w_in_t, l, 0, OFF_DF, [F32, BF16] if caches is None else [F32])
    z_qkv = qkv[0].reshape(b, t, OFF_DF)
    zcb = _matmul_nt(hf, w_in_t, l, OFF_CX, 4 * bw, [F32])[0].reshape(b, t, 4 * bw)
    ka = z_qkv[:, :, 1 * bw:2 * bw]
    va = z_qkv[:, :, 2 * bw:3 * bw]
    kd = z_qkv[:, :, 4 * bw:5 * bw]
    vd = z_qkv[:, :, 5 * bw:6 * bw]

    if caches is None:
        logf_t, cum, _ = _logf(hf, b, w_in_t, l, p['b_forget'][l], None)
        tq = bias.shape[2]
        z_bf = qkv[1].reshape(b, t, OFF_DF)
        o_a = _attn_a_prompt(z_bf, bias, p['lambda_qk'][l], p['diff_norm_g'][l], tq=tq, lam_init=lam_init)
        o_d = _attn_d_prompt(z_bf, cum, tq=tq)
        conv_buf = jnp.zeros((b, CONV_WIDTH - 1, bw), F32)
        lru_h0 = jnp.zeros((b, bw), F32)
    else:
        a_k_t, a_v, d_k, d_v, d_logf_t, state_conv, state_lru = caches
        logf_t, cum_new, cum_past = _logf(hf, b, w_in_t, l, p['b_forget'][l], d_logf_t)
        logf_t = logf_t[:, :, :t]
        cq = cum_new[:, :, :t].reshape(b, N_HEADS, t, 1)
        o_a = _attn_a_sample(z_qkv, a_k_t, a_v, l, bias[0], bias[1], p['lambda_qk'][l], p['diff_norm_g'][l],
                             lam_init=lam_init)
        cum_past = cum_past.transpose(0, 2, 1).reshape(b, 1, -1)
        o_d = _attn_d_sample(z_qkv, d_k, d_v, l, cq, cum_past, cum_new)
        conv_buf, lru_h0 = state_conv[l], state_lru[l]

    o_c, new_buf, h_last = _lru(zcb, conv_buf, lru_h0, p['conv_w'][l], p['conv_b'][l], p['w_rgate'][l],
                                p['b_rgate'][l], p['w_igate'][l], p['b_igate'][l], p['lru_lambda'][l])
    o_b, v_rows = _gmlp(zcb, p['gmlp_ln_g'][l], p['gmlp_ln_b'][l], p['gmlp_ws'][l], p['gmlp_bs'][l],
                        want_rows=caches is not None)

    y = _merge(hf, [o.reshape(m, bw) for o in (o_a, o_b, o_c, o_d)], p['bf16_weights']['gate', l],
               p['w_out'], l)
    states = (ka.reshape(b, t, N_HEADS, 2, HEAD_DIM_A), va.reshape(b, t, N_HEADS, HEAD_W),
              kd.reshape(b, t, N_HEADS, HEAD_W), vd.reshape(b, t, N_HEADS, HEAD_W),
              logf_t.transpose(0, 2, 1), new_buf, h_last.reshape(b, bw), v_rows)
    return y, states


def _ffn(h, l, j, p):
    b, t, d = h.shape
    idx = l * 2 + j
    x = h.reshape(b * t, d)
    bf16_weights = p['bf16_weights']
    if ('down', idx) in bf16_weights:
        mid = _ffn_up(x, p['w_ffn_gate'], p['w_ffn_up'], idx)
    else:
        mid, w_down = _ffn_up(x, p['w_ffn_gate'], p['w_ffn_up'], idx, w_down=p['w_ffn_down'])
        bf16_weights['down', idx] = w_down[None]
    w_down = bf16_weights['down', idx]
    if ('gate', l) in bf16_weights:
        y = _matmul(mid, w_down, 0, F32, tm_target=1024, tn_target=256)
    else:
        y, w_gate = _matmul(mid, w_down, 0, F32, tm_target=1024, tn_target=256,
                            cast_rows=(p['w_in_t'], l, OFF_GATE, N_BRANCH * d))
        bf16_weights['gate', l] = w_gate[None]
    return y.reshape(b, t, d)


def _run_group(x, mods, p, caches, bias):
    depth = len(mods)
    states = []
    _, h = _norm(x, pre=(mods[0], p['g_pre'][0], 0))
    for l in range(depth):
        mod, g_pre, g_post = mods[l], p['g_pre'][l], p['g_post'][l]
        x, h = _norm(x, _ffn(h, l, 0, p), post=(mod, g_post, 0, 0.5), pre=(mod, g_pre, 1))
        y, st = _mixers(h, l, p, caches, bias)
        states.append(st)
        x, h = _norm(x, y.reshape(x.shape), post=(mod, g_post, 1, 1.0), pre=(mod, g_pre, 2))
        nxt = (mods[l + 1], p['g_pre'][l + 1], 0) if l + 1 < depth else None
        x, h = _norm(x, _ffn(h, l, 1, p), post=(mod, g_post, 2, 0.5), pre=nxt)
    return x, states


def kernel(x_prompt, x_sample, c_prompt, c_sample, cache_a_k, cache_a_v, cache_d_k, cache_d_v, cache_d_logf, state_conv, state_lru, w_ada, b_ada, g_pre, g_post, w_ffn_gate, w_ffn_up, w_ffn_down, w_in, w_out, rel_bias, lambda_qk, diff_norm_g, gmlp_ln_g, gmlp_ln_b, gmlp_ws, gmlp_bs, conv_w, conv_b, w_rgate, b_rgate, w_igate, b_igate, lru_lambda, b_forget):
    depth, d_model, _ = w_in.shape
    bp, seq, _ = x_prompt.shape
    bs, dec_seq, _ = x_sample.shape
    past = cache_a_k.shape[2]
    d_ff = w_ffn_gate.shape[-1]
    p = {'g_pre': g_pre, 'g_post': g_post,
         'w_ffn_gate': w_ffn_gate.reshape(depth * 2, d_model, d_ff),
         'w_ffn_up': w_ffn_up.reshape(depth * 2, d_model, d_ff),
         'w_ffn_down': w_ffn_down.reshape(depth * 2, d_ff, d_model),
         'w_in_t': jnp.swapaxes(w_in, 1, 2),
         'bf16_weights': {},
         'w_out': w_out.reshape(depth * N_BRANCH, BRANCH_WIDTH, d_model),
         'lambda_qk': lambda_qk, 'diff_norm_g': diff_norm_g, 'gmlp_ln_g': gmlp_ln_g,
         'gmlp_ln_b': gmlp_ln_b, 'gmlp_ws': gmlp_ws, 'gmlp_bs': gmlp_bs, 'conv_w': conv_w,
         'conv_b': conv_b, 'w_rgate': w_rgate, 'b_rgate': b_rgate, 'w_igate': w_igate,
         'b_igate': b_igate, 'lru_lambda': lru_lambda, 'b_forget': b_forget}

    mod = _ada(jnp.concatenate([c_prompt, c_sample], axis=0), w_ada, b_ada)
    mod = mod.reshape(depth, bp + bs, 3 * N_SUB, d_model)
    mods_p = [mod[l, :bp] for l in range(depth)]
    mods_s = [mod[l, bp:] for l in range(depth)]

    tq = _tile(seq, 256, LANE)
    assert tq + 1 >= T5_SATURATION and tq % CHUNK == 0
    bias_p = _bias_tiles(rel_bias, tq, tq, [(0, 0), (tq, 0), None], True, seq)
    xp, st_p = _run_group(x_prompt, mods_p, p, None, bias_p)

    bias_s = (_bias_tiles(rel_bias, dec_seq, past, [(past, 0)], False, past + dec_seq),
              _bias_tiles(rel_bias, dec_seq, LANE, [(past, past)], False, past + dec_seq))
    caches = (cache_a_k.transpose(0, 1, 3, 4, 5, 2).reshape(depth, bs, N_HEADS, HEAD_W, past),
              cache_a_v, cache_d_k, cache_d_v,
              cache_d_logf.transpose(0, 1, 3, 2), state_conv, state_lru)
    xs, st_s = _run_group(x_sample, mods_s, p, caches, bias_s)

    np_ = [jnp.stack([st_p[l][j] for l in range(depth)]) for j in range(7)]
    ns_ = [jnp.stack([st_s[l][j] for l in range(depth)]) for j in range(8)]
    return (xp, xs, *np_, *ns_)
```

```python
import functools
import math

import jax
import jax.numpy as jnp
from jax import lax
from jax.experimental import pallas as pl
from jax.experimental.pallas import tpu as pltpu

F32 = jnp.float32
BF16 = jnp.bfloat16

LANE = 128
SUBLANE = 8
VMEM_LIMIT_BYTES = 56 * 1024 * 1024

CHUNK = 64
CHUNK_SHIFT = 6
N_HEADS = 8
HEAD_W = 128
HEAD_DIM_A = 64
N_BRANCH = 4
BRANCH_WIDTH = 1024
LRU_BLOCKS = 8
LRU_C = 8.0
CONV_WIDTH = 4
GMLP_GROUPS = 8
GMLP_CHUNK = 128
NUM_BUCKETS = 32
N_SUB = 3
EPS = 1e-6
NEG_INF = -1e30
LOG2E = 1.0 / math.log(2.0)
T5_LARGE_STARTS = (12, 16, 23, 32, 46, 64, 91)
T5_SATURATION = 91
T5_FAR_BUCKET = NUM_BUCKETS // 2 - 1

OFF_DF = 6 * BRANCH_WIDTH
OFF_CX = OFF_DF + N_HEADS
OFF_GATE = OFF_CX + 4 * BRANCH_WIDTH
REST_CX, REST_CG, REST_BU, REST_BV = 0, 1, 2, 3

NT = (((1,), (1,)), ((), ()))
TN = (((0,), (0,)), ((), ()))


def _params(*sem):
    return pltpu.CompilerParams(dimension_semantics=sem, vmem_limit_bytes=VMEM_LIMIT_BYTES)


def _tile(n, target, mult):
    best = None
    for t in range(mult, min(n, target) + 1, mult):
        if n % t == 0:
            best = t
    return n if best is None else best


def _resident(shape, index_map):
    return pl.BlockSpec(shape, index_map, pipeline_mode=pl.Buffered(1))


def _gelu(x):
    return x * (0.5 * (1.0 + jnp.tanh(math.sqrt(2.0 / math.pi) * (x + 0.044715 * (x * x * x)))))


def _rms(x, g):
    return x * lax.rsqrt(jnp.mean(x * x, axis=-1, keepdims=True) + EPS) * g


def _pad_rows(x, rows):
    if x.shape[0] == rows:
        return x
    return jnp.concatenate([x, jnp.zeros((rows - x.shape[0],) + x.shape[1:], x.dtype)], axis=0)


def _lane_cumsum(x):
    n = x.shape[-1]
    lane = lax.broadcasted_iota(jnp.int32, x.shape, x.ndim - 1)
    s = 1
    while s < n:
        x = x + jnp.where(lane >= s, pltpu.roll(x, s, x.ndim - 1), 0.0)
        s *= 2
    return x


def _ada_kernel(c_ref, w_ref, b_ref, o_ref):
    c = c_ref[...]
    a = (c * jax.nn.sigmoid(c)).astype(BF16)
    o_ref[0] = jnp.dot(a, w_ref[0].astype(BF16), preferred_element_type=F32) + b_ref[0]


def _ada(c_all, w_ada, b_ada):
    depth, d, n = w_ada.shape
    bt = c_all.shape[0]
    tn = _tile(n, 1024, LANE)
    return pl.pallas_call(
        _ada_kernel,
        grid=(depth, n // tn),
        in_specs=[pl.BlockSpec((bt, d), lambda l, j: (0, 0)),
                  pl.BlockSpec((1, d, tn), lambda l, j: (l, 0, j)),
                  pl.BlockSpec((1, 1, tn), lambda l, j: (l, 0, j))],
        out_specs=pl.BlockSpec((1, bt, tn), lambda l, j: (l, 0, j)),
        out_shape=jax.ShapeDtypeStruct((depth, bt, n), F32),
        compiler_params=_params("parallel", "parallel"),
        name="ada",
    )(c_all, w_ada, b_ada.reshape(depth, 1, n))


def _norm_kernel(*refs, k_post, coef, k_pre):
    refs = list(refs)
    x_ref = refs.pop(0)
    x = x_ref[0]
    if k_post is not None:
        y_ref, mpost_ref, gpost_ref = refs.pop(0), refs.pop(0), refs.pop(0)
    if k_pre is not None:
        mpre_ref, gpre_ref = refs.pop(0), refs.pop(0)
    if k_post is not None:
        xo_ref = refs.pop(0)
        m = mpost_ref[0]
        gate = m[3 * k_post + 2:3 * k_post + 3]
        x = x + (coef * gate) * _rms(y_ref[0], gpost_ref[k_post:k_post + 1, :])
        xo_ref[0] = x
    if k_pre is not None:
        ho_ref = refs.pop(0)
        m = mpre_ref[0]
        shift = m[3 * k_pre:3 * k_pre + 1]
        scale = m[3 * k_pre + 1:3 * k_pre + 2]
        ho_ref[0] = (_rms(x, gpre_ref[k_pre:k_pre + 1, :]) * (1.0 + scale) + shift).astype(BF16)


def _norm(x, y=None, post=None, pre=None):
    b, t, d = x.shape
    tt = _tile(t, 256, SUBLANE)
    row = pl.BlockSpec((1, tt, d), lambda i, j: (i, j, 0))
    mod_spec = pl.BlockSpec((1, 3 * N_SUB, d), lambda i, j: (i, 0, 0))
    g_spec = pl.BlockSpec((N_SUB, d), lambda i, j: (0, 0))
    args, in_specs, out_shape, out_specs = [x], [row], [], []
    if post is not None:
        args += [y, post[0], post[1]]
        in_specs += [row, mod_spec, g_spec]
        out_shape.append(jax.ShapeDtypeStruct((b, t, d), F32))
        out_specs.append(row)
    if pre is not None:
        args += [pre[0], pre[1]]
        in_specs += [mod_spec, g_spec]
        out_shape.append(jax.ShapeDtypeStruct((b, t, d), BF16))
        out_specs.append(row)
    outs = pl.pallas_call(
        functools.partial(_norm_kernel, k_post=None if post is None else post[2],
                          coef=None if post is None else post[3],
                          k_pre=None if pre is None else pre[2]),
        grid=(b, t // tt), in_specs=in_specs, out_specs=out_specs, out_shape=out_shape,
        compiler_params=_params("parallel", "parallel"),
        name="norm",
    )(*args)
    outs = list(outs)
    x_new = outs.pop(0) if post is not None else None
    h = outs.pop(0) if pre is not None else None
    return x_new, h


def _mm_kernel(x_ref, w_ref, *refs):
    if len(refs) == 3:
        refs[2][...] = refs[0][...].astype(BF16)
    o_ref = refs[-2] if len(refs) == 3 else refs[0]
    o_ref[...] = jnp.dot(x_ref[...], w_ref[0].astype(BF16),
                         preferred_element_type=F32).astype(o_ref.dtype)


def _matmul(x, w, l, out_dtype, tm_target, tn_target, cast_rows=None):
    m, k = x.shape
    n = w.shape[-1]
    tm = _tile(m, tm_target, 16)
    tn = _tile(n, tn_target, LANE)
    nj = n // tn
    args = [x, w]
    in_specs = [_resident((tm, k), lambda i, j: (i, 0)),
                pl.BlockSpec((1, k, tn), lambda i, j: (l, 0, j))]
    out_specs = [pl.BlockSpec((tm, tn), lambda i, j: (i, j))]
    out_shape = [jax.ShapeDtypeStruct((m, n), out_dtype)]
    if cast_rows is not None:
        wt, layer, row0, nrows = cast_rows
        slab = nrows // ((m // tm) * nj)
        assert slab * (m // tm) * nj == nrows and slab % 16 == 0
        d = wt.shape[-1]
        args.append(wt)
        in_specs.append(pl.BlockSpec(
            (None, pl.Element(slab), pl.Element(d)),
            lambda i, j: (layer, pl.multiple_of(row0 + (i * nj + j) * slab, SUBLANE), 0)))
        out_specs.append(pl.BlockSpec((slab, d), lambda i, j: (i * nj + j, 0)))
        out_shape.append(jax.ShapeDtypeStruct((nrows, d), BF16))
    outs = pl.pallas_call(
        _mm_kernel,
        grid=(m // tm, nj),
        in_specs=in_specs, out_specs=out_specs, out_shape=out_shape,
        compiler_params=_params("parallel", "arbitrary"),
        name="matmul",
    )(*args)
    return outs if cast_rows is not None else outs[0]


def _mm_nt_kernel(x_ref, w_ref, *o_refs):
    z = lax.dot_general(x_ref[...], w_ref[...].astype(BF16), NT, preferred_element_type=F32)
    for o_ref in o_refs:
        o_ref[...] = z.astype(o_ref.dtype)


def _matmul_nt(x, wt, l, row0, nrows, out_dtypes, tm_target=2048, tn_target=512):
    m, k = x.shape
    tm = _tile(m, tm_target, 16)
    tn = _tile(nrows, tn_target, LANE)
    return pl.pallas_call(
        _mm_nt_kernel,
        grid=(m // tm, nrows // tn),
        in_specs=[_resident((tm, k), lambda i, j: (i, 0)),
                  pl.BlockSpec((None, pl.Element(tn), pl.Element(k)),
                               lambda i, j: (l, pl.multiple_of(row0 + j * tn, SUBLANE), 0))],
        out_specs=[pl.BlockSpec((tm, tn), lambda i, j: (i, j)) for _ in out_dtypes],
        out_shape=[jax.ShapeDtypeStruct((m, nrows), dt) for dt in out_dtypes],
        compiler_params=_params("parallel", "arbitrary"),
        name="matmul_nt",
    )(x, wt)


def _ffn_up_kernel(x_ref, wg_ref, wu_ref, *refs):
    if len(refs) == 3:
        refs[2][...] = refs[0][0].astype(BF16)
    o_ref = refs[-2] if len(refs) == 3 else refs[0]
    x = x_ref[...]
    g = jnp.dot(x, wg_ref[0].astype(BF16), preferred_element_type=F32)
    u = jnp.dot(x, wu_ref[0].astype(BF16), preferred_element_type=F32)
    o_ref[...] = ((g * jax.nn.sigmoid(g)) * u).astype(o_ref.dtype)


def _ffn_up(x, wg, wu, l, w_down=None):
    m, k = x.shape
    n = wg.shape[-1]
    tm = _tile(m, 2048, 16)
    tn = _tile(n, 256, LANE)
    nj = n // tn
    w_spec = pl.BlockSpec((1, k, tn), lambda i, j: (l, 0, j))
    args = [x, wg, wu]
    in_specs = [_resident((tm, k), lambda i, j: (i, 0)), w_spec, w_spec]
    out_specs = [pl.BlockSpec((tm, tn), lambda i, j: (i, j))]
    out_shape = [jax.ShapeDtypeStruct((m, n), BF16)]
    if w_down is not None:
        d = w_down.shape[-1]
        slab = n // ((m // tm) * nj)
        assert slab * (m // tm) * nj == n and slab % 16 == 0
        args.append(w_down)
        in_specs.append(pl.BlockSpec((1, slab, d), lambda i, j: (l, i * nj + j, 0)))
        out_specs.append(pl.BlockSpec((slab, d), lambda i, j: (i * nj + j, 0)))
        out_shape.append(jax.ShapeDtypeStruct((n, d), BF16))
    outs = pl.pallas_call(
        _ffn_up_kernel,
        grid=(m // tm, nj),
        in_specs=in_specs, out_specs=out_specs, out_shape=out_shape,
        compiler_params=_params("parallel", "arbitrary"),
        name="ffn_up",
    )(*args)
    return outs if w_down is not None else outs[0]


def _merge_kernel(h_ref, oa_ref, ob_ref, oc_ref, od_ref, wg0, wg1, wg2, wg3, wo0, wo1, wo2, wo3, y_ref):
    h = h_ref[...]
    y = None
    for o_ref, wg, wo in ((oa_ref, wg0, wo0), (ob_ref, wg1, wo1), (oc_ref, wg2, wo2), (od_ref, wg3, wo3)):
        gate = jax.nn.sigmoid(lax.dot_general(h, wg[0], NT, preferred_element_type=F32))
        proj = jnp.dot(o_ref[...], wo[0].astype(BF16), preferred_element_type=F32)
        y = gate * proj if y is None else y + gate * proj
    y_ref[...] = y


def _merge(h, branches, w_gate, w_out, l):
    m, d = h.shape
    n = w_out.shape[-1]
    bw = branches[0].shape[-1]
    tm = _tile(m, 1024, 16)
    tn = _tile(n, 256, LANE)
    nj = n // tn

    def gate_spec(i):
        return pl.BlockSpec((1, tn, d), lambda a, j: (0, i * nj + j, 0))

    def out_spec(i):
        return pl.BlockSpec((1, bw, tn), lambda a, j: (l * N_BRANCH + i, 0, j))

    return pl.pallas_call(
        _merge_kernel,
        grid=(m // tm, nj),
        in_specs=[_resident((tm, d), lambda a, j: (a, 0))] + [_resident((tm, bw), lambda a, j: (a, 0))] * N_BRANCH
        + [gate_spec(i) for i in range(N_BRANCH)] + [out_spec(i) for i in range(N_BRANCH)],
        out_specs=pl.BlockSpec((tm, tn), lambda a, j: (a, j)),
        out_shape=jax.ShapeDtypeStruct((m, n), F32),
        compiler_params=_params("parallel", "arbitrary"),
        name="merge",
    )(h, *branches, w_gate, w_gate, w_gate, w_gate, w_out, w_out, w_out, w_out)


def _bias_kernel(tab_ref, o_ref, *, nq, nk, tiles, transposed, kv_end):
    h = pl.program_id(0)
    shape = (nk, nq) if transposed else (nq, nk)
    q_i = lax.broadcasted_iota(jnp.int32, shape, 1 if transposed else 0)
    k_i = lax.broadcasted_iota(jnp.int32, shape, 0 if transposed else 1)
    for c, tile in enumerate(tiles):
        for mp in range(2):
            if tile is None:
                val = jnp.full(shape, NEG_INF, F32)
            else:
                q0, k0 = tile
                q_pos, k_pos = q0 + q_i, k0 + k_i
                rel = k_pos - q_pos
                n = jnp.abs(rel)
                large = NUM_BUCKETS // 4
                for start in T5_LARGE_STARTS:
                    large = large + jnp.where(n >= start, 1, 0)
                bucket = jnp.where(rel > 0, NUM_BUCKETS // 2, 0) + jnp.where(n < NUM_BUCKETS // 4, n, large)
                val = jnp.zeros(shape, F32)
                for b in range(NUM_BUCKETS):
                    val = jnp.where(bucket == b, tab_ref[b * (2 * N_HEADS) + h * 2 + mp], val)
                val = val - tab_ref[T5_FAR_BUCKET * (2 * N_HEADS) + h * 2 + mp]
                visible = jnp.logical_and((k_pos >> CHUNK_SHIFT) <= (q_pos >> CHUNK_SHIFT), k_pos < kv_end)
                val = jnp.where(visible, val, NEG_INF)
            if transposed:
                o_ref[0, c, :, mp * nq:(mp + 1) * nq] = val
            else:
                o_ref[0, c, mp * nq:(mp + 1) * nq, :] = val


def _bias_tiles(rel_bias, nq, nk, tiles, transposed, kv_end):
    shape = (nk, 2 * nq) if transposed else (2 * nq, nk)
    return pl.pallas_call(
        functools.partial(_bias_kernel, nq=nq, nk=nk, tiles=tuple(tiles), transposed=transposed, kv_end=kv_end),
        grid=(N_HEADS,),
        in_specs=[pl.BlockSpec(memory_space=pltpu.SMEM)],
        out_specs=pl.BlockSpec((1, len(tiles)) + shape, lambda h: (h, 0, 0, 0)),
        out_shape=jax.ShapeDtypeStruct((N_HEADS, len(tiles)) + shape, F32),
        compiler_params=_params("parallel"),
        name="bias_tiles",
    )(rel_bias.reshape(-1))


def _lambda(lq_ref, lam_init):
    lq = lq_ref[...]
    return (jnp.exp(jnp.sum(lq[0:1] * lq[1:2], axis=-1, keepdims=True))
            - jnp.exp(jnp.sum(lq[2:3] * lq[3:4], axis=-1, keepdims=True)) + lam_init)


def _split_maps(q):
    lane = lax.broadcasted_iota(jnp.int32, q.shape, 1)
    q = q * (HEAD_DIM_A ** -0.5)
    return jnp.concatenate([jnp.where(lane < HEAD_DIM_A, q, 0.0),
                            jnp.where(lane >= HEAD_DIM_A, q, 0.0)], axis=0).astype(BF16)


def _flash_tile_t(carries, acc_sc, scores, values, query_terms=None):
    out, probs, alphas = [], [], []
    for g, ((m, l), s) in enumerate(zip(carries, scores)):
        if query_terms is None:
            m_new = jnp.maximum(m, jnp.max(s, axis=0, keepdims=True))
            p = jnp.exp(s - m_new)
            alpha = jnp.exp(m - m_new)
        else:
            m_new = jnp.maximum(m, jnp.max(s, axis=0, keepdims=True) + query_terms[g])
            p = jnp.exp2(s - (m_new - query_terms[g]))
            alpha = jnp.exp2(m - m_new)
        out.append((m_new, alpha * l + jnp.sum(p, axis=0, keepdims=True)))
        probs.append(p.astype(BF16))
        alphas.append(alpha)
    for g, (v, p, alpha) in enumerate(zip(values, probs, alphas)):
        acc_sc[g] = alpha * acc_sc[g] + lax.dot_general(v, p, TN, preferred_element_type=F32)
    return tuple(out)


def _flash_init(acc_ref):
    acc_ref[...] = jnp.zeros(acc_ref.shape, F32)
    nq = acc_ref.shape[-1]
    return jnp.full((1, nq), NEG_INF, F32), jnp.zeros((1, nq), F32)


HEADS_PER_STEP = 8


def _head(x, g):
    return x[:, g * HEAD_W:(g + 1) * HEAD_W]


def _attn_a_prompt_kernel(lq_ref, q_ref, k_ref, v_ref, bias_ref, g_ref, o_ref, acc_sc, *, tq, lam_init):
    qi = pl.program_id(2)
    heads = range(HEADS_PER_STEP)
    q_all = q_ref[0]
    q2 = [_split_maps(_head(q_all, g)) for g in heads]

    def tile(kt, carries, bias_tile):
        k0 = pl.multiple_of(kt * tq, tq)
        k_all = k_ref[0, pl.ds(k0, tq), :]
        v_all = v_ref[0, pl.ds(k0, tq), :]
        scores = [lax.dot_general(_head(k_all, g).astype(BF16), q2[g], NT, preferred_element_type=F32)
                  for g in heads]
        if bias_tile is not None:
            scores = [s + bias_ref[g, bias_tile] for g, s in enumerate(scores)]
        return _flash_tile_t(carries, acc_sc, scores, [_head(v_all, g).astype(BF16) for g in heads])

    carries = tile(qi, tuple(_flash_init(acc_sc.at[g]) for g in heads), 0)
    carries = tile(jnp.maximum(qi - 1, 0), carries, jnp.where(qi == 0, 2, 1))
    carries = lax.fori_loop(0, qi - 1, lambda kt, c: tile(kt, c, None), carries)
    lam = _lambda(lq_ref, lam_init)
    for g in heads:
        o = acc_sc[g] * (1.0 / carries[g][1])
        o = (o[:, :tq] - lam * o[:, tq:]).T
        o_ref[0, :, g * HEAD_W:(g + 1) * HEAD_W] = (_rms(o, g_ref[...]) * (1.0 - lam_init)).astype(o_ref.dtype)


def _attn_a_prompt(z_qkv, bias, lq, g_norm, *, tq, lam_init):
    b, t, _ = z_qkv.shape
    gw = HEADS_PER_STEP * HEAD_W
    nblk = BRANCH_WIDTH // gw
    return pl.pallas_call(
        functools.partial(_attn_a_prompt_kernel, tq=tq, lam_init=lam_init),
        grid=(b, nblk, t // tq),
        in_specs=[pl.BlockSpec((4, HEAD_DIM_A), lambda i, h, j: (0, 0)),
                  pl.BlockSpec((1, tq, gw), lambda i, h, j: (i, j, h)),
                  _resident((1, t, gw), lambda i, h, j: (i, 0, nblk + h)),
                  _resident((1, t, gw), lambda i, h, j: (i, 0, 2 * nblk + h)),
                  _resident((HEADS_PER_STEP, 3, tq, 2 * tq), lambda i, h, j: (h, 0, 0, 0)),
                  pl.BlockSpec((1, HEAD_W), lambda i, h, j: (0, 0))],
        out_specs=pl.BlockSpec((1, tq, gw), lambda i, h, j: (i, j, h)),
        out_shape=jax.ShapeDtypeStruct((b, t, N_HEADS * HEAD_W), BF16),
        scratch_shapes=[pltpu.VMEM((HEADS_PER_STEP, HEAD_W, 2 * tq), F32)],
        compiler_params=_params("parallel", "parallel", "arbitrary"),
        name="attn_a_prompt",
    )(lq, z_qkv, z_qkv, z_qkv, bias, g_norm.reshape(1, HEAD_W))


def _attn_d_prompt_kernel(q_ref, k_ref, v_ref, cq_ref, ck_ref, o_ref, ckb_sc, acc_sc, *, tq, t):
    h0 = pl.program_id(1) * HEADS_PER_STEP
    qi = pl.program_id(2)
    heads = range(HEADS_PER_STEP)

    @pl.when(qi == 0)
    def _():
        for g in heads:
            ck = ck_ref[0, pl.ds(h0 + g, 1), :] * LOG2E
            for c in range(t // LANE):
                ckb_sc[g, c * LANE:(c + 1) * LANE, :] = jnp.broadcast_to(
                    ck[:, c * LANE:(c + 1) * LANE], (LANE, LANE)).T

    q_all = q_ref[0]
    q = [_head(q_all, g).astype(BF16) for g in heads]
    cq = [cq_ref[0, pl.ds(h0 + g, 1), :] * LOG2E for g in heads]
    scale = HEAD_W ** -0.5 * LOG2E

    def tile(kt, carries, diagonal):
        k0 = pl.multiple_of(kt * tq, tq)
        k_all = k_ref[0, pl.ds(k0, tq), :]
        v_all = v_ref[0, pl.ds(k0, tq), :]
        scores = [lax.dot_general(_head(k_all, g).astype(BF16), q[g], NT, preferred_element_type=F32)
                  for g in heads]
        for g in heads:
            ckb = ckb_sc[g, pl.ds(k0, tq), :]
            s = scores[g] * scale - jnp.concatenate([ckb] * (tq // LANE), axis=1)
            if diagonal:
                key = lax.broadcasted_iota(jnp.int32, (tq, tq), 0)
                qry = lax.broadcasted_iota(jnp.int32, (tq, tq), 1)
                s = jnp.where(key <= qry, s, NEG_INF)
            scores[g] = s
        return _flash_tile_t(carries, acc_sc, scores, [_head(v_all, g).astype(BF16) for g in heads], cq)

    carries = tile(qi, tuple(_flash_init(acc_sc.at[g]) for g in heads), True)
    carries = lax.fori_loop(0, qi, lambda kt, c: tile(kt, c, False), carries)
    for g in heads:
        o = acc_sc[g] * (1.0 / carries[g][1])
        o_ref[0, :, g * HEAD_W:(g + 1) * HEAD_W] = o.T.astype(o_ref.dtype)


def _attn_d_prompt(z_qkv, cum, *, tq):
    b, t, _ = z_qkv.shape
    gw = HEADS_PER_STEP * HEAD_W
    nblk = BRANCH_WIDTH // gw
    return pl.pallas_call(
        functools.partial(_attn_d_prompt_kernel, tq=tq, t=t),
        grid=(b, nblk, t // tq),
        in_specs=[pl.BlockSpec((1, tq, gw), lambda i, h, j: (i, j, 3 * nblk + h)),
                  _resident((1, t, gw), lambda i, h, j: (i, 0, 4 * nblk + h)),
                  _resident((1, t, gw), lambda i, h, j: (i, 0, 5 * nblk + h)),
                  pl.BlockSpec((1, N_HEADS, tq), lambda i, h, j: (i, 0, j)),
                  pl.BlockSpec((1, N_HEADS, t), lambda i, h, j: (i, 0, 0))],
        out_specs=pl.BlockSpec((1, tq, gw), lambda i, h, j: (i, j, h)),
        out_shape=jax.ShapeDtypeStruct((b, t, N_HEADS * HEAD_W), BF16),
        scratch_shapes=[pltpu.VMEM((HEADS_PER_STEP, t, LANE), F32),
                        pltpu.VMEM((HEADS_PER_STEP, HEAD_W, tq), F32)],
        compiler_params=_params("parallel", "arbitrary", "arbitrary"),
        name="attn_d_prompt",
    )(z_qkv, z_qkv, z_qkv, cum, cum)


def _softmax_two(s_p, s_n, v_p, v_n):
    m = jnp.maximum(jnp.max(s_p, axis=-1, keepdims=True), jnp.max(s_n, axis=-1, keepdims=True))
    p_p = jnp.exp(s_p - m)
    p_n = jnp.exp(s_n - m)
    l = jnp.sum(p_p, axis=-1, keepdims=True) + jnp.sum(p_n, axis=-1, keepdims=True)
    acc = (jnp.dot(p_p.astype(BF16), v_p, preferred_element_type=F32)
           + jnp.dot(p_n.astype(BF16), v_n, preferred_element_type=F32))
    return acc / l


def _attn_a_sample_kernel(lq_ref, q_ref, kt_ref, vp_ref, kn_ref, vn_ref, bp_ref, bn_ref, g_ref, o_ref, *,
                          t, lam_init):
    lam = _lambda(lq_ref, lam_init)
    q_all, kn_all, vn_all = q_ref[0], kn_ref[0], vn_ref[0]
    for h in range(N_HEADS):
        q2 = _split_maps(_head(q_all, h))
        kn = _pad_rows(_head(kn_all, h), LANE).astype(BF16)
        vn = _pad_rows(_head(vn_all, h), LANE).astype(BF16)
        s_p = jnp.dot(q2, kt_ref[0, 0, h].astype(BF16), preferred_element_type=F32) + bp_ref[h, 0]
        s_n = lax.dot_general(q2, kn, NT, preferred_element_type=F32) + bn_ref[h, 0]
        o = _softmax_two(s_p, s_n, vp_ref[0, 0, :, h, :].astype(BF16), vn)
        o = o[:t] - lam * o[t:]
        o_ref[0, :, h * HEAD_W:(h + 1) * HEAD_W] = (_rms(o, g_ref[...]) * (1.0 - lam_init)).astype(o_ref.dtype)


def _attn_a_sample(z_qkv, k_cache_t, v_cache, l, bias_past, bias_new, lq, g_norm, *, lam_init):
    b, t, _ = z_qkv.shape
    past = v_cache.shape[2]
    bw = BRANCH_WIDTH
    return pl.pallas_call(
        functools.partial(_attn_a_sample_kernel, t=t, lam_init=lam_init),
        grid=(b,),
        in_specs=[pl.BlockSpec((4, HEAD_DIM_A), lambda i: (0, 0)),
                  pl.BlockSpec((1, t, bw), lambda i: (i, 0, 0)),
                  pl.BlockSpec((1, 1, N_HEADS, HEAD_W, past), lambda i: (l, i, 0, 0, 0)),
                  pl.BlockSpec((1, 1, past, N_HEADS, HEAD_W), lambda i: (l, i, 0, 0, 0)),
                  pl.BlockSpec((1, t, bw), lambda i: (i, 0, 1)),
                  pl.BlockSpec((1, t, bw), lambda i: (i, 0, 2)),
                  pl.BlockSpec((N_HEADS, 1, 2 * t, past), lambda i: (0, 0, 0, 0)),
                  pl.BlockSpec((N_HEADS, 1, 2 * t, LANE), lambda i: (0, 0, 0, 0)),
                  pl.BlockSpec((1, HEAD_W), lambda i: (0, 0))],
        out_specs=pl.BlockSpec((1, t, bw), lambda i: (i, 0, 0)),
        out_shape=jax.ShapeDtypeStruct((b, t, bw), BF16),
        compiler_params=_params("parallel"),
        name="attn_a_sample",
    )(lq, z_qkv, k_cache_t, v_cache, z_qkv, z_qkv, bias_past, bias_new, g_norm.reshape(1, HEAD_W))


def _attn_d_sample_kernel(q_ref, kp_ref, vp_ref, kn_ref, vn_ref, cq_ref, ckp_ref, ckn_ref, o_ref, *, t):
    heads = range(N_HEADS)
    past = kp_ref.shape[2]
    scale = HEAD_W ** -0.5
    q_all, kn_all, vn_all = q_ref[0], kn_ref[0], vn_ref[0]
    q = jnp.concatenate([_head(q_all, h) for h in heads], axis=0).astype(BF16)
    cq = jnp.concatenate([cq_ref[0, h] for h in heads], axis=0)
    k_all = kp_ref[0, 0].reshape(past * N_HEADS, HEAD_W).astype(BF16)
    v_all = vp_ref[0, 0].reshape(past * N_HEADS, HEAD_W).astype(BF16)
    s_p = lax.dot_general(q, k_all, NT, preferred_element_type=F32) * scale + cq - ckp_ref[0]
    row_head = jnp.concatenate([jnp.full((t, 1), h, jnp.int32) for h in heads], axis=0)
    col_head = lax.broadcasted_iota(jnp.int32, (1, past * N_HEADS), 1) & (N_HEADS - 1)
    s_p = jnp.where(row_head == col_head, s_p, NEG_INF)

    kn = [_pad_rows(_head(kn_all, h), LANE).astype(BF16) for h in heads]
    vn = [_pad_rows(_head(vn_all, h), LANE).astype(BF16) for h in heads]
    s_n = jnp.concatenate([lax.dot_general(q[h * t:(h + 1) * t], kn[h], NT, preferred_element_type=F32)
                           for h in heads], axis=0) * scale
    ckn = jnp.concatenate([jnp.broadcast_to(ckn_ref[0, h:h + 1, :], (t, LANE)) for h in heads], axis=0)
    qry = jnp.concatenate([lax.broadcasted_iota(jnp.int32, (t, LANE), 0)] * N_HEADS, axis=0)
    key = lax.broadcasted_iota(jnp.int32, (N_HEADS * t, LANE), 1)
    s_n = jnp.where(key <= qry, s_n + cq - ckn, NEG_INF)

    m = jnp.maximum(jnp.max(s_p, axis=-1, keepdims=True), jnp.max(s_n, axis=-1, keepdims=True))
    p_p = jnp.exp(s_p - m)
    p_n = jnp.exp(s_n - m)
    l_sum = jnp.sum(p_p, axis=-1, keepdims=True) + jnp.sum(p_n, axis=-1, keepdims=True)
    p_n = p_n.astype(BF16)
    acc = jnp.dot(p_p.astype(BF16), v_all, preferred_element_type=F32)
    acc = acc + jnp.concatenate([jnp.dot(p_n[h * t:(h + 1) * t], vn[h], preferred_element_type=F32)
                                 for h in heads], axis=0)
    o = acc / l_sum
    for h in heads:
        o_ref[0, :, h * HEAD_W:(h + 1) * HEAD_W] = o[h * t:(h + 1) * t].astype(o_ref.dtype)


def _attn_d_sample(z_qkv, k_cache, v_cache, l, cq, cum_past, cum_new):
    b, t, _ = z_qkv.shape
    past = k_cache.shape[2]
    bw = BRANCH_WIDTH
    assert N_HEADS & (N_HEADS - 1) == 0
    cache_spec = pl.BlockSpec((1, 1, past, N_HEADS, HEAD_W), lambda i: (l, i, 0, 0, 0))
    return pl.pallas_call(
        functools.partial(_attn_d_sample_kernel, t=t),
        grid=(b,),
        in_specs=[pl.BlockSpec((1, t, bw), lambda i: (i, 0, 3)),
                  cache_spec, cache_spec,
                  pl.BlockSpec((1, t, bw), lambda i: (i, 0, 4)),
                  pl.BlockSpec((1, t, bw), lambda i: (i, 0, 5)),
                  pl.BlockSpec((1, N_HEADS, t, 1), lambda i: (i, 0, 0, 0)),
                  pl.BlockSpec((1, 1, past * N_HEADS), lambda i: (i, 0, 0)),
                  pl.BlockSpec((1, N_HEADS, LANE), lambda i: (i, 0, 0))],
        out_specs=pl.BlockSpec((1, t, bw), lambda i: (i, 0, 0)),
        out_shape=jax.ShapeDtypeStruct((b, t, bw), BF16),
        compiler_params=_params("parallel"),
        name="attn_d_sample",
    )(z_qkv, k_cache, v_cache, z_qkv, z_qkv, cq, cum_past, cum_new)


def _logf_kernel(*refs, tt, has_past):
    refs = list(refs)
    h_ref, wf_ref, bf_ref = refs.pop(0), refs.pop(0), refs.pop(0)
    past_ref = refs.pop(0) if has_past else None
    logf_ref, cum_ref = refs.pop(0), refs.pop(0)
    cump_ref = refs.pop(0) if has_past else None
    carry_sc = refs.pop(0)

    @pl.when(pl.program_id(1) == 0)
    def _():
        if has_past:
            cump = _lane_cumsum(past_ref[0, 0])
            cump_ref[0] = cump
            carry_sc[...] = jnp.broadcast_to(cump[:, cump.shape[1] - 1:], carry_sc.shape)
        else:
            carry_sc[...] = jnp.zeros(carry_sc.shape, F32)

    ttp = logf_ref.shape[-1]
    hb = _pad_rows(h_ref[...], ttp)
    z = lax.dot_general(wf_ref[...].astype(BF16), hb, NT, preferred_element_type=F32) + bf_ref[...]
    logf = jnp.minimum(z, 0.0) - jnp.log1p(jnp.exp(-jnp.abs(z)))
    if tt < ttp:
        logf = jnp.where(lax.broadcasted_iota(jnp.int32, logf.shape, 1) < tt, logf, 0.0)
    logf_ref[0] = logf
    cum = _lane_cumsum(logf) + carry_sc[:, 0:1]
    cum_ref[0] = cum
    carry_sc[...] = jnp.broadcast_to(cum[:, ttp - 1:], carry_sc.shape)


def _logf(hf, b, w_in_t, l, b_forget, past_logf_t):
    m, d = hf.shape
    t = m // b
    tt = _tile(t, 1024, LANE) if t >= LANE else t
    ttp = max(tt, LANE)
    nt = t // tt
    has_past = past_logf_t is not None
    args = [hf, w_in_t, b_forget.reshape(N_HEADS, 1)]
    in_specs = [pl.BlockSpec((tt, d), lambda i, j: (i * nt + j, 0)),
                pl.BlockSpec((None, pl.Element(N_HEADS), pl.Element(d)), lambda i, j: (l, OFF_DF, 0)),
                pl.BlockSpec((N_HEADS, 1), lambda i, j: (0, 0))]
    out_shape = [jax.ShapeDtypeStruct((b, N_HEADS, nt * ttp), F32)] * 2
    out_specs = [pl.BlockSpec((1, N_HEADS, ttp), lambda i, j: (i, 0, j))] * 2
    if has_past:
        past = past_logf_t.shape[-1]
        args.append(past_logf_t)
        in_specs.append(pl.BlockSpec((1, 1, N_HEADS, past), lambda i, j: (l, i, 0, 0)))
        out_shape.append(jax.ShapeDtypeStruct((b, N_HEADS, past), F32))
        out_specs.append(pl.BlockSpec((1, N_HEADS, past), lambda i, j: (i, 0, 0)))
    outs = pl.pallas_call(
        functools.partial(_logf_kernel, tt=tt, has_past=has_past),
        grid=(b, nt), in_specs=in_specs, out_specs=out_specs, out_shape=out_shape,
        scratch_shapes=[pltpu.VMEM((N_HEADS, LANE), F32)],
        compiler_params=_params("parallel", "arbitrary"),
        name="logf_cumsum",
    )(*args)
    return outs[0], outs[1], (outs[2] if has_past else None)


def _lru_kernel(cx_ref, cg_ref, buf0_ref, h0_ref, cw_ref, cb_ref, wr_ref, br_ref, wi_ref, bi_ref,
                lam_ref, oc_ref, nbuf_ref, hl_ref, tail_sc, h_sc, *, tt):
    w = cx_ref.shape[-1]
    blk = w // LRU_BLOCKS
    pad = SUBLANE - (CONV_WIDTH - 1)

    @pl.when(pl.program_id(1) == 0)
    def _():
        tail_sc[0:pad, :] = jnp.zeros((pad, w), F32)
        tail_sc[pad:SUBLANE, :] = buf0_ref[0]
        h_sc[...] = h0_ref[0]

    full = jnp.concatenate([tail_sc[...], cx_ref[0]], axis=0)
    cw = cw_ref[...]
    xc = cb_ref[...]
    for j in range(CONV_WIDTH):
        xc = xc + full[pad + j:pad + j + tt] * cw[j:j + 1]
    tail_sc[...] = full[tt:tt + SUBLANE]
    nbuf_ref[0] = full[tt + pad:tt + SUBLANE]

    xb = xc.astype(BF16)

    def gate(w_ref, b_ref):
        parts = [jnp.dot(xb[:, g * blk:(g + 1) * blk], w_ref[g].astype(BF16), preferred_element_type=F32)
                 for g in range(LRU_BLOCKS)]
        return jax.nn.sigmoid(jnp.concatenate(parts, axis=1) + b_ref[...])

    r = gate(wr_ref, br_ref)
    i = gate(wi_ref, bi_ref)
    nl = -lam_ref[...]
    softplus = jnp.maximum(nl, 0.0) + jnp.log1p(jnp.exp(-jnp.abs(nl)))
    log_a = (-LRU_C * r) * softplus
    a = jnp.exp(log_a)
    u = jnp.sqrt(-jnp.tanh(log_a) * (a * a + 1.0)) * (i * xc)

    in_group = lax.broadcasted_iota(jnp.int32, (tt, w), 0) & (SUBLANE - 1)
    s = 1
    while s < SUBLANE:
        valid = in_group >= s
        u = jnp.where(valid, a * pltpu.roll(u, s, 0) + u, u)
        a = jnp.where(valid, a * pltpu.roll(a, s, 0), a)
        s *= 2
    h_prev = h_sc[...]
    groups = []
    for g in range(tt // SUBLANE):
        rows = slice(g * SUBLANE, (g + 1) * SUBLANE)
        groups.append(u[rows] + a[rows] * h_prev)
        h_prev = groups[-1][SUBLANE - 1:SUBLANE]
    h = jnp.concatenate(groups, axis=0)
    h_sc[...] = h_prev
    hl_ref[0] = h_prev
    oc_ref[0] = (h * _gelu(cg_ref[0])).astype(oc_ref.dtype)


def _lru(zcb, conv_buf, h0, conv_w, conv_b, w_r, b_r, w_i, b_i, lam):
    b, t, _ = zcb.shape
    w = BRANCH_WIDTH
    tt = _tile(t, 256, SUBLANE)
    vec = pl.BlockSpec((1, w), lambda i, j: (0, 0))
    mat = pl.BlockSpec((LRU_BLOCKS, w // LRU_BLOCKS, w // LRU_BLOCKS), lambda i, j: (0, 0, 0))
    return pl.pallas_call(
        functools.partial(_lru_kernel, tt=tt),
        grid=(b, t // tt),
        in_specs=[pl.BlockSpec((1, tt, w), lambda i, j: (i, j, REST_CX)),
                  pl.BlockSpec((1, tt, w), lambda i, j: (i, j, REST_CG)),
                  pl.BlockSpec((1, CONV_WIDTH - 1, w), lambda i, j: (i, 0, 0)),
                  pl.BlockSpec((1, 1, w), lambda i, j: (i, 0, 0)),
                  pl.BlockSpec((CONV_WIDTH, w), lambda i, j: (0, 0)),
                  vec, mat, vec, mat, vec, vec],
        out_specs=[pl.BlockSpec((1, tt, w), lambda i, j: (i, j, 0)),
                   pl.BlockSpec((1, CONV_WIDTH - 1, w), lambda i, j: (i, 0, 0)),
                   pl.BlockSpec((1, 1, w), lambda i, j: (i, 0, 0))],
        out_shape=[jax.ShapeDtypeStruct((b, t, w), BF16),
                   jax.ShapeDtypeStruct((b, CONV_WIDTH - 1, w), F32),
                   jax.ShapeDtypeStruct((b, 1, w), F32)],
        scratch_shapes=[pltpu.VMEM((SUBLANE, w), F32), pltpu.VMEM((1, w), F32)],
        compiler_params=_params("parallel", "arbitrary"),
        name="conv_lru",
    )(zcb, zcb, conv_buf, h0.reshape(b, 1, w), conv_w, conv_b.reshape(1, w), w_r, b_r.reshape(1, w),
      w_i, b_i.reshape(1, w), lam.reshape(1, w))


def _gmlp_kernel(zu_ref, zv_ref, lng_ref, lnb_ref, ws_ref, bst_ref, ob_ref, *vn_refs, ct):
    u = _gelu(zu_ref[0])
    v = _gelu(zv_ref[0])
    w = v.shape[-1]
    gw = w // GMLP_GROUPS
    d = v - jnp.mean(v, axis=-1, keepdims=True)
    vn = d * lax.rsqrt(jnp.mean(d * d, axis=-1, keepdims=True) + EPS) * lng_ref[...] + lnb_ref[...]
    if vn_refs:
        vn_refs[0][0] = vn
    vb = _pad_rows(vn, GMLP_CHUNK).astype(BF16)
    tril = (lax.broadcasted_iota(jnp.int32, (GMLP_CHUNK, GMLP_CHUNK), 0)
            >= lax.broadcasted_iota(jnp.int32, (GMLP_CHUNK, GMLP_CHUNK), 1))
    parts = []
    for g in range(GMLP_GROUPS):
        wg = jnp.where(tril, ws_ref[g], 0.0).astype(BF16)
        s = jnp.dot(wg, vb[:, g * gw:(g + 1) * gw], preferred_element_type=F32) + bst_ref[:, g:g + 1]
        parts.append(s[:ct])
    ob_ref[0] = (u * jnp.concatenate(parts, axis=1)).astype(ob_ref.dtype)


def _gmlp(zcb, ln_g, ln_b, ws, bs, want_rows):
    b, t, _ = zcb.shape
    w = BRANCH_WIDTH
    ct = min(t, GMLP_CHUNK)
    vec = pl.BlockSpec((1, w), lambda i, j: (0, 0))
    row_out = pl.BlockSpec((1, ct, w), lambda i, j: (i, j, 0))
    out_shape = [jax.ShapeDtypeStruct((b, t, w), BF16)]
    out_specs = [row_out]
    if want_rows:
        out_shape.append(jax.ShapeDtypeStruct((b, t, w), F32))
        out_specs.append(row_out)
    outs = pl.pallas_call(
        functools.partial(_gmlp_kernel, ct=ct),
        grid=(b, t // ct),
        in_specs=[pl.BlockSpec((1, ct, w), lambda i, j: (i, j, REST_BU)),
                  pl.BlockSpec((1, ct, w), lambda i, j: (i, j, REST_BV)),
                  vec, vec,
                  pl.BlockSpec((GMLP_GROUPS, GMLP_CHUNK, GMLP_CHUNK), lambda i, j: (0, 0, 0)),
                  pl.BlockSpec((GMLP_CHUNK, GMLP_GROUPS), lambda i, j: (0, 0))],
        out_specs=out_specs, out_shape=out_shape,
        compiler_params=_params("parallel", "parallel"),
        name="gmlp",
    )(zcb, zcb, ln_g.reshape(1, w), ln_b.reshape(1, w), ws, bs.T)
    return (outs[0], outs[1]) if want_rows else (outs[0], None)


def _mixers(h, l, p, caches, bias):
    b, t, d = h.shape
    m = b * t
    hf = h.reshape(m, d)
    bw = BRANCH_WIDTH
    w_in_t = p['w_in_t']
    lam_init = 0.8 - 0.6 * math.exp(-0.3 * l)

    qkv = _matmul_nt(hf, w_in_t, l, 0, OFF_DF, [F32, BF16] if caches is None else [F32])
    z_qkv = qkv[0].reshape(b, t, OFF_DF)
    zcb = _matmul_nt(hf, w_in_t, l, OFF_CX, 4 * bw, [F32])[0].reshape(b, t, 4 * bw)
    ka = z_qkv[:, :, 1 * bw:2 * bw]
    va = z_qkv[:, :, 2 * bw:3 * bw]
    kd = z_qkv[:, :, 4 * bw:5 * bw]
    vd = z_qkv[:, :, 5 * bw:6 * bw]

    if caches is None:
        logf_t, cum, _ = _logf(hf, b, w_in_t, l, p['b_forget'][l], None)
        tq = bias.shape[2]
        z_bf = qkv[1].reshape(b, t, OFF_DF)
        o_a = _attn_a_prompt(z_bf, bias, p['lambda_qk'][l], p['diff_norm_g'][l], tq=tq, lam_init=lam_init)
        o_d = _attn_d_prompt(z_bf, cum, tq=tq)
        conv_buf = jnp.zeros((b, CONV_WIDTH - 1, bw), F32)
        lru_h0 = jnp.zeros((b, bw), F32)
    else:
        a_k_t, a_v, d_k, d_v, d_logf_t, state_conv, state_lru = caches
        logf_t, cum_new, cum_past = _logf(hf, b, w_in_t, l, p['b_forget'][l], d_logf_t)
        logf_t = logf_t[:, :, :t]
        cq = cum_new[:, :, :t].reshape(b, N_HEADS, t, 1)
        o_a = _attn_a_sample(z_qkv, a_k_t, a_v, l, bias[0], bias[1], p['lambda_qk'][l], p['diff_norm_g'][l],
                             lam_init=lam_init)
        cum_past = cum_past.transpose(0, 2, 1).reshape(b, 1, -1)
        o_d = _attn_d_sample(z_qkv, d_k, d_v, l, cq, cum_past, cum_new)
        conv_buf, lru_h0 = state_conv[l], state_lru[l]

    o_c, new_buf, h_last = _lru(zcb, conv_buf, lru_h0, p['conv_w'][l], p['conv_b'][l], p['w_rgate'][l],
                                p['b_rgate'][l], p['w_igate'][l], p['b_igate'][l], p['lru_lambda'][l])
    o_b, v_rows = _gmlp(zcb, p['gmlp_ln_g'][l], p['gmlp_ln_b'][l], p['gmlp_ws'][l], p['gmlp_bs'][l],
                        want_rows=caches is not None)

    y = _merge(hf, [o.reshape(m, bw) for o in (o_a, o_b, o_c, o_d)], p['bf16_weights']['gate', l],
               p['w_out'], l)
    states = (ka.reshape(b, t, N_HEADS, 2, HEAD_DIM_A), va.reshape(b, t, N_HEADS, HEAD_W),
              kd.reshape(b, t, N_HEADS, HEAD_W), vd.reshape(b, t, N_HEADS, HEAD_W),
              logf_t.transpose(0, 2, 1), new_buf, h_last.reshape(b, bw), v_rows)
    return y, states


def _ffn(h, l, j, p):
    b, t, d = h.shape
    idx = l * 2 + j
    x = h.reshape(b * t, d)
    bf16_weights = p['bf16_weights']
    if ('down', idx) in bf16_weights:
        mid = _ffn_up(x, p['w_ffn_gate'], p['w_ffn_up'], idx)
    else:
        mid, w_down = _ffn_up(x, p['w_ffn_gate'], p['w_ffn_up'], idx, w_down=p['w_ffn_down'])
        bf16_weights['down', idx] = w_down[None]
    w_down = bf16_weights['down', idx]
    if ('gate', l) in bf16_weights:
        y = _matmul(mid, w_down, 0, F32, tm_target=1024, tn_target=512 if x.shape[0] <= 1024 else 256)
    else:
        y, w_gate = _matmul(mid, w_down, 0, F32, tm_target=1024, tn_target=256,
                            cast_rows=(p['w_in_t'], l, OFF_GATE, N_BRANCH * d))
        bf16_weights['gate', l] = w_gate[None]
    return y.reshape(b, t, d)


def _run_group(x, mods, p, caches, bias):
    depth = len(mods)
    states = []
    _, h = _norm(x, pre=(mods[0], p['g_pre'][0], 0))
    for l in range(depth):
        mod, g_pre, g_post = mods[l], p['g_pre'][l], p['g_post'][l]
        x, h = _norm(x, _ffn(h, l, 0, p), post=(mod, g_post, 0, 0.5), pre=(mod, g_pre, 1))
        y, st = _mixers(h, l, p, caches, bias)
        states.append(st)
        x, h = _norm(x, y.reshape(x.shape), post=(mod, g_post, 1, 1.0), pre=(mod, g_pre, 2))
        nxt = (mods[l + 1], p['g_pre'][l + 1], 0) if l + 1 < depth else None
        x, h = _norm(x, _ffn(h, l, 1, p), post=(mod, g_post, 2, 0.5), pre=nxt)
    return x, states


def kernel(x_prompt, x_sample, c_prompt, c_sample, cache_a_k, cache_a_v, cache_d_k, cache_d_v, cache_d_logf, state_conv, state_lru, w_ada, b_ada, g_pre, g_post, w_ffn_gate, w_ffn_up, w_ffn_down, w_in, w_out, rel_bias, lambda_qk, diff_norm_g, gmlp_ln_g, gmlp_ln_b, gmlp_ws, gmlp_bs, conv_w, conv_b, w_rgate, b_rgate, w_igate, b_igate, lru_lambda, b_forget):
    depth, d_model, _ = w_in.shape
    bp, seq, _ = x_prompt.shape
    bs, dec_seq, _ = x_sample.shape
    past = cache_a_k.shape[2]
    d_ff = w_ffn_gate.shape[-1]
    p = {'g_pre': g_pre, 'g_post': g_post,
         'w_ffn_gate': w_ffn_gate.reshape(depth * 2, d_model, d_ff),
         'w_ffn_up': w_ffn_up.reshape(depth * 2, d_model, d_ff),
         'w_ffn_down': w_ffn_down.reshape(depth * 2, d_ff, d_model),
         'w_in_t': jnp.swapaxes(w_in, 1, 2),
         'bf16_weights': {},
         'w_out': w_out.reshape(depth * N_BRANCH, BRANCH_WIDTH, d_model),
         'lambda_qk': lambda_qk, 'diff_norm_g': diff_norm_g, 'gmlp_ln_g': gmlp_ln_g,
         'gmlp_ln_b': gmlp_ln_b, 'gmlp_ws': gmlp_ws, 'gmlp_bs': gmlp_bs, 'conv_w': conv_w,
         'conv_b': conv_b, 'w_rgate': w_rgate, 'b_rgate': b_rgate, 'w_igate': w_igate,
         'b_igate': b_igate, 'lru_lambda': lru_lambda, 'b_forget': b_forget}

    mod = _ada(jnp.concatenate([c_prompt, c_sample], axis=0), w_ada, b_ada)
    mod = mod.reshape(depth, bp + bs, 3 * N_SUB, d_model)
    mods_p = [mod[l, :bp] for l in range(depth)]
    mods_s = [mod[l, bp:] for l in range(depth)]

    tq = _tile(seq, 256, LANE)
    assert tq + 1 >= T5_SATURATION and tq % CHUNK == 0
    bias_p = _bias_tiles(rel_bias, tq, tq, [(0, 0), (tq, 0), None], True, seq)
    xp, st_p = _run_group(x_prompt, mods_p, p, None, bias_p)

    bias_s = (_bias_tiles(rel_bias, dec_seq, past, [(past, 0)], False, past + dec_seq),
              _bias_tiles(rel_bias, dec_seq, LANE, [(past, past)], False, past + dec_seq))
    caches = (cache_a_k.transpose(0, 1, 3, 4, 5, 2).reshape(depth, bs, N_HEADS, HEAD_W, past),
              cache_a_v, cache_d_k, cache_d_v,
              cache_d_logf.transpose(0, 1, 3, 2), state_conv, state_lru)
    xs, st_s = _run_group(x_sample, mods_s, p, caches, bias_s)

    np_ = [jnp.stack([st_p[l][j] for l in range(depth)]) for j in range(7)]
    ns_ = [jnp.stack([st_s[l][j] for l in range(depth)]) for j in range(8)]
    return (xp, xs, *np_, *ns_)
```

```python
import functools
import math

import jax
import jax.numpy as jnp
from jax import lax
from jax.experimental import pallas as pl
from jax.experimental.pallas import tpu as pltpu

F32 = jnp.float32
BF16 = jnp.bfloat16

LANE = 128
SUBLANE = 8
VMEM_LIMIT_BYTES = 56 * 1024 * 1024

CHUNK = 64
CHUNK_SHIFT = 6
N_HEADS = 8
HEAD_W = 128
HEAD_DIM_A = 64
N_BRANCH = 4
BRANCH_WIDTH = 1024
LRU_BLOCKS = 8
LRU_C = 8.0
CONV_WIDTH = 4
GMLP_GROUPS = 8
GMLP_CHUNK = 128
NUM_BUCKETS = 32
N_SUB = 3
EPS = 1e-6
NEG_INF = -1e30
LOG2E = 1.0 / math.log(2.0)
T5_LARGE_STARTS = (12, 16, 23, 32, 46, 64, 91)
T5_SATURATION = 91
T5_FAR_BUCKET = NUM_BUCKETS // 2 - 1

OFF_DF = 6 * BRANCH_WIDTH
OFF_CX = OFF_DF + N_HEADS
OFF_GATE = OFF_CX + 4 * BRANCH_WIDTH
REST_CX, REST_CG, REST_BU, REST_BV = 0, 1, 2, 3

NT = (((1,), (1,)), ((), ()))
TN = (((0,), (0,)), ((), ()))


def _params(*sem):
    return pltpu.CompilerParams(dimension_semantics=sem, vmem_limit_bytes=VMEM_LIMIT_BYTES)


def _tile(n, target, mult):
    best = None
    for t in range(mult, min(n, target) + 1, mult):
        if n % t == 0:
            best = t
    return n if best is None else best


def _resident(shape, index_map):
    return pl.BlockSpec(shape, index_map, pipeline_mode=pl.Buffered(1))


def _gelu(x):
    return x * (0.5 * (1.0 + jnp.tanh(math.sqrt(2.0 / math.pi) * (x + 0.044715 * (x * x * x)))))


def _rms(x, g):
    return x * lax.rsqrt(jnp.mean(x * x, axis=-1, keepdims=True) + EPS) * g


def _pad_rows(x, rows):
    if x.shape[0] == rows:
        return x
    return jnp.concatenate([x, jnp.zeros((rows - x.shape[0],) + x.shape[1:], x.dtype)], axis=0)


def _lane_cumsum(x):
    n = x.shape[-1]
    lane = lax.broadcasted_iota(jnp.int32, x.shape, x.ndim - 1)
    s = 1
    while s < n:
        x = x + jnp.where(lane >= s, pltpu.roll(x, s, x.ndim - 1), 0.0)
        s *= 2
    return x


def _ada_kernel(c_ref, w_ref, b_ref, o_ref):
    c = c_ref[...]
    a = (c * jax.nn.sigmoid(c)).astype(BF16)
    o_ref[0] = jnp.dot(a, w_ref[0].astype(BF16), preferred_element_type=F32) + b_ref[0]


def _ada(c_all, w_ada, b_ada):
    depth, d, n = w_ada.shape
    bt = c_all.shape[0]
    tn = _tile(n, 1024, LANE)
    return pl.pallas_call(
        _ada_kernel,
        grid=(depth, n // tn),
        in_specs=[pl.BlockSpec((bt, d), lambda l, j: (0, 0)),
                  pl.BlockSpec((1, d, tn), lambda l, j: (l, 0, j)),
                  pl.BlockSpec((1, 1, tn), lambda l, j: (l, 0, j))],
        out_specs=pl.BlockSpec((1, bt, tn), lambda l, j: (l, 0, j)),
        out_shape=jax.ShapeDtypeStruct((depth, bt, n), F32),
        compiler_params=_params("parallel", "parallel"),
        name="ada",
    )(c_all, w_ada, b_ada.reshape(depth, 1, n))


def _norm_kernel(*refs, k_post, coef, k_pre):
    refs = list(refs)
    x_ref = refs.pop(0)
    x = x_ref[0]
    if k_post is not None:
        y_ref, mpost_ref, gpost_ref = refs.pop(0), refs.pop(0), refs.pop(0)
    if k_pre is not None:
        mpre_ref, gpre_ref = refs.pop(0), refs.pop(0)
    if k_post is not None:
        xo_ref = refs.pop(0)
        m = mpost_ref[0]
        gate = m[3 * k_post + 2:3 * k_post + 3]
        x = x + (coef * gate) * _rms(y_ref[0], gpost_ref[k_post:k_post + 1, :])
        xo_ref[0] = x
    if k_pre is not None:
        ho_ref = refs.pop(0)
        m = mpre_ref[0]
        shift = m[3 * k_pre:3 * k_pre + 1]
        scale = m[3 * k_pre + 1:3 * k_pre + 2]
        ho_ref[0] = (_rms(x, gpre_ref[k_pre:k_pre + 1, :]) * (1.0 + scale) + shift).astype(BF16)


def _norm(x, y=None, post=None, pre=None):
    b, t, d = x.shape
    tt = _tile(t, 256, SUBLANE)
    row = pl.BlockSpec((1, tt, d), lambda i, j: (i, j, 0))
    mod_spec = pl.BlockSpec((1, 3 * N_SUB, d), lambda i, j: (i, 0, 0))
    g_spec = pl.BlockSpec((N_SUB, d), lambda i, j: (0, 0))
    args, in_specs, out_shape, out_specs = [x], [row], [], []
    if post is not None:
        args += [y, post[0], post[1]]
        in_specs += [row, mod_spec, g_spec]
        out_shape.append(jax.ShapeDtypeStruct((b, t, d), F32))
        out_specs.append(row)
    if pre is not None:
        args += [pre[0], pre[1]]
        in_specs += [mod_spec, g_spec]
        out_shape.append(jax.ShapeDtypeStruct((b, t, d), BF16))
        out_specs.append(row)
    outs = pl.pallas_call(
        functools.partial(_norm_kernel, k_post=None if post is None else post[2],
                          coef=None if post is None else post[3],
                          k_pre=None if pre is None else pre[2]),
        grid=(b, t // tt), in_specs=in_specs, out_specs=out_specs, out_shape=out_shape,
        compiler_params=_params("parallel", "parallel"),
        name="norm",
    )(*args)
    outs = list(outs)
    x_new = outs.pop(0) if post is not None else None
    h = outs.pop(0) if pre is not None else None
    return x_new, h


def _mm_kernel(x_ref, w_ref, *refs):
    if len(refs) == 3:
        refs[2][...] = refs[0][...].astype(BF16)
    o_ref = refs[-2] if len(refs) == 3 else refs[0]
    o_ref[...] = jnp.dot(x_ref[...], w_ref[0].astype(BF16),
                         preferred_element_type=F32).astype(o_ref.dtype)


def _matmul(x, w, l, out_dtype, tm_target, tn_target, cast_rows=None):
    m, k = x.shape
    n = w.shape[-1]
    tm = _tile(m, tm_target, 16)
    tn = _tile(n, tn_target, LANE)
    nj = n // tn
    args = [x, w]
    in_specs = [_resident((tm, k), lambda i, j: (i, 0)),
                pl.BlockSpec((1, k, tn), lambda i, j: (l, 0, j))]
    out_specs = [pl.BlockSpec((tm, tn), lambda i, j: (i, j))]
    out_shape = [jax.ShapeDtypeStruct((m, n), out_dtype)]
    if cast_rows is not None:
        wt, layer, row0, nrows = cast_rows
        slab = nrows // ((m // tm) * nj)
        assert slab * (m // tm) * nj == nrows and slab % 16 == 0
        d = wt.shape[-1]
        args.append(wt)
        in_specs.append(pl.BlockSpec(
            (None, pl.Element(slab), pl.Element(d)),
            lambda i, j: (layer, pl.multiple_of(row0 + (i * nj + j) * slab, SUBLANE), 0)))
        out_specs.append(pl.BlockSpec((slab, d), lambda i, j: (i * nj + j, 0)))
        out_shape.append(jax.ShapeDtypeStruct((nrows, d), BF16))
    outs = pl.pallas_call(
        _mm_kernel,
        grid=(m // tm, nj),
        in_specs=in_specs, out_specs=out_specs, out_shape=out_shape,
        compiler_params=_params("parallel", "arbitrary"),
        name="matmul",
    )(*args)
    return outs if cast_rows is not None else outs[0]


def _mm_nt_kernel(x_ref, w_ref, *o_refs):
    z = lax.dot_general(x_ref[...], w_ref[...].astype(BF16), NT, preferred_element_type=F32)
    for o_ref in o_refs:
        o_ref[...] = z.astype(o_ref.dtype)


def _matmul_nt(x, wt, l, row0, nrows, out_dtypes, tm_target=2048, tn_target=512):
    m, k = x.shape
    tm = _tile(m, tm_target, 16)
    tn = _tile(nrows, tn_target, LANE)
    return pl.pallas_call(
        _mm_nt_kernel,
        grid=(m // tm, nrows // tn),
        in_specs=[_resident((tm, k), lambda i, j: (i, 0)),
                  pl.BlockSpec((None, pl.Element(tn), pl.Element(k)),
                               lambda i, j: (l, pl.multiple_of(row0 + j * tn, SUBLANE), 0))],
        out_specs=[pl.BlockSpec((tm, tn), lambda i, j: (i, j)) for _ in out_dtypes],
        out_shape=[jax.ShapeDtypeStruct((m, nrows), dt) for dt in out_dtypes],
        compiler_params=_params("parallel", "arbitrary"),
        name="matmul_nt",
    )(x, wt)


def _ffn_up_kernel(x_ref, wg_ref, wu_ref, *refs):
    if len(refs) == 3:
        refs[2][...] = refs[0][0].astype(BF16)
    o_ref = refs[-2] if len(refs) == 3 else refs[0]
    x = x_ref[...]
    g = jnp.dot(x, wg_ref[0].astype(BF16), preferred_element_type=F32)
    u = jnp.dot(x, wu_ref[0].astype(BF16), preferred_element_type=F32)
    o_ref[...] = ((g * jax.nn.sigmoid(g)) * u).astype(o_ref.dtype)


def _ffn_up(x, wg, wu, l, w_down=None):
    m, k = x.shape
    n = wg.shape[-1]
    tm = _tile(m, 2048, 16)
    tn = _tile(n, 256, LANE)
    nj = n // tn
    w_spec = pl.BlockSpec((1, k, tn), lambda i, j: (l, 0, j))
    args = [x, wg, wu]
    in_specs = [_resident((tm, k), lambda i, j: (i, 0)), w_spec, w_spec]
    out_specs = [pl.BlockSpec((tm, tn), lambda i, j: (i, j))]
    out_shape = [jax.ShapeDtypeStruct((m, n), BF16)]
    if w_down is not None:
        d = w_down.shape[-1]
        slab = n // ((m // tm) * nj)
        assert slab * (m // tm) * nj == n and slab % 16 == 0
        args.append(w_down)
        in_specs.append(pl.BlockSpec((1, slab, d), lambda i, j: (l, i * nj + j, 0)))
        out_specs.append(pl.BlockSpec((slab, d), lambda i, j: (i * nj + j, 0)))
        out_shape.append(jax.ShapeDtypeStruct((n, d), BF16))
    outs = pl.pallas_call(
        _ffn_up_kernel,
        grid=(m // tm, nj),
        in_specs=in_specs, out_specs=out_specs, out_shape=out_shape,
        compiler_params=_params("parallel", "arbitrary"),
        name="ffn_up",
    )(*args)
    return outs if w_down is not None else outs[0]


def _merge_kernel(h_ref, oa_ref, ob_ref, oc_ref, od_ref, wg0, wg1, wg2, wg3, wo0, wo1, wo2, wo3, y_ref):
    h = h_ref[...]
    y = None
    for o_ref, wg, wo in ((oa_ref, wg0, wo0), (ob_ref, wg1, wo1), (oc_ref, wg2, wo2), (od_ref, wg3, wo3)):
        gate = jax.nn.sigmoid(lax.dot_general(h, wg[0], NT, preferred_element_type=F32))
        proj = jnp.dot(o_ref[...], wo[0].astype(BF16), preferred_element_type=F32)
        y = gate * proj if y is None else y + gate * proj
    y_ref[...] = y


def _merge(h, branches, w_gate, w_out, l):
    m, d = h.shape
    n = w_out.shape[-1]
    bw = branches[0].shape[-1]
    tm = _tile(m, 1024, 16)
    tn = _tile(n, 256, LANE)
    nj = n // tn

    def gate_spec(i):
        return pl.BlockSpec((1, tn, d), lambda a, j: (0, i * nj + j, 0))

    def out_spec(i):
        return pl.BlockSpec((1, bw, tn), lambda a, j: (l * N_BRANCH + i, 0, j))

    return pl.pallas_call(
        _merge_kernel,
        grid=(m // tm, nj),
        in_specs=[_resident((tm, d), lambda a, j: (a, 0))] + [_resident((tm, bw), lambda a, j: (a, 0))] * N_BRANCH
        + [gate_spec(i) for i in range(N_BRANCH)] + [out_spec(i) for i in range(N_BRANCH)],
        out_specs=pl.BlockSpec((tm, tn), lambda a, j: (a, j)),
        out_shape=jax.ShapeDtypeStruct((m, n), F32),
        compiler_params=_params("parallel", "arbitrary"),
        name="merge",
    )(h, *branches, w_gate, w_gate, w_gate, w_gate, w_out, w_out, w_out, w_out)


def _bias_kernel(tab_ref, o_ref, *, nq, nk, tiles, transposed, kv_end):
    h = pl.program_id(0)
    shape = (nk, nq) if transposed else (nq, nk)
    q_i = lax.broadcasted_iota(jnp.int32, shape, 1 if transposed else 0)
    k_i = lax.broadcasted_iota(jnp.int32, shape, 0 if transposed else 1)
    for c, tile in enumerate(tiles):
        for mp in range(2):
            if tile is None:
                val = jnp.full(shape, NEG_INF, F32)
            else:
                q0, k0 = tile
                q_pos, k_pos = q0 + q_i, k0 + k_i
                rel = k_pos - q_pos
                n = jnp.abs(rel)
                large = NUM_BUCKETS // 4
                for start in T5_LARGE_STARTS:
                    large = large + jnp.where(n >= start, 1, 0)
                bucket = jnp.where(rel > 0, NUM_BUCKETS // 2, 0) + jnp.where(n < NUM_BUCKETS // 4, n, large)
                val = jnp.zeros(shape, F32)
                for b in range(NUM_BUCKETS):
                    val = jnp.where(bucket == b, tab_ref[b * (2 * N_HEADS) + h * 2 + mp], val)
                val = val - tab_ref[T5_FAR_BUCKET * (2 * N_HEADS) + h * 2 + mp]
                visible = jnp.logical_and((k_pos >> CHUNK_SHIFT) <= (q_pos >> CHUNK_SHIFT), k_pos < kv_end)
                val = jnp.where(visible, val, NEG_INF)
            if transposed:
                o_ref[0, c, :, mp * nq:(mp + 1) * nq] = val
            else:
                o_ref[0, c, mp * nq:(mp + 1) * nq, :] = val


def _bias_tiles(rel_bias, nq, nk, tiles, transposed, kv_end):
    shape = (nk, 2 * nq) if transposed else (2 * nq, nk)
    return pl.pallas_call(
        functools.partial(_bias_kernel, nq=nq, nk=nk, tiles=tuple(tiles), transposed=transposed, kv_end=kv_end),
        grid=(N_HEADS,),
        in_specs=[pl.BlockSpec(memory_space=pltpu.SMEM)],
        out_specs=pl.BlockSpec((1, len(tiles)) + shape, lambda h: (h, 0, 0, 0)),
        out_shape=jax.ShapeDtypeStruct((N_HEADS, len(tiles)) + shape, F32),
        compiler_params=_params("parallel"),
        name="bias_tiles",
    )(rel_bias.reshape(-1))


def _lambda(lq_ref, lam_init):
    lq = lq_ref[...]
    return (jnp.exp(jnp.sum(lq[0:1] * lq[1:2], axis=-1, keepdims=True))
            - jnp.exp(jnp.sum(lq[2:3] * lq[3:4], axis=-1, keepdims=True)) + lam_init)


def _split_maps(q):
    lane = lax.broadcasted_iota(jnp.int32, q.shape, 1)
    q = q * (HEAD_DIM_A ** -0.5)
    return jnp.concatenate([jnp.where(lane < HEAD_DIM_A, q, 0.0),
                            jnp.where(lane >= HEAD_DIM_A, q, 0.0)], axis=0).astype(BF16)


def _flash_tile_t(carries, acc_sc, scores, values, query_terms=None):
    out, probs, alphas = [], [], []
    for g, ((m, l), s) in enumerate(zip(carries, scores)):
        if query_terms is None:
            m_new = jnp.maximum(m, jnp.max(s, axis=0, keepdims=True))
            p = jnp.exp(s - m_new)
            alpha = jnp.exp(m - m_new)
        else:
            m_new = jnp.maximum(m, jnp.max(s, axis=0, keepdims=True) + query_terms[g])
            p = jnp.exp2(s - (m_new - query_terms[g]))
            alpha = jnp.exp2(m - m_new)
        out.append((m_new, alpha * l + jnp.sum(p, axis=0, keepdims=True)))
        probs.append(p.astype(BF16))
        alphas.append(alpha)
    for g, (v, p, alpha) in enumerate(zip(values, probs, alphas)):
        acc_sc[g] = alpha * acc_sc[g] + lax.dot_general(v, p, TN, preferred_element_type=F32)
    return tuple(out)


def _flash_init(acc_ref):
    acc_ref[...] = jnp.zeros(acc_ref.shape, F32)
    nq = acc_ref.shape[-1]
    return jnp.full((1, nq), NEG_INF, F32), jnp.zeros((1, nq), F32)


HEADS_PER_STEP = 8


def _head(x, g):
    return x[:, g * HEAD_W:(g + 1) * HEAD_W]


def _attn_a_prompt_kernel(lq_ref, q_ref, k_ref, v_ref, bias_ref, g_ref, o_ref, acc_sc, *, tq, lam_init):
    qi = pl.program_id(2)
    heads = range(HEADS_PER_STEP)
    q_all = q_ref[0]
    q2 = [_split_maps(_head(q_all, g)) for g in heads]

    def tile(kt, carries, bias_tile):
        k0 = pl.multiple_of(kt * tq, tq)
        k_all = k_ref[0, pl.ds(k0, tq), :]
        v_all = v_ref[0, pl.ds(k0, tq), :]
        scores = [lax.dot_general(_head(k_all, g).astype(BF16), q2[g], NT, preferred_element_type=F32)
                  for g in heads]
        if bias_tile is not None:
            scores = [s + bias_ref[g, bias_tile] for g, s in enumerate(scores)]
        return _flash_tile_t(carries, acc_sc, scores, [_head(v_all, g).astype(BF16) for g in heads])

    carries = tile(qi, tuple(_flash_init(acc_sc.at[g]) for g in heads), 0)
    carries = tile(jnp.maximum(qi - 1, 0), carries, jnp.where(qi == 0, 2, 1))
    carries = lax.fori_loop(0, qi - 1, lambda kt, c: tile(kt, c, None), carries)
    lam = _lambda(lq_ref, lam_init)
    for g in heads:
        o = acc_sc[g] * (1.0 / carries[g][1])
        o = (o[:, :tq] - lam * o[:, tq:]).T
        o_ref[0, :, g * HEAD_W:(g + 1) * HEAD_W] = (_rms(o, g_ref[...]) * (1.0 - lam_init)).astype(o_ref.dtype)


def _attn_a_prompt(z_qkv, bias, lq, g_norm, *, tq, lam_init):
    b, t, _ = z_qkv.shape
    gw = HEADS_PER_STEP * HEAD_W
    nblk = BRANCH_WIDTH // gw
    return pl.pallas_call(
        functools.partial(_attn_a_prompt_kernel, tq=tq, lam_init=lam_init),
        grid=(b, nblk, t // tq),
        in_specs=[pl.BlockSpec((4, HEAD_DIM_A), lambda i, h, j: (0, 0)),
                  pl.BlockSpec((1, tq, gw), lambda i, h, j: (i, j, h)),
                  _resident((1, t, gw), lambda i, h, j: (i, 0, nblk + h)),
                  _resident((1, t, gw), lambda i, h, j: (i, 0, 2 * nblk + h)),
                  _resident((HEADS_PER_STEP, 3, tq, 2 * tq), lambda i, h, j: (h, 0, 0, 0)),
                  pl.BlockSpec((1, HEAD_W), lambda i, h, j: (0, 0))],
        out_specs=pl.BlockSpec((1, tq, gw), lambda i, h, j: (i, j, h)),
        out_shape=jax.ShapeDtypeStruct((b, t, N_HEADS * HEAD_W), BF16),
        scratch_shapes=[pltpu.VMEM((HEADS_PER_STEP, HEAD_W, 2 * tq), F32)],
        compiler_params=_params("parallel", "parallel", "arbitrary"),
        name="attn_a_prompt",
    )(lq, z_qkv, z_qkv, z_qkv, bias, g_norm.reshape(1, HEAD_W))


def _attn_d_prompt_kernel(q_ref, k_ref, v_ref, cq_ref, ck_ref, o_ref, ckb_sc, acc_sc, *, tq, t):
    h0 = pl.program_id(1) * HEADS_PER_STEP
    qi = pl.program_id(2)
    heads = range(HEADS_PER_STEP)

    @pl.when(qi == 0)
    def _():
        for g in heads:
            ck = ck_ref[0, pl.ds(h0 + g, 1), :] * LOG2E
            for c in range(t // LANE):
                ckb_sc[g, c * LANE:(c + 1) * LANE, :] = jnp.broadcast_to(
                    ck[:, c * LANE:(c + 1) * LANE], (LANE, LANE)).T

    q_all = q_ref[0]
    q = [_head(q_all, g).astype(BF16) for g in heads]
    cq = [cq_ref[0, pl.ds(h0 + g, 1), :] * LOG2E for g in heads]
    scale = HEAD_W ** -0.5 * LOG2E

    def tile(kt, carries, diagonal):
        k0 = pl.multiple_of(kt * tq, tq)
        k_all = k_ref[0, pl.ds(k0, tq), :]
        v_all = v_ref[0, pl.ds(k0, tq), :]
        scores = [lax.dot_general(_head(k_all, g).astype(BF16), q[g], NT, preferred_element_type=F32)
                  for g in heads]
        for g in heads:
            ckb = ckb_sc[g, pl.ds(k0, tq), :]
            s = scores[g] * scale - jnp.concatenate([ckb] * (tq // LANE), axis=1)
            if diagonal:
                key = lax.broadcasted_iota(jnp.int32, (tq, tq), 0)
                qry = lax.broadcasted_iota(jnp.int32, (tq, tq), 1)
                s = jnp.where(key <= qry, s, NEG_INF)
            scores[g] = s
        return _flash_tile_t(carries, acc_sc, scores, [_head(v_all, g).astype(BF16) for g in heads], cq)

    carries = tile(qi, tuple(_flash_init(acc_sc.at[g]) for g in heads), True)
    carries = lax.fori_loop(0, qi, lambda kt, c: tile(kt, c, False), carries)
    for g in heads:
        o = acc_sc[g] * (1.0 / carries[g][1])
        o_ref[0, :, g * HEAD_W:(g + 1) * HEAD_W] = o.T.astype(o_ref.dtype)


def _attn_d_prompt(z_qkv, cum, *, tq):
    b, t, _ = z_qkv.shape
    gw = HEADS_PER_STEP * HEAD_W
    nblk = BRANCH_WIDTH // gw
    return pl.pallas_call(
        functools.partial(_attn_d_prompt_kernel, tq=tq, t=t),
        grid=(b, nblk, t // tq),
        in_specs=[pl.BlockSpec((1, tq, gw), lambda i, h, j: (i, j, 3 * nblk + h)),
                  _resident((1, t, gw), lambda i, h, j: (i, 0, 4 * nblk + h)),
                  _resident((1, t, gw), lambda i, h, j: (i, 0, 5 * nblk + h)),
                  pl.BlockSpec((1, N_HEADS, tq), lambda i, h, j: (i, 0, j)),
                  pl.BlockSpec((1, N_HEADS, t), lambda i, h, j: (i, 0, 0))],
        out_specs=pl.BlockSpec((1, tq, gw), lambda i, h, j: (i, j, h)),
        out_shape=jax.ShapeDtypeStruct((b, t, N_HEADS * HEAD_W), BF16),
        scratch_shapes=[pltpu.VMEM((HEADS_PER_STEP, t, LANE), F32),
                        pltpu.VMEM((HEADS_PER_STEP, HEAD_W, tq), F32)],
        compiler_params=_params("parallel", "arbitrary", "arbitrary"),
        name="attn_d_prompt",
    )(z_qkv, z_qkv, z_qkv, cum, cum)


def _softmax_two(s_p, s_n, v_p, v_n):
    m = jnp.maximum(jnp.max(s_p, axis=-1, keepdims=True), jnp.max(s_n, axis=-1, keepdims=True))
    p_p = jnp.exp(s_p - m)
    p_n = jnp.exp(s_n - m)
    l = jnp.sum(p_p, axis=-1, keepdims=True) + jnp.sum(p_n, axis=-1, keepdims=True)
    acc = (jnp.dot(p_p.astype(BF16), v_p, preferred_element_type=F32)
           + jnp.dot(p_n.astype(BF16), v_n, preferred_element_type=F32))
    return acc / l


def _attn_a_sample_kernel(lq_ref, q_ref, kt_ref, vp_ref, kn_ref, vn_ref, bp_ref, bn_ref, g_ref, o_ref, *,
                          t, lam_init):
    lam = _lambda(lq_ref, lam_init)
    q_all, kn_all, vn_all = q_ref[0], kn_ref[0], vn_ref[0]
    for h in range(N_HEADS):
        q2 = _split_maps(_head(q_all, h))
        kn = _pad_rows(_head(kn_all, h), LANE).astype(BF16)
        vn = _pad_rows(_head(vn_all, h), LANE).astype(BF16)
        s_p = jnp.dot(q2, kt_ref[0, 0, h].astype(BF16), preferred_element_type=F32) + bp_ref[h, 0]
        s_n = lax.dot_general(q2, kn, NT, preferred_element_type=F32) + bn_ref[h, 0]
        o = _softmax_two(s_p, s_n, vp_ref[0, 0, :, h, :].astype(BF16), vn)
        o = o[:t] - lam * o[t:]
        o_ref[0, :, h * HEAD_W:(h + 1) * HEAD_W] = (_rms(o, g_ref[...]) * (1.0 - lam_init)).astype(o_ref.dtype)


def _attn_a_sample(z_qkv, k_cache_t, v_cache, l, bias_past, bias_new, lq, g_norm, *, lam_init):
    b, t, _ = z_qkv.shape
    past = v_cache.shape[2]
    bw = BRANCH_WIDTH
    return pl.pallas_call(
        functools.partial(_attn_a_sample_kernel, t=t, lam_init=lam_init),
        grid=(b,),
        in_specs=[pl.BlockSpec((4, HEAD_DIM_A), lambda i: (0, 0)),
                  pl.BlockSpec((1, t, bw), lambda i: (i, 0, 0)),
                  pl.BlockSpec((1, 1, N_HEADS, HEAD_W, past), lambda i: (l, i, 0, 0, 0)),
                  pl.BlockSpec((1, 1, past, N_HEADS, HEAD_W), lambda i: (l, i, 0, 0, 0)),
                  pl.BlockSpec((1, t, bw), lambda i: (i, 0, 1)),
                  pl.BlockSpec((1, t, bw), lambda i: (i, 0, 2)),
                  pl.BlockSpec((N_HEADS, 1, 2 * t, past), lambda i: (0, 0, 0, 0)),
                  pl.BlockSpec((N_HEADS, 1, 2 * t, LANE), lambda i: (0, 0, 0, 0)),
                  pl.BlockSpec((1, HEAD_W), lambda i: (0, 0))],
        out_specs=pl.BlockSpec((1, t, bw), lambda i: (i, 0, 0)),
        out_shape=jax.ShapeDtypeStruct((b, t, bw), BF16),
        compiler_params=_params("parallel"),
        name="attn_a_sample",
    )(lq, z_qkv, k_cache_t, v_cache, z_qkv, z_qkv, bias_past, bias_new, g_norm.reshape(1, HEAD_W))


def _attn_d_sample_kernel(q_ref, kp_ref, vp_ref, kn_ref, vn_ref, cq_ref, ckp_ref, ckn_ref, o_ref, *, t):
    heads = range(N_HEADS)
    past = kp_ref.shape[2]
    scale = HEAD_W ** -0.5
    q_all, kn_all, vn_all = q_ref[0], kn_ref[0], vn_ref[0]
    q = jnp.concatenate([_head(q_all, h) for h in heads], axis=0).astype(BF16)
    cq = jnp.concatenate([cq_ref[0, h] for h in heads], axis=0)
    k_all = kp_ref[0, 0].reshape(past * N_HEADS, HEAD_W).astype(BF16)
    v_all = vp_ref[0, 0].reshape(past * N_HEADS, HEAD_W).astype(BF16)
    s_p = lax.dot_general(q, k_all, NT, preferred_element_type=F32) * scale + cq - ckp_ref[0]
    row_head = jnp.concatenate([jnp.full((t, 1), h, jnp.int32) for h in heads], axis=0)
    col_head = lax.broadcasted_iota(jnp.int32, (1, past * N_HEADS), 1) & (N_HEADS - 1)
    s_p = jnp.where(row_head == col_head, s_p, NEG_INF)

    kn = [_pad_rows(_head(kn_all, h), LANE).astype(BF16) for h in heads]
    vn = [_pad_rows(_head(vn_all, h), LANE).astype(BF16) for h in heads]
    s_n = jnp.concatenate([lax.dot_general(q[h * t:(h + 1) * t], kn[h], NT, preferred_element_type=F32)
                           for h in heads], axis=0) * scale
    ckn = jnp.concatenate([jnp.broadcast_to(ckn_ref[0, h:h + 1, :], (t, LANE)) for h in heads], axis=0)
    qry = jnp.concatenate([lax.broadcasted_iota(jnp.int32, (t, LANE), 0)] * N_HEADS, axis=0)
    key = lax.broadcasted_iota(jnp.int32, (N_HEADS * t, LANE), 1)
    s_n = jnp.where(key <= qry, s_n + cq - ckn, NEG_INF)

    m = jnp.maximum(jnp.max(s_p, axis=-1, keepdims=True), jnp.max(s_n, axis=-1, keepdims=True))
    p_p = jnp.exp(s_p - m)
    p_n = jnp.exp(s_n - m)
    l_sum = jnp.sum(p_p, axis=-1, keepdims=True) + jnp.sum(p_n, axis=-1, keepdims=True)
    p_n = p_n.astype(BF16)
    acc = jnp.dot(p_p.astype(BF16), v_all, preferred_element_type=F32)
    acc = acc + jnp.concatenate([jnp.dot(p_n[h * t:(h + 1) * t], vn[h], preferred_element_type=F32)
                                 for h in heads], axis=0)
    o = acc / l_sum
    for h in heads:
        o_ref[0, :, h * HEAD_W:(h + 1) * HEAD_W] = o[h * t:(h + 1) * t].astype(o_ref.dtype)


def _attn_d_sample(z_qkv, k_cache, v_cache, l, cq, cum_past, cum_new):
    b, t, _ = z_qkv.shape
    past = k_cache.shape[2]
    bw = BRANCH_WIDTH
    assert N_HEADS & (N_HEADS - 1) == 0
    cache_spec = pl.BlockSpec((1, 1, past, N_HEADS, HEAD_W), lambda i: (l, i, 0, 0, 0))
    return pl.pallas_call(
        functools.partial(_attn_d_sample_kernel, t=t),
        grid=(b,),
        in_specs=[pl.BlockSpec((1, t, bw), lambda i: (i, 0, 3)),
                  cache_spec, cache_spec,
                  pl.BlockSpec((1, t, bw), lambda i: (i, 0, 4)),
                  pl.BlockSpec((1, t, bw), lambda i: (i, 0, 5)),
                  pl.BlockSpec((1, N_HEADS, t, 1), lambda i: (i, 0, 0, 0)),
                  pl.BlockSpec((1, 1, past * N_HEADS), lambda i: (i, 0, 0)),
                  pl.BlockSpec((1, N_HEADS, LANE), lambda i: (i, 0, 0))],
        out_specs=pl.BlockSpec((1, t, bw), lambda i: (i, 0, 0)),
        out_shape=jax.ShapeDtypeStruct((b, t, bw), BF16),
        compiler_params=_params("parallel"),
        name="attn_d_sample",
    )(z_qkv, k_cache, v_cache, z_qkv, z_qkv, cq, cum_past, cum_new)


def _logf_kernel(*refs, tt, has_past):
    refs = list(refs)
    h_ref, wf_ref, bf_ref = refs.pop(0), refs.pop(0), refs.pop(0)
    past_ref = refs.pop(0) if has_past else None
    logf_ref, cum_ref = refs.pop(0), refs.pop(0)
    cump_ref = refs.pop(0) if has_past else None
    carry_sc = refs.pop(0)

    @pl.when(pl.program_id(1) == 0)
    def _():
        if has_past:
            cump = _lane_cumsum(past_ref[0, 0])
            cump_ref[0] = cump
            carry_sc[...] = jnp.broadcast_to(cump[:, cump.shape[1] - 1:], carry_sc.shape)
        else:
            carry_sc[...] = jnp.zeros(carry_sc.shape, F32)

    ttp = logf_ref.shape[-1]
    hb = _pad_rows(h_ref[...], ttp)
    z = lax.dot_general(wf_ref[...].astype(BF16), hb, NT, preferred_element_type=F32) + bf_ref[...]
    logf = jnp.minimum(z, 0.0) - jnp.log1p(jnp.exp(-jnp.abs(z)))
    if tt < ttp:
        logf = jnp.where(lax.broadcasted_iota(jnp.int32, logf.shape, 1) < tt, logf, 0.0)
    logf_ref[0] = logf
    cum = _lane_cumsum(logf) + carry_sc[:, 0:1]
    cum_ref[0] = cum
    carry_sc[...] = jnp.broadcast_to(cum[:, ttp - 1:], carry_sc.shape)


def _logf(hf, b, w_in_t, l, b_forget, past_logf_t):
    m, d = hf.shape
    t = m // b
    tt = _tile(t, 1024, LANE) if t >= LANE else t
    ttp = max(tt, LANE)
    nt = t // tt
    has_past = past_logf_t is not None
    args = [hf, w_in_t, b_forget.reshape(N_HEADS, 1)]
    in_specs = [pl.BlockSpec((tt, d), lambda i, j: (i * nt + j, 0)),
                pl.BlockSpec((None, pl.Element(N_HEADS), pl.Element(d)), lambda i, j: (l, OFF_DF, 0)),
                pl.BlockSpec((N_HEADS, 1), lambda i, j: (0, 0))]
    out_shape = [jax.ShapeDtypeStruct((b, N_HEADS, nt * ttp), F32)] * 2
    out_specs = [pl.BlockSpec((1, N_HEADS, ttp), lambda i, j: (i, 0, j))] * 2
    if has_past:
        past = past_logf_t.shape[-1]
        args.append(past_logf_t)
        in_specs.append(pl.BlockSpec((1, 1, N_HEADS, past), lambda i, j: (l, i, 0, 0)))
        out_shape.append(jax.ShapeDtypeStruct((b, N_HEADS, past), F32))
        out_specs.append(pl.BlockSpec((1, N_HEADS, past), lambda i, j: (i, 0, 0)))
    outs = pl.pallas_call(
        functools.partial(_logf_kernel, tt=tt, has_past=has_past),
        grid=(b, nt), in_specs=in_specs, out_specs=out_specs, out_shape=out_shape,
        scratch_shapes=[pltpu.VMEM((N_HEADS, LANE), F32)],
        compiler_params=_params("parallel", "arbitrary"),
        name="logf_cumsum",
    )(*args)
    return outs[0], outs[1], (outs[2] if has_past else None)


def _lru_kernel(cx_ref, cg_ref, buf0_ref, h0_ref, cw_ref, cb_ref, wr_ref, br_ref, wi_ref, bi_ref,
                lam_ref, oc_ref, nbuf_ref, hl_ref, tail_sc, h_sc, *, tt):
    w = cx_ref.shape[-1]
    blk = w // LRU_BLOCKS
    pad = SUBLANE - (CONV_WIDTH - 1)

    @pl.when(pl.program_id(1) == 0)
    def _():
        tail_sc[0:pad, :] = jnp.zeros((pad, w), F32)
        tail_sc[pad:SUBLANE, :] = buf0_ref[0]
        h_sc[...] = h0_ref[0]

    full = jnp.concatenate([tail_sc[...], cx_ref[0]], axis=0)
    cw = cw_ref[...]
    xc = cb_ref[...]
    for j in range(CONV_WIDTH):
        xc = xc + full[pad + j:pad + j + tt] * cw[j:j + 1]
    tail_sc[...] = full[tt:tt + SUBLANE]
    nbuf_ref[0] = full[tt + pad:tt + SUBLANE]

    xb = xc.astype(BF16)

    def gate(w_ref, b_ref):
        parts = [jnp.dot(xb[:, g * blk:(g + 1) * blk], w_ref[g].astype(BF16), preferred_element_type=F32)
                 for g in range(LRU_BLOCKS)]
        return jax.nn.sigmoid(jnp.concatenate(parts, axis=1) + b_ref[...])

    r = gate(wr_ref, br_ref)
    i = gate(wi_ref, bi_ref)
    nl = -lam_ref[...]
    softplus = jnp.maximum(nl, 0.0) + jnp.log1p(jnp.exp(-jnp.abs(nl)))
    log_a = (-LRU_C * r) * softplus
    a = jnp.exp(log_a)
    u = jnp.sqrt(-jnp.tanh(log_a) * (a * a + 1.0)) * (i * xc)

    in_group = lax.broadcasted_iota(jnp.int32, (tt, w), 0) & (SUBLANE - 1)
    s = 1
    while s < SUBLANE:
        valid = in_group >= s
        u = jnp.where(valid, a * pltpu.roll(u, s, 0) + u, u)
        a = jnp.where(valid, a * pltpu.roll(a, s, 0), a)
        s *= 2
    h_prev = h_sc[...]
    groups = []
    for g in range(tt // SUBLANE):
        rows = slice(g * SUBLANE, (g + 1) * SUBLANE)
        groups.append(u[rows] + a[rows] * h_prev)
        h_prev = groups[-1][SUBLANE - 1:SUBLANE]
    h = jnp.concatenate(groups, axis=0)
    h_sc[...] = h_prev
    hl_ref[0] = h_prev
    oc_ref[0] = (h * _gelu(cg_ref[0])).astype(oc_ref.dtype)


def _lru(zcb, conv_buf, h0, conv_w, conv_b, w_r, b_r, w_i, b_i, lam):
    b, t, _ = zcb.shape
    w = BRANCH_WIDTH
    tt = _tile(t, 256, SUBLANE)
    vec = pl.BlockSpec((1, w), lambda i, j: (0, 0))
    mat = pl.BlockSpec((LRU_BLOCKS, w // LRU_BLOCKS, w // LRU_BLOCKS), lambda i, j: (0, 0, 0))
    return pl.pallas_call(
        functools.partial(_lru_kernel, tt=tt),
        grid=(b, t // tt),
        in_specs=[pl.BlockSpec((1, tt, w), lambda i, j: (i, j, REST_CX)),
                  pl.BlockSpec((1, tt, w), lambda i, j: (i, j, REST_CG)),
                  pl.BlockSpec((1, CONV_WIDTH - 1, w), lambda i, j: (i, 0, 0)),
                  pl.BlockSpec((1, 1, w), lambda i, j: (i, 0, 0)),
                  pl.BlockSpec((CONV_WIDTH, w), lambda i, j: (0, 0)),
                  vec, mat, vec, mat, vec, vec],
        out_specs=[pl.BlockSpec((1, tt, w), lambda i, j: (i, j, 0)),
                   pl.BlockSpec((1, CONV_WIDTH - 1, w), lambda i, j: (i, 0, 0)),
                   pl.BlockSpec((1, 1, w), lambda i, j: (i, 0, 0))],
        out_shape=[jax.ShapeDtypeStruct((b, t, w), BF16),
                   jax.ShapeDtypeStruct((b, CONV_WIDTH - 1, w), F32),
                   jax.ShapeDtypeStruct((b, 1, w), F32)],
        scratch_shapes=[pltpu.VMEM((SUBLANE, w), F32), pltpu.VMEM((1, w), F32)],
        compiler_params=_params("parallel", "arbitrary"),
        name="conv_lru",
    )(zcb, zcb, conv_buf, h0.reshape(b, 1, w), conv_w, conv_b.reshape(1, w), w_r, b_r.reshape(1, w),
      w_i, b_i.reshape(1, w), lam.reshape(1, w))


def _gmlp_kernel(zu_ref, zv_ref, lng_ref, lnb_ref, ws_ref, bst_ref, ob_ref, *vn_refs, ct):
    u = _gelu(zu_ref[0])
    v = _gelu(zv_ref[0])
    w = v.shape[-1]
    gw = w // GMLP_GROUPS
    d = v - jnp.mean(v, axis=-1, keepdims=True)
    vn = d * lax.rsqrt(jnp.mean(d * d, axis=-1, keepdims=True) + EPS) * lng_ref[...] + lnb_ref[...]
    if vn_refs:
        vn_refs[0][0] = vn
    vb = _pad_rows(vn, GMLP_CHUNK).astype(BF16)
    tril = (lax.broadcasted_iota(jnp.int32, (GMLP_CHUNK, GMLP_CHUNK), 0)
            >= lax.broadcasted_iota(jnp.int32, (GMLP_CHUNK, GMLP_CHUNK), 1))
    parts = []
    for g in range(GMLP_GROUPS):
        wg = jnp.where(tril, ws_ref[g], 0.0).astype(BF16)
        s = jnp.dot(wg, vb[:, g * gw:(g + 1) * gw], preferred_element_type=F32) + bst_ref[:, g:g + 1]
        parts.append(s[:ct])
    ob_ref[0] = (u * jnp.concatenate(parts, axis=1)).astype(ob_ref.dtype)


def _gmlp(zcb, ln_g, ln_b, ws, bs, want_rows):
    b, t, _ = zcb.shape
    w = BRANCH_WIDTH
    ct = min(t, GMLP_CHUNK)
    vec = pl.BlockSpec((1, w), lambda i, j: (0, 0))
    row_out = pl.BlockSpec((1, ct, w), lambda i, j: (i, j, 0))
    out_shape = [jax.ShapeDtypeStruct((b, t, w), BF16)]
    out_specs = [row_out]
    if want_rows:
        out_shape.append(jax.ShapeDtypeStruct((b, t, w), F32))
        out_specs.append(row_out)
    outs = pl.pallas_call(
        functools.partial(_gmlp_kernel, ct=ct),
        grid=(b, t // ct),
        in_specs=[pl.BlockSpec((1, ct, w), lambda i, j: (i, j, REST_BU)),
                  pl.BlockSpec((1, ct, w), lambda i, j: (i, j, REST_BV)),
                  vec, vec,
                  pl.BlockSpec((GMLP_GROUPS, GMLP_CHUNK, GMLP_CHUNK), lambda i, j: (0, 0, 0)),
                  pl.BlockSpec((GMLP_CHUNK, GMLP_GROUPS), lambda i, j: (0, 0))],
        out_specs=out_specs, out_shape=out_shape,
        compiler_params=_params("parallel", "parallel"),
        name="gmlp",
    )(zcb, zcb, ln_g.reshape(1, w), ln_b.reshape(1, w), ws, bs.T)
    return (outs[0], outs[1]) if want_rows else (outs[0], None)


def _mixers(h, l, p, caches, bias):
    b, t, d = h.shape
    m = b * t
    hf = h.reshape(m, d)
    bw = BRANCH_WIDTH
    w_in_t = p['w_in_t']
    lam_init = 0.8 - 0.6 * math.exp(-0.3 * l)

    qkv = _matmul_nt(hf, w_in_t, l, 0, OFF_DF, [F32, BF16] if caches is None else [F32])
    z_qkv = qkv[0].reshape(b, t, OFF_DF)
    zcb = _matmul_nt(hf, w_in_t, l, OFF_CX, 4 * bw, [F32])[0].reshape(b, t, 4 * bw)
    ka = z_qkv[:, :, 1 * bw:2 * bw]
    va = z_qkv[:, :, 2 * bw:3 * bw]
    kd = z_qkv[:, :, 4 * bw:5 * bw]
    vd = z_qkv[:, :, 5 * bw:6 * bw]

    if caches is None:
        logf_t, cum, _ = _logf(hf, b, w_in_t, l, p['b_forget'][l], None)
        tq = bias.shape[2]
        z_bf = qkv[1].reshape(b, t, OFF_DF)
        o_a = _attn_a_prompt(z_bf, bias, p['lambda_qk'][l], p['diff_norm_g'][l], tq=tq, lam_init=lam_init)
        o_d = _attn_d_prompt(z_bf, cum, tq=_tile(t, 2 * tq, LANE))
        conv_buf = jnp.zeros((b, CONV_WIDTH - 1, bw), F32)
        lru_h0 = jnp.zeros((b, bw), F32)
    else:
        a_k_t, a_v, d_k, d_v, d_logf_t, state_conv, state_lru = caches
        logf_t, cum_new, cum_past = _logf(hf, b, w_in_t, l, p['b_forget'][l], d_logf_t)
        logf_t = logf_t[:, :, :t]
        cq = cum_new[:, :, :t].reshape(b, N_HEADS, t, 1)
        o_a = _attn_a_sample(z_qkv, a_k_t, a_v, l, bias[0], bias[1], p['lambda_qk'][l], p['diff_norm_g'][l],
                             lam_init=lam_init)
        cum_past = cum_past.transpose(0, 2, 1).reshape(b, 1, -1)
        o_d = _attn_d_sample(z_qkv, d_k, d_v, l, cq, cum_past, cum_new)
        conv_buf, lru_h0 = state_conv[l], state_lru[l]

    o_c, new_buf, h_last = _lru(zcb, conv_buf, lru_h0, p['conv_w'][l], p['conv_b'][l], p['w_rgate'][l],
                                p['b_rgate'][l], p['w_igate'][l], p['b_igate'][l], p['lru_lambda'][l])
    o_b, v_rows = _gmlp(zcb, p['gmlp_ln_g'][l], p['gmlp_ln_b'][l], p['gmlp_ws'][l], p['gmlp_bs'][l],
                        want_rows=caches is not None)

    y = _merge(hf, [o.reshape(m, bw) for o in (o_a, o_b, o_c, o_d)], p['bf16_weights']['gate', l],
               p['w_out'], l)
    states = (ka.reshape(b, t, N_HEADS, 2, HEAD_DIM_A), va.reshape(b, t, N_HEADS, HEAD_W),
              kd.reshape(b, t, N_HEADS, HEAD_W), vd.reshape(b, t, N_HEADS, HEAD_W),
              logf_t.transpose(0, 2, 1), new_buf, h_last.reshape(b, bw), v_rows)
    return y, states


def _ffn(h, l, j, p):
    b, t, d = h.shape
    idx = l * 2 + j
    x = h.reshape(b * t, d)
    bf16_weights = p['bf16_weights']
    if ('down', idx) in bf16_weights:
        mid = _ffn_up(x, p['w_ffn_gate'], p['w_ffn_up'], idx)
    else:
        mid, w_down = _ffn_up(x, p['w_ffn_gate'], p['w_ffn_up'], idx, w_down=p['w_ffn_down'])
        bf16_weights['down', idx] = w_down[None]
    w_down = bf16_weights['down', idx]
    if ('gate', l) in bf16_weights:
        y = _matmul(mid, w_down, 0, F32, tm_target=1024, tn_target=512 if x.shape[0] <= 1024 else 256)
    else:
        y, w_gate = _matmul(mid, w_down, 0, F32, tm_target=1024, tn_target=256,
                            cast_rows=(p['w_in_t'], l, OFF_GATE, N_BRANCH * d))
        bf16_weights['gate', l] = w_gate[None]
    return y.reshape(b, t, d)


def _run_group(x, mods, p, caches, bias):
    depth = len(mods)
    states = []
    _, h = _norm(x, pre=(mods[0], p['g_pre'][0], 0))
    for l in range(depth):
        mod, g_pre, g_post = mods[l], p['g_pre'][l], p['g_post'][l]
        x, h = _norm(x, _ffn(h, l, 0, p), post=(mod, g_post, 0, 0.5), pre=(mod, g_pre, 1))
        y, st = _mixers(h, l, p, caches, bias)
        states.append(st)
        x, h = _norm(x, y.reshape(x.shape), post=(mod, g_post, 1, 1.0), pre=(mod, g_pre, 2))
        nxt = (mods[l + 1], p['g_pre'][l + 1], 0) if l + 1 < depth else None
        x, h = _norm(x, _ffn(h, l, 1, p), post=(mod, g_post, 2, 0.5), pre=nxt)
    return x, states


def kernel(x_prompt, x_sample, c_prompt, c_sample, cache_a_k, cache_a_v, cache_d_k, cache_d_v, cache_d_logf, state_conv, state_lru, w_ada, b_ada, g_pre, g_post, w_ffn_gate, w_ffn_up, w_ffn_down, w_in, w_out, rel_bias, lambda_qk, diff_norm_g, gmlp_ln_g, gmlp_ln_b, gmlp_ws, gmlp_bs, conv_w, conv_b, w_rgate, b_rgate, w_igate, b_igate, lru_lambda, b_forget):
    depth, d_model, _ = w_in.shape
    bp, seq, _ = x_prompt.shape
    bs, dec_seq, _ = x_sample.shape
    past = cache_a_k.shape[2]
    d_ff = w_ffn_gate.shape[-1]
    p = {'g_pre': g_pre, 'g_post': g_post,
         'w_ffn_gate': w_ffn_gate.reshape(depth * 2, d_model, d_ff),
         'w_ffn_up': w_ffn_up.reshape(depth * 2, d_model, d_ff),
         'w_ffn_down': w_ffn_down.reshape(depth * 2, d_ff, d_model),
         'w_in_t': jnp.swapaxes(w_in, 1, 2),
         'bf16_weights': {},
         'w_out': w_out.reshape(depth * N_BRANCH, BRANCH_WIDTH, d_model),
         'lambda_qk': lambda_qk, 'diff_norm_g': diff_norm_g, 'gmlp_ln_g': gmlp_ln_g,
         'gmlp_ln_b': gmlp_ln_b, 'gmlp_ws': gmlp_ws, 'gmlp_bs': gmlp_bs, 'conv_w': conv_w,
         'conv_b': conv_b, 'w_rgate': w_rgate, 'b_rgate': b_rgate, 'w_igate': w_igate,
         'b_igate': b_igate, 'lru_lambda': lru_lambda, 'b_forget': b_forget}

    mod = _ada(jnp.concatenate([c_prompt, c_sample], axis=0), w_ada, b_ada)
    mod = mod.reshape(depth, bp + bs, 3 * N_SUB, d_model)
    mods_p = [mod[l, :bp] for l in range(depth)]
    mods_s = [mod[l, bp:] for l in range(depth)]

    tq = _tile(seq, 256, LANE)
    assert tq + 1 >= T5_SATURATION and tq % CHUNK == 0
    bias_p = _bias_tiles(rel_bias, tq, tq, [(0, 0), (tq, 0), None], True, seq)
    xp, st_p = _run_group(x_prompt, mods_p, p, None, bias_p)

    bias_s = (_bias_tiles(rel_bias, dec_seq, past, [(past, 0)], False, past + dec_seq),
              _bias_tiles(rel_bias, dec_seq, LANE, [(past, past)], False, past + dec_seq))
    caches = (cache_a_k.transpose(0, 1, 3, 4, 5, 2).reshape(depth, bs, N_HEADS, HEAD_W, past),
              cache_a_v, cache_d_k, cache_d_v,
              cache_d_logf.transpose(0, 1, 3, 2), state_conv, state_lru)
    xs, st_s = _run_group(x_sample, mods_s, p, caches, bias_s)

    np_ = [jnp.stack([st_p[l][j] for l in range(depth)]) for j in range(7)]
    ns_ = [jnp.stack([st_s[l][j] for l in range(depth)]) for j in range(8)]
    return (xp, xs, *np_, *ns_)
```
